```python
import math
import jax, jax.numpy as jnp
from jax import lax
import numpy as np

D_MODEL = 2048
BATCH = 4
SEQ = 2048
DEPTH = 4

HEAD_DIM = 128
ROPE_THETA = 500000.0
ROPE_DIM = HEAD_DIM // 4
Q_BLOCK = 128

NSA_WIDTH = D_MODEL // 2
NSA_HEADS = NSA_WIDTH // HEAD_DIM
NSA_KV_HEADS = 2
NSA_HPG = NSA_HEADS // NSA_KV_HEADS
CMP_LEN = 32
CMP_STRIDE = 16
SLC_LEN = 64
N_SLC = 8
WIN = 512
NSA_KV_COLS = 3 * 2 * NSA_KV_HEADS * HEAD_DIM
NSA_GATE_COLS = 3 * NSA_HEADS
CONV_WIDTH = D_MODEL - NSA_WIDTH
CONV_KERNEL = 31
CONV_GROUPS = 8
EVEN_IN = NSA_WIDTH + NSA_KV_COLS + NSA_GATE_COLS + 2 * CONV_WIDTH

FOX_HEADS = D_MODEL // HEAD_DIM
FOX_IN = 4 * D_MODEL + FOX_HEADS

N_EXPERTS = 16
N_EXPERT_GROUPS = 4
EXPERTS_PER_GROUP = N_EXPERTS // N_EXPERT_GROUPS
TOP_K = 2
D_EXPERT = 512

ALPHA = (2.0 * DEPTH) ** 0.25
BETA = (8.0 * DEPTH) ** -0.25
LN_EPS = 1e-5
NEG = -1e30
BIG = 1e9
TINY = 1e-30

kernel_name = "nsa_conformer_fox_deepnorm_grouped_moe"


def layer_norm(x, g, b):
    xf = x.astype(jnp.float32)
    mu = xf.mean(-1, keepdims=True)
    var = jnp.square(xf - mu).mean(-1, keepdims=True)
    return ((xf - mu) * lax.rsqrt(var + LN_EPS) * g.astype(jnp.float32) + b.astype(jnp.float32)).astype(x.dtype)


def rms_norm(x, g):
    xf = x.astype(jnp.float32)
    return (xf * lax.rsqrt(jnp.mean(xf * xf, -1, keepdims=True) + LN_EPS) * g.astype(jnp.float32)).astype(x.dtype)


def group_norm_tokens(x, g, b):
    B, T, C = x.shape
    xf = x.astype(jnp.float32).reshape(B, T, CONV_GROUPS, C // CONV_GROUPS)
    mu = xf.mean(-1, keepdims=True)
    var = jnp.square(xf - mu).mean(-1, keepdims=True)
    y = ((xf - mu) * lax.rsqrt(var + LN_EPS)).reshape(B, T, C)
    return (y * g.astype(jnp.float32) + b.astype(jnp.float32)).astype(x.dtype)


def masked_softmax(s, mask):
    s = jnp.where(mask, s, NEG)
    m = jnp.max(s, axis=-1, keepdims=True)
    e = jnp.exp(s - m) * mask
    return e / jnp.maximum(e.sum(-1, keepdims=True), TINY)


def rope_tables(pos):
    inv = ROPE_THETA ** (-jnp.arange(0, ROPE_DIM, 2, dtype=jnp.float32) / ROPE_DIM)
    ang = pos.astype(jnp.float32)[:, None] * inv[None, :]
    return jnp.cos(ang), jnp.sin(ang)


def apply_partial_rope(x, cos, sin):
    half = ROPE_DIM // 2
    c = cos[:, None, :].astype(x.dtype)
    s = sin[:, None, :].astype(x.dtype)
    x1, x2, rest = x[..., :half], x[..., half:ROPE_DIM], x[..., ROPE_DIM:]
    return jnp.concatenate([x1 * c - x2 * s, x1 * s + x2 * c, rest], axis=-1)


def compress_blocks(raw, idx_cmp, pos_enc, w1, w2):
    B = raw.shape[0]
    blocks = raw[:, idx_cmp] + pos_enc[None, None, :, None, :].astype(raw.dtype)
    n_cmp = blocks.shape[1]
    flat = blocks.transpose(0, 1, 3, 2, 4).reshape(B, n_cmp, NSA_KV_HEADS, CMP_LEN * HEAD_DIM)
    return jax.nn.gelu(flat @ w1) @ w2


def nsa_conformer_mixer(x, w_in, w_out, cmp_pos, cmp_w1, cmp_w2, conv_w, conv_b, gn_g, gn_b):
    B, T, _ = x.shape
    G, HPG, dh = NSA_KV_HEADS, NSA_HPG, HEAD_DIM
    nq = T // Q_BLOCK
    proj = x @ w_in
    q, kv, gl, glu = jnp.split(proj, [NSA_WIDTH, NSA_WIDTH + NSA_KV_COLS,
                                      NSA_WIDTH + NSA_KV_COLS + NSA_GATE_COLS], axis=-1)
    pos = jnp.arange(T)
    cos, sin = rope_tables(pos)

    q = apply_partial_rope(q.reshape(B, T, NSA_HEADS, dh), cos, sin) * (dh ** -0.5)
    q = q.reshape(B, T, G, HPG, dh).transpose(0, 2, 3, 1, 4)
    kv = kv.reshape(B, T, 3, 2, G, dh)
    gates = jax.nn.sigmoid(gl).reshape(B, nq, Q_BLOCK, G, HPG, 3)

    n_cmp = (T - CMP_LEN) // CMP_STRIDE + 1
    idx_cmp = np.arange(n_cmp)[:, None] * CMP_STRIDE + np.arange(CMP_LEN)[None, :]
    cmp_end = jnp.asarray(idx_cmp[:, -1])
    ccos, csin = rope_tables(cmp_end)
    k_cmp = compress_blocks(kv[:, :, 0, 0], idx_cmp, cmp_pos[0], cmp_w1[0], cmp_w2[0])
    k_cmp = apply_partial_rope(k_cmp, ccos, csin).transpose(0, 2, 1, 3)
    v_cmp = compress_blocks(kv[:, :, 0, 1], idx_cmp, cmp_pos[1], cmp_w1[1], cmp_w2[1]).transpose(0, 2, 1, 3)

    n_slc = T // SLC_LEN
    k_slc = apply_partial_rope(kv[:, :, 1, 0], cos, sin).transpose(0, 2, 1, 3).reshape(B, G, n_slc, SLC_LEN, dh)
    v_slc = kv[:, :, 1, 1].transpose(0, 2, 1, 3).reshape(B, G, n_slc, SLC_LEN, dh)
    c_start = np.arange(n_cmp) * CMP_STRIDE
    s_start = np.arange(n_slc) * SLC_LEN
    cmp_to_slc = jnp.asarray(((c_start[:, None] < s_start[None, :] + SLC_LEN) &
                              (c_start[:, None] + CMP_LEN > s_start[None, :])).astype(np.float32))
    k_top = min(N_SLC, n_slc)

    pad = ((0, 0), (0, 0), (WIN, 0), (0, 0))
    k_win = jnp.pad(apply_partial_rope(kv[:, :, 2, 0], cos, sin).transpose(0, 2, 1, 3), pad)
    v_win = jnp.pad(kv[:, :, 2, 1].transpose(0, 2, 1, 3), pad)

    bi = jnp.arange(B)[:, None, None, None]
    gi = jnp.arange(G)[None, :, None, None]
    blk_ids = jnp.arange(n_slc)

    def block(args):
        i, qb, gb = args
        t = i * Q_BLOCK + jnp.arange(Q_BLOCK)
        gb = gb.transpose(0, 2, 3, 1, 4)
        s = jnp.einsum('bghqd,bgcd->bghqc', qb, k_cmp).astype(jnp.float32)
        p_cmp = masked_softmax(s, cmp_end[None, :] <= t[:, None])
        o_cmp = jnp.einsum('bghqc,bgcd->bghqd', p_cmp.astype(v_cmp.dtype), v_cmp)
        imp = jnp.einsum('bghqc,cj->bgqj', p_cmp, cmp_to_slc)
        cur = t // SLC_LEN
        causal_blk = s_start[None, :] <= t[:, None]
        forced = (blk_ids[None, :] == 0) | (blk_ids[None, :] == cur[:, None]) | (blk_ids[None, :] == cur[:, None] - 1)
        score = jnp.where(forced, BIG, jnp.where(causal_blk, imp, -BIG))
        _, top_idx = lax.top_k(score, k_top)
        k_sel = k_slc[bi, gi, top_idx].reshape(B, G, Q_BLOCK, k_top * SLC_LEN, dh)
        v_sel = v_slc[bi, gi, top_idx].reshape(B, G, Q_BLOCK, k_top * SLC_LEN, dh)
        pos_sel = (top_idx[..., None] * SLC_LEN + jnp.arange(SLC_LEN)).reshape(B, G, Q_BLOCK, k_top * SLC_LEN)
        s = jnp.einsum('bghqd,bgqnd->bghqn', qb, k_sel).astype(jnp.float32)
        p = masked_softmax(s, (pos_sel <= t[None, None, :, None])[:, :, None])
        o_slc = jnp.einsum('bghqn,bgqnd->bghqd', p.astype(v_sel.dtype), v_sel)
        kw = lax.dynamic_slice_in_dim(k_win, i * Q_BLOCK, WIN + Q_BLOCK, axis=2)
        vw = lax.dynamic_slice_in_dim(v_win, i * Q_BLOCK, WIN + Q_BLOCK, axis=2)
        pos_w = i * Q_BLOCK - WIN + jnp.arange(WIN + Q_BLOCK)
        mask_w = (pos_w[None, :] <= t[:, None]) & (pos_w[None, :] > t[:, None] - WIN) & (pos_w[None, :] >= 0)
        s = jnp.einsum('bghqd,bgkd->bghqk', qb, kw).astype(jnp.float32)
        p = masked_softmax(s, mask_w)
        o_win = jnp.einsum('bghqk,bgkd->bghqd', p.astype(vw.dtype), vw)
        return (gb[..., 0:1] * o_cmp + gb[..., 1:2] * o_slc + gb[..., 2:3] * o_win)

    q_blocks = jnp.moveaxis(q.reshape(B, G, HPG, nq, Q_BLOCK, dh), 3, 0)
    g_blocks = jnp.moveaxis(gates, 1, 0)
    o = lax.map(block, (jnp.arange(nq), q_blocks, g_blocks))
    o_nsa = o.transpose(1, 0, 4, 2, 3, 5).reshape(B, T, NSA_WIDTH)

    a, b = jnp.split(glu, 2, axis=-1)
    u = a * jax.nn.sigmoid(b)
    u = jnp.pad(u, ((0, 0), (CONV_KERNEL - 1, 0), (0, 0)))
    u = lax.conv_general_dilated(u, conv_w[:, None, :].astype(u.dtype), window_strides=(1,),
                                 padding='VALID', dimension_numbers=('NWC', 'WIO', 'NWC'),
                                 feature_group_count=CONV_WIDTH) + conv_b
    u = jax.nn.silu(group_norm_tokens(u, gn_g, gn_b))

    return jnp.concatenate([o_nsa, u], axis=-1) @ w_out


def fox_mixer(x, w_in, f_bias, q_gain, k_gain, w_out):
    B, T, _ = x.shape
    H, dh = FOX_HEADS, HEAD_DIM
    nq = T // Q_BLOCK
    q, k, v, og, fl = jnp.split(x @ w_in, [D_MODEL, 2 * D_MODEL, 3 * D_MODEL, 4 * D_MODEL], axis=-1)
    q = rms_norm(q.reshape(B, T, H, dh), q_gain).transpose(0, 2, 1, 3) * (dh ** -0.5)
    k = rms_norm(k.reshape(B, T, H, dh), k_gain).transpose(0, 2, 1, 3)
    v = v.reshape(B, T, H, dh).transpose(0, 2, 1, 3)
    log_f = jax.nn.log_sigmoid((fl + f_bias).astype(jnp.float32))
    c = jnp.cumsum(log_f, axis=1).transpose(0, 2, 1)
    kpos = jnp.arange(T)

    def block(args):
        i, qb, cq = args
        t = i * Q_BLOCK + jnp.arange(Q_BLOCK)
        s = jnp.einsum('bhqd,bhkd->bhqk', qb, k).astype(jnp.float32) + cq[..., :, None] - c[:, :, None, :]
        p = masked_softmax(s, kpos[None, :] <= t[:, None])
        return jnp.einsum('bhqk,bhkd->bhqd', p.astype(v.dtype), v)

    q_blocks = jnp.moveaxis(q.reshape(B, H, nq, Q_BLOCK, dh), 2, 0)
    c_blocks = jnp.moveaxis(c.reshape(B, H, nq, Q_BLOCK), 2, 0)
    o = lax.map(block, (jnp.arange(nq), q_blocks, c_blocks))
    o = o.transpose(1, 0, 3, 2, 4).reshape(B, T, D_MODEL)
    return (o * jax.nn.sigmoid(og)) @ w_out


def grouped_moe(x, router_w, router_bias, w_gate, w_up, w_down):
    B, T, D = x.shape
    xt = x.reshape(B * T, D)
    aff = jax.nn.sigmoid((xt @ router_w).astype(jnp.float32))
    sel = aff + router_bias.astype(jnp.float32)
    grp_score = lax.top_k(sel.reshape(-1, N_EXPERT_GROUPS, EXPERTS_PER_GROUP), 2)[0].sum(-1)
    grp_onehot = jax.nn.one_hot(jnp.argmax(grp_score, axis=-1), N_EXPERT_GROUPS, dtype=jnp.float32)
    expert_ok = jnp.repeat(grp_onehot, EXPERTS_PER_GROUP, axis=-1) > 0
    _, top_idx = lax.top_k(jnp.where(expert_ok, sel, NEG), TOP_K)
    top_aff = jnp.take_along_axis(aff, top_idx, axis=-1)
    wts = top_aff / top_aff.sum(-1, keepdims=True)
    gates = jnp.sum(jax.nn.one_hot(top_idx, N_EXPERTS, dtype=jnp.float32) * wts[..., None], axis=1)
    h = jax.nn.silu(jnp.einsum('nd,edf->nef', xt, w_gate)) * jnp.einsum('nd,edf->nef', xt, w_up)
    h = h * gates[..., None].astype(h.dtype)
    return jnp.einsum('nef,efd->nd', h, w_down).reshape(B, T, D)


def setup_inputs(seed: int = 0) -> dict:
    key = jax.random.key(seed)
    ks = jax.random.split(key, 32)
    n_even = (DEPTH + 1) // 2
    n_odd = DEPTH // 2
    D = D_MODEL

    def nrm(k, shape, scale):
        return jax.random.normal(k, shape, jnp.float32) * scale

    return {
        "x": nrm(ks[0], (BATCH, SEQ, D), 1.0),
        "even_w_in": nrm(ks[1], (n_even, D, EVEN_IN), D ** -0.5),
        "even_w_out": nrm(ks[2], (n_even, D, D), BETA * D ** -0.5),
        "nsa_cmp_pos": nrm(ks[3], (n_even, 2, CMP_LEN, HEAD_DIM), 0.1),
        "nsa_cmp_w1": nrm(ks[4], (n_even, 2, CMP_LEN * HEAD_DIM, HEAD_DIM), (CMP_LEN * HEAD_DIM) ** -0.5),
        "nsa_cmp_w2": nrm(ks[5], (n_even, 2, HEAD_DIM, HEAD_DIM), HEAD_DIM ** -0.5),
        "conv_w": nrm(ks[6], (n_even, CONV_KERNEL, CONV_WIDTH), CONV_KERNEL ** -0.5),
        "conv_b": nrm(ks[7], (n_even, CONV_WIDTH), 0.01),
        "conv_gn_g": 1.0 + nrm(ks[8], (n_even, CONV_WIDTH), 0.01),
        "conv_gn_b": nrm(ks[9], (n_even, CONV_WIDTH), 0.01),
        "fox_w_in": nrm(ks[10], (n_odd, D, FOX_IN), D ** -0.5),
        "fox_f_bias": 3.0 + nrm(ks[11], (n_odd, FOX_HEADS), 0.1),
        "fox_q_gain": 1.0 + nrm(ks[12], (n_odd, HEAD_DIM), 0.01),
        "fox_k_gain": 1.0 + nrm(ks[13], (n_odd, HEAD_DIM), 0.01),
        "fox_w_out": nrm(ks[14], (n_odd, D, D), BETA * D ** -0.5),
        "ln_mix_g": 1.0 + nrm(ks[15], (DEPTH, D), 0.01),
        "ln_mix_b": nrm(ks[16], (DEPTH, D), 0.01),
        "ln_ffn_g": 1.0 + nrm(ks[17], (DEPTH, D), 0.01),
        "ln_ffn_b": nrm(ks[18], (DEPTH, D), 0.01),
        "router_w": nrm(ks[19], (D, N_EXPERTS), D ** -0.5),
        "router_bias": nrm(ks[20], (N_EXPERTS,), 0.01),
        "exp_w_gate": nrm(ks[21], (DEPTH, N_EXPERTS, D, D_EXPERT), D ** -0.5),
        "exp_w_up": nrm(ks[22], (DEPTH, N_EXPERTS, D, D_EXPERT), D ** -0.5),
        "exp_w_down": nrm(ks[23], (DEPTH, N_EXPERTS, D_EXPERT, D), BETA * D_EXPERT ** -0.5),
    }


def reference(x, even_w_in, even_w_out, nsa_cmp_pos, nsa_cmp_w1, nsa_cmp_w2, conv_w, conv_b,
              conv_gn_g, conv_gn_b, fox_w_in, fox_f_bias, fox_q_gain, fox_k_gain, fox_w_out,
              ln_mix_g, ln_mix_b, ln_ffn_g, ln_ffn_b, router_w, router_bias,
              exp_w_gate, exp_w_up, exp_w_down):
    for layer in range(DEPTH):
        j = layer // 2
        if layer % 2 == 0:
            h = nsa_conformer_mixer(x, even_w_in[j], even_w_out[j], nsa_cmp_pos[j], nsa_cmp_w1[j],
                                    nsa_cmp_w2[j], conv_w[j], conv_b[j], conv_gn_g[j], conv_gn_b[j])
        else:
            h = fox_mixer(x, fox_w_in[j], fox_f_bias[j], fox_q_gain[j], fox_k_gain[j], fox_w_out[j])
        x = layer_norm(ALPHA * x + h, ln_mix_g[layer], ln_mix_b[layer])
        f = grouped_moe(x, router_w, router_bias, exp_w_gate[layer], exp_w_up[layer], exp_w_down[layer])
        x = layer_norm(ALPHA * x + f, ln_ffn_g[layer], ln_ffn_b[layer])
    return x
```

```python
import functools

import numpy as np
import jax
import jax.numpy as jnp
from jax import lax
from jax.experimental import pallas as pl
from jax.experimental.pallas import tpu as pltpu

F32 = jnp.float32
BF16 = jnp.bfloat16

HEAD_DIM = 128
ROPE_THETA = 500000.0
ROPE_DIM = HEAD_DIM // 4
ROPE_HALF = ROPE_DIM // 2
Q_BLOCK = 128

NSA_KV_HEADS = 2
NSA_HPG = 4
CMP_LEN = 32
CMP_STRIDE = 16
SLC_LEN = 64
N_SLC = 8
WIN = 512
CONV_KERNEL = 31
CONV_GROUPS = 8

N_EXPERTS = 16
N_EXPERT_GROUPS = 4
EXPERTS_PER_GROUP = 4
TOP_K = 2

LN_EPS = 1e-5
NEG = -1e30
BIG = 1e9
TINY = 1e-30

LANES = 128
VMEM_LIMIT = 56 * 1024 * 1024

MOE_TM = 256


def _cparams(sem, vmem=VMEM_LIMIT):
    return pltpu.CompilerParams(dimension_semantics=sem, vmem_limit_bytes=vmem)


def _dot(a, b):
    return jnp.dot(a, b, preferred_element_type=F32)


def _dot_nt(a, b):
    return lax.dot_general(a, b, (((1,), (1,)), ((), ())), preferred_element_type=F32)


def _sigmoid(x):
    return 1.0 / (1.0 + jnp.exp(-x))


def _linear_kernel(x_ref, w_ref, o_ref, wb_ref):
    @pl.when(pl.program_id(1) == 0)
    def _():
        wb_ref[...] = w_ref[...].astype(BF16)

    o_ref[...] = _dot(x_ref[...].astype(BF16), wb_ref[...]).astype(o_ref.dtype)


def linear(x, w, n_cols, out_dtype, *, layer=None, tm=1024, tn=512, name="linear"):
    M, K = x.shape
    tm = min(tm, M)
    tn = min(tn, n_cols)
    assert M % tm == 0 and n_cols % tn == 0
    row_block = 0
    if layer is not None:
        w = w.reshape(-1, w.shape[-1])
        row_block = layer
    w_spec = pl.BlockSpec((K, tn), lambda j, i: (row_block, j))
    return pl.pallas_call(
        _linear_kernel,
        out_shape=jax.ShapeDtypeStruct((M, n_cols), out_dtype),
        grid=(n_cols // tn, M // tm),
        in_specs=[
            pl.BlockSpec((tm, K), lambda j, i: (i, 0)),
            w_spec,
        ],
        out_specs=pl.BlockSpec((tm, tn), lambda j, i: (i, j)),
        scratch_shapes=[pltpu.VMEM((K, tn), BF16)],
        compiler_params=_cparams(("arbitrary", "arbitrary")),
        name=name,
    )(x, w)


def _layer_norm_rows(y, g, b):
    mu = jnp.mean(y, axis=-1, keepdims=True)
    d = y - mu
    var = jnp.mean(d * d, axis=-1, keepdims=True)
    return d * lax.rsqrt(var + LN_EPS) * g + b


def _outproj_ln_kernel(a1_ref, a2_ref, w_ref, x_ref, g_ref, b_ref, xo_ref, xb_ref, wb_ref, *, alpha):
    @pl.when(pl.program_id(0) == 0)
    def _():
        wb_ref[...] = w_ref[...].astype(BF16)

    half = a1_ref.shape[1]
    h = _dot(a1_ref[...].astype(BF16), wb_ref[0:half, :])
    h = h + _dot(a2_ref[...].astype(BF16), wb_ref[half:2 * half, :])
    out = _layer_norm_rows(alpha * x_ref[...] + h, g_ref[...], b_ref[...])
    xo_ref[...] = out
    xb_ref[...] = out.astype(BF16)


def outproj_ln(a1, a2, a2_col_block, w, layer, x, g, b, alpha, *, tm=256):
    M, D = x.shape
    half = D // 2
    return pl.pallas_call(
        functools.partial(_outproj_ln_kernel, alpha=alpha),
        out_shape=(jax.ShapeDtypeStruct((M, D), F32), jax.ShapeDtypeStruct((M, D), BF16)),
        grid=(M // tm,),
        in_specs=[
            pl.BlockSpec((tm, half), lambda i: (i, 0)),
            pl.BlockSpec((tm, half), lambda i: (i, a2_col_block)),
            pl.BlockSpec((None, D, D), lambda i: (layer, 0, 0), pipeline_mode=pl.Buffered(1)),
            pl.BlockSpec((tm, D), lambda i: (i, 0)),
            pl.BlockSpec((1, D), lambda i: (0, 0)),
            pl.BlockSpec((1, D), lambda i: (0, 0)),
        ],
        out_specs=(pl.BlockSpec((tm, D), lambda i: (i, 0)), pl.BlockSpec((tm, D), lambda i: (i, 0))),
        scratch_shapes=[pltpu.VMEM((D, D), BF16)],
        compiler_params=_cparams(("arbitrary",)),
        name="outproj_ln",
    )(a1, a2, w, x, g.reshape(1, D), b.reshape(1, D))


ROW_DMA_UNROLL = 8


def _row_copies(dest_ref, n_tok, base, tm, make):
    def body(r, carry):
        for k in range(TOP_K):
            make(r, k, dest_ref[k * n_tok + base + r])
        return carry

    lax.fori_loop(0, tm, body, 0, unroll=ROW_DMA_UNROLL)


def _combine_ln_kernel(dest_ref, x_ref, w_ref, g_ref, b_ref, y_hbm, xo_ref, xb_ref, buf_ref, sem, *,
                       alpha, tm, n_tok):
    i = pl.program_id(0)
    n_steps = pl.num_programs(0)

    def copies(step, slot, start):
        def make(r, k, d):
            cp = pltpu.make_async_copy(y_hbm.at[pl.ds(d, 1)], buf_ref.at[slot, k, pl.ds(r, 1)], sem.at[slot])
            if start:
                cp.start(priority=k)
            else:
                cp.wait()
        _row_copies(dest_ref, n_tok, step * tm, tm, make)

    @pl.when(i == 0)
    def _():
        copies(0, 0, True)

    @pl.when(i + 1 < n_steps)
    def _():
        copies(i + 1, (i + 1) % 2, True)

    slot = i % 2
    copies(i, slot, False)
    f = w_ref[:, 0:1] * buf_ref[slot, 0] + w_ref[:, 1:2] * buf_ref[slot, 1]
    out = _layer_norm_rows(alpha * x_ref[...] + f, g_ref[...], b_ref[...])
    xo_ref[...] = out
    xb_ref[...] = out.astype(BF16)


def combine_ln(x, y, dest, wts, g, b, alpha, *, tm=256):
    N, D = x.shape
    row = lambda i, d: (i, 0)
    grid_spec = pltpu.PrefetchScalarGridSpec(
        num_scalar_prefetch=1,
        grid=(N // tm,),
        in_specs=[
            pl.BlockSpec((tm, D), row),
            pl.BlockSpec((tm, TOP_K), row),
            pl.BlockSpec((1, D), lambda i, d: (0, 0)),
            pl.BlockSpec((1, D), lambda i, d: (0, 0)),
            pl.BlockSpec(memory_space=pl.ANY),
        ],
        out_specs=(pl.BlockSpec((tm, D), row), pl.BlockSpec((tm, D), row)),
        scratch_shapes=[pltpu.VMEM((2, TOP_K, tm, D), F32), pltpu.SemaphoreType.DMA((2,))],
    )
    return pl.pallas_call(
        functools.partial(_combine_ln_kernel, alpha=alpha, tm=tm, n_tok=N),
        out_shape=(jax.ShapeDtypeStruct((N, D), F32), jax.ShapeDtypeStruct((N, D), BF16)),
        grid_spec=grid_spec,
        compiler_params=_cparams(("arbitrary",)),
        name="moe_combine_ln",
    )(dest, x, wts, g.reshape(1, D), b.reshape(1, D), y)


def _rms(x, gain):
    return x * lax.rsqrt(jnp.mean(x * x, axis=-1, keepdims=True) + LN_EPS) * gain


CUM_CHUNK = 256


def _decay_kernel(fl_ref, bias_ref, c_ref):
    T = fl_ref.shape[0]
    tri = (lax.broadcasted_iota(jnp.int32, (CUM_CHUNK, CUM_CHUNK), 0)
           >= lax.broadcasted_iota(jnp.int32, (CUM_CHUNK, CUM_CHUNK), 1)).astype(F32)
    carry = jnp.zeros((1, LANES), F32)
    for c in range(T // CUM_CHUNK):
        z = fl_ref[c * CUM_CHUNK:(c + 1) * CUM_CHUNK, :] + bias_ref[...]
        log_f = jnp.minimum(z, 0.0) - jnp.log1p(jnp.exp(-jnp.abs(z)))
        cs = jnp.dot(tri, log_f, preferred_element_type=F32, precision=lax.Precision.HIGHEST) + carry
        c_ref[c * CUM_CHUNK:(c + 1) * CUM_CHUNK, :] = cs
        carry = cs[CUM_CHUNK - 1:CUM_CHUNK, :]


def fox_decay(fl, bias, B, T):
    assert T % CUM_CHUNK == 0
    return pl.pallas_call(
        _decay_kernel,
        out_shape=jax.ShapeDtypeStruct(fl.shape, F32),
        grid=(B,),
        in_specs=[pl.BlockSpec((T, LANES), lambda b: (b, 0)), pl.BlockSpec((1, LANES), lambda b: (0, 0))],
        out_specs=pl.BlockSpec((T, LANES), lambda b: (b, 0)),
        compiler_params=_cparams(("arbitrary",)),
        name="fox_decay",
    )(fl, bias.reshape(1, LANES))


LOG2E = 1.4426950408889634
FOX_FAST_BOUND = 38.0
FOX_AUX = 3


def _split3(a):
    hi = a.astype(BF16).astype(F32)
    r = a - hi
    mid = r.astype(BF16).astype(F32)
    return hi, mid, (r - mid).astype(BF16).astype(F32)


def _aux_lanes(pieces, ones_first, n):
    lane = lax.broadcasted_iota(jnp.int32, (n, LANES), 1)
    p0 = FOX_AUX if ones_first else 0
    o0 = 0 if ones_first else FOX_AUX
    out = jnp.where((lane >= o0) & (lane < o0 + FOX_AUX), 1.0, 0.0)
    for i, piece in enumerate(pieces):
        out = out + jnp.where(lane == p0 + i, piece, 0.0)
    return out


def _fox_kernel(fast_ref, q_ref, k_ref, v_ref, og_ref, ckrow_ref, ckcol_ref, cq_ref, bound_ref, qg_ref, kg_ref,
                o_ref, kx_ref, vx_ref, *, tq, tk, hb):
    qi = pl.program_id(2)
    dh = HEAD_DIM
    T = k_ref.shape[1]

    @pl.when(qi == 0)
    def _():
        for h in range(hb):
            kx_ref[h, :, 0:dh] = _rms(k_ref[0, :, h * dh:(h + 1) * dh].astype(F32), kg_ref[...]).astype(BF16)
            kx_ref[h, :, dh:2 * dh] = _aux_lanes(_split3(-LOG2E * ckcol_ref[0, 0, :, h:h + 1]), False, T
                                                 ).astype(BF16)
            vx_ref[h, :, 0:dh] = v_ref[0, :, h * dh:(h + 1) * dh].astype(BF16)
            vx_ref[h, :, dh:2 * dh] = jnp.ones((T, dh), BF16)

    row = qi * tq + lax.broadcasted_iota(jnp.int32, (tq, tk), 0)
    lane = lax.broadcasted_iota(jnp.int32, (tq, tk), 1)
    n_full = (qi * tq) // tk

    def finish(accs):
        for h in range(hb):
            o = accs[h][:, 0:dh] / jnp.maximum(accs[h][:, dh:2 * dh], TINY)
            gate = _sigmoid(og_ref[0, :, h * dh:(h + 1) * dh].astype(F32))
            o_ref[0, :, h * dh:(h + 1) * dh] = (o * gate).astype(o_ref.dtype)

    def q_normed(h):
        return _rms(q_ref[0, :, h * dh:(h + 1) * dh].astype(F32), qg_ref[...]) * (dh ** -0.5)

    @pl.when(fast_ref[0] == 1)
    def _():
        qx = []
        for h in range(hb):
            r = LOG2E * (cq_ref[0, 0, :, h:h + 1] - bound_ref[0:1, 0:1])
            qx.append(jnp.concatenate([(q_normed(h) * LOG2E).astype(BF16),
                                       _aux_lanes(_split3(r), True, tq).astype(BF16)], axis=1))

        def step(j, accs, masked):
            start = pl.multiple_of(j * tk, tk)
            out = []
            for h in range(hb):
                s = _dot_nt(qx[h], kx_ref[h, pl.ds(start, tk), :])
                if masked:
                    s = jnp.where(j * tk + lane <= row, s, NEG)
                out.append(accs[h] + _dot(jnp.exp2(s).astype(BF16), vx_ref[h, pl.ds(start, tk), :]))
            return tuple(out)

        accs = tuple(jnp.zeros((tq, 2 * dh), F32) for _ in range(hb))
        accs = lax.fori_loop(0, n_full, lambda j, a: step(j, a, False), accs)
        finish(step(n_full, accs, True))

    @pl.when(fast_ref[0] == 0)
    def _():
        qs = [q_normed(h).astype(BF16) for h in range(hb)]

        def step(j, carry, masked):
            start = pl.multiple_of(j * tk, tk)
            out = []
            for h in range(hb):
                m, acc = carry[h]
                s = _dot_nt(qs[h], kx_ref[h, pl.ds(start, tk), 0:dh]) - ckrow_ref[0, h, pl.ds(j, 1), :]
                if masked:
                    s = jnp.where(j * tk + lane <= row, s, NEG)
                m_new = jnp.maximum(m, jnp.max(s, axis=-1, keepdims=True))
                p = jnp.exp(s - m_new).astype(BF16)
                acc = jnp.exp(m - m_new) * acc + _dot(p, vx_ref[h, pl.ds(start, tk), :])
                out.append((m_new, acc))
            return tuple(out)

        init = tuple((jnp.full((tq, 1), NEG, F32), jnp.zeros((tq, 2 * dh), F32)) for _ in range(hb))
        carry = lax.fori_loop(0, n_full, lambda j, c: step(j, c, False), init)
        finish([c[1] for c in step(n_full, carry, True)])


def fox_attention(proj, c, q_gain, k_gain, B, T, H, *, tq=512, tk=512, hb=4):
    nq = T // tq
    nk = T // tk
    dh = HEAD_DIM
    hg = H // hb
    c_row = c.transpose(0, 2, 1).reshape(B, H, nk, tk)
    c_col = c.reshape(B, T, hg, hb).transpose(0, 2, 1, 3)
    bound = jnp.max(jnp.abs(q_gain)) * jnp.max(jnp.abs(k_gain)) * (dh ** 0.5) * 1.01
    fast = (bound <= FOX_FAST_BOUND).astype(jnp.int32).reshape(1)
    full = lambda off: pl.BlockSpec((1, T, hb * dh), lambda b, h, i, f: (b, 0, off + h))
    tile = lambda off: pl.BlockSpec((1, tq, hb * dh), lambda b, h, i, f: (b, i, off + h))
    vec = pl.BlockSpec((1, dh), lambda b, h, i, f: (0, 0))
    grid_spec = pltpu.PrefetchScalarGridSpec(
        num_scalar_prefetch=1,
        grid=(B, hg, nq),
        in_specs=[
            tile(0), full(hg), full(2 * hg), tile(3 * hg),
            pl.BlockSpec((1, hb, nk, tk), lambda b, h, i, f: (b, h, 0, 0)),
            pl.BlockSpec((1, 1, T, hb), lambda b, h, i, f: (b, h, 0, 0)),
            pl.BlockSpec((1, 1, tq, hb), lambda b, h, i, f: (b, h, i, 0)),
            vec, vec, vec,
        ],
        out_specs=pl.BlockSpec((1, tq, hb * dh), lambda b, h, i, f: (b, i, h)),
        scratch_shapes=[pltpu.VMEM((hb, T, 2 * dh), BF16), pltpu.VMEM((hb, T, 2 * dh), BF16)],
    )
    return pl.pallas_call(
        functools.partial(_fox_kernel, tq=tq, tk=tk, hb=hb),
        out_shape=jax.ShapeDtypeStruct((B, T, H * dh), BF16),
        grid_spec=grid_spec,
        compiler_params=_cparams(("arbitrary", "arbitrary", "arbitrary")),
        name="fox_attention",
    )(fast, proj, proj, proj, proj, c_row, c_col, c_col, jnp.full((1, dh), bound, F32),
      q_gain.reshape(1, dh), k_gain.reshape(1, dh))


def _rope(x, cos, sin_lo, sin_hi):
    return (x * cos + pltpu.roll(x, LANES - ROPE_HALF, 1) * sin_lo
            + pltpu.roll(x, ROPE_HALF, 1) * sin_hi)


def _gelu_tanh(x):
    return 0.5 * x * (1.0 + jnp.tanh(0.7978845608028654 * (x + 0.044715 * (x * x * x))))


def _compress_kernel(raw_ref, pos_ref, w1_ref, w2_ref, cos_ref, slo_ref, shi_ref, o_ref, *, n_cmp):
    nb = o_ref.shape[-2]
    half = CMP_LEN // 2
    acc_a = jnp.zeros((nb, HEAD_DIM), F32)
    acc_b = jnp.zeros((nb, HEAD_DIM), F32)
    for l in range(half):
        rl = raw_ref[0, pl.ds(l, nb, stride=CMP_STRIDE), :]
        wa = w1_ref[0, l * HEAD_DIM:(l + 1) * HEAD_DIM, :].astype(BF16)
        wb = w1_ref[0, (half + l) * HEAD_DIM:(half + l + 1) * HEAD_DIM, :].astype(BF16)
        acc_a = acc_a + _dot((rl + pos_ref[0, l:l + 1, :]).astype(BF16), wa)
        acc_b = acc_b + _dot((rl + pos_ref[0, half + l:half + l + 1, :]).astype(BF16), wb)
    pre = acc_a + pltpu.roll(acc_b, nb - 1, 0)
    out = _dot(_gelu_tanh(pre).astype(BF16), w2_ref[0].astype(BF16))
    roped = _rope(out, cos_ref[...], slo_ref[...], shi_ref[...])
    out = jnp.where(pl.program_id(1) == 0, roped, out)
    rows = lax.broadcasted_iota(jnp.int32, out.shape, 0)
    o_ref[0, 0, 0] = jnp.where(rows < n_cmp, out, 0.0)


def compress(qkv, pos, w1, w2, tabs, B, T, kv_block0):
    G = NSA_KV_HEADS
    nb = T // CMP_STRIDE
    n_cmp = (T - CMP_LEN) // CMP_STRIDE + 1
    cos, slo, shi = tabs
    tab = pl.BlockSpec((nb, HEAD_DIM), lambda b, kv, g: (0, 0))
    return pl.pallas_call(
        functools.partial(_compress_kernel, n_cmp=n_cmp),
        out_shape=jax.ShapeDtypeStruct((B, 2, G, nb, HEAD_DIM), F32),
        grid=(B, 2, G),
        in_specs=[
            pl.BlockSpec((1, T, HEAD_DIM), lambda b, kv, g: (b, 0, kv_block0 + kv * G + g)),
            pl.BlockSpec((1, CMP_LEN, HEAD_DIM), lambda b, kv, g: (kv, 0, 0)),
            pl.BlockSpec((1, CMP_LEN * HEAD_DIM, HEAD_DIM), lambda b, kv, g: (kv, 0, 0)),
            pl.BlockSpec((1, HEAD_DIM, HEAD_DIM), lambda b, kv, g: (kv, 0, 0)),
            tab, tab, tab,
        ],
        out_specs=pl.BlockSpec((1, 1, 1, nb, HEAD_DIM), lambda b, kv, g: (b, kv, g, 0, 0)),
        compiler_params=_cparams(("arbitrary", "arbitrary", "arbitrary")),
        name="nsa_compress",
    )(qkv, pos, w1, w2, cos, slo, shi)


SLC_CHUNK = 512
WIN_SPAN = WIN + Q_BLOCK
NSA_QB = 2


def _select_blocks(imp, qi, n_slc):
    QB = Q_BLOCK
    n_rows = -(-n_slc // 8) * 8
    imp_t = jnp.transpose(imp)[0:n_rows]
    blk = lax.broadcasted_iota(jnp.int32, (n_rows, QB), 0)
    t = qi * QB + lax.broadcasted_iota(jnp.int32, (n_rows, QB), 1)
    cur = jnp.right_shift(t, 6)
    forced = (blk == 0) | (blk == cur) | (blk == cur - 1)
    score = jnp.where(forced, BIG, jnp.where(blk * SLC_LEN <= t, imp_t, -BIG))
    score = jnp.where(blk < n_slc, score, -2.0 * BIG)
    cnt = jnp.zeros((n_rows, QB), F32)
    for jp in range(n_slc):
        r = score[jp:jp + 1, :]
        cnt = cnt + ((r > score) | ((r == score) & (jp < blk))).astype(F32)
    sel_t = ((cnt < float(min(N_SLC, n_slc))) & (blk < n_slc)).astype(F32)
    if n_rows < LANES:
        sel_t = jnp.concatenate([sel_t, jnp.zeros((LANES - n_rows, QB), F32)], axis=0)
    return jnp.transpose(sel_t).astype(BF16)


def _nsa_kernel(q_ref, ks_ref, vs_ref, kw_ref, vw_ref, kc_ref, vc_ref, gl_ref,
                cq_ref, sloq_ref, shiq_ref, ck_ref, slok_ref, shik_ref, c2s_ref, ex_ref,
                o_ref, ksb_ref, vsx_ref, kwb_ref, vwx_ref, bias_ref, *, n_slc, n_cmp):
    step = pl.program_id(2)
    QB, HPG, dh = Q_BLOCK, NSA_HPG, HEAD_DIM
    T = ks_ref.shape[1]

    @pl.when(step == 0)
    def _():
        ksb_ref[...] = _rope(ks_ref[0], ck_ref[...], slok_ref[...], shik_ref[...]).astype(BF16)
        kwb_ref[...] = _rope(kw_ref[0], ck_ref[...], slok_ref[...], shik_ref[...]).astype(BF16)
        ones = jnp.ones((T, dh), BF16)
        vsx_ref[:, 0:dh] = vs_ref[0].astype(BF16)
        vsx_ref[:, dh:2 * dh] = ones
        vwx_ref[:, 0:dh] = vw_ref[0].astype(BF16)
        vwx_ref[:, dh:2 * dh] = ones

    qis = [step * NSA_QB + b for b in range(NSA_QB)]
    rows = lambda b, h: slice((b * HPG + h) * QB, (b * HPG + h + 1) * QB)
    qrows = lambda b: slice(b * QB, (b + 1) * QB)
    pieces = []
    for b in range(NSA_QB):
        for h in range(HPG):
            x = q_ref[0, qrows(b), h * dh:(h + 1) * dh]
            pieces.append(_rope(x, cq_ref[qrows(b), :], sloq_ref[qrows(b), :], shiq_ref[qrows(b), :]) * (dh ** -0.5))
    q = jnp.concatenate(pieces, axis=0).astype(BF16)

    lane = lax.broadcasted_iota(jnp.int32, (QB, LANES), 1)
    sub = lax.broadcasted_iota(jnp.int32, (QB, LANES), 0)

    kc = kc_ref[0, 0, 0].astype(BF16)
    vc = vc_ref[0, 0, 0].astype(BF16)
    sc = _dot_nt(q, kc)
    o_cmp, sels = {}, []
    for b in range(NSA_QB):
        t = qis[b] * QB + sub
        mask_c = (lane * CMP_STRIDE + (CMP_LEN - 1) <= t) & (lane < n_cmp)
        mask_cf = mask_c.astype(F32)
        imp_c = jnp.zeros((QB, LANES), F32)
        for h in range(HPG):
            s = jnp.where(mask_c, sc[rows(b, h)], NEG)
            e = jnp.exp(s - jnp.max(s, axis=-1, keepdims=True)) * mask_cf
            p = e / jnp.maximum(jnp.sum(e, axis=-1, keepdims=True), TINY)
            imp_c = imp_c + p
            o_cmp[b, h] = _dot(p.astype(BF16), vc)
        imp = jnp.dot(imp_c, c2s_ref[...], preferred_element_type=F32, precision=lax.Precision.HIGHEST)
        sels.append(_select_blocks(imp, qis[b], n_slc))

    sel_all = jnp.concatenate(sels, axis=0)
    for c in range(T // SLC_CHUNK):
        selx = _dot(sel_all, ex_ref[:, c * SLC_CHUNK:(c + 1) * SLC_CHUNK])
        kpos = c * SLC_CHUNK + lax.broadcasted_iota(jnp.int32, (QB, SLC_CHUNK), 1)
        for b in range(NSA_QB):
            tq_ = qis[b] * QB + lax.broadcasted_iota(jnp.int32, (QB, SLC_CHUNK), 0)
            bias_ref[b, c] = jnp.where((selx[qrows(b)] > 0.5) & (kpos <= tq_), 0.0, NEG)

    o_win = {}
    for b in range(NSA_QB):
        wstart = pl.multiple_of(jnp.clip(qis[b] * QB - WIN, 0, T - WIN_SPAN), Q_BLOCK)
        qb = q[b * HPG * QB:(b + 1) * HPG * QB]
        sw = _dot_nt(qb, kwb_ref[pl.ds(wstart, WIN_SPAN), :])
        kpos = wstart + lax.broadcasted_iota(jnp.int32, (QB, WIN_SPAN), 1)
        tw = qis[b] * QB + lax.broadcasted_iota(jnp.int32, (QB, WIN_SPAN), 0)
        bias_w = jnp.where((kpos <= tw) & (kpos > tw - WIN), 0.0, NEG)
        pw = []
        for h in range(HPG):
            sh = sw[h * QB:(h + 1) * QB] + bias_w
            pw.append(jnp.exp(sh - jnp.max(sh, axis=-1, keepdims=True)).astype(BF16))
        ow = _dot(jnp.concatenate(pw, axis=0), vwx_ref[pl.ds(wstart, WIN_SPAN), :])
        for h in range(HPG):
            acc_w = ow[h * QB:(h + 1) * QB]
            o_win[b, h] = acc_w[:, 0:dh] / jnp.maximum(acc_w[:, dh:2 * dh], TINY)

    chains = [(b, h) for b in range(NSA_QB) for h in range(HPG)]

    def slc_body(c, carry):
        start = pl.multiple_of(c * SLC_CHUNK, SLC_CHUNK)
        s = _dot_nt(q, ksb_ref[pl.ds(start, SLC_CHUNK), :])
        ms, alphas, ps = [], [], []
        for n, (b, h) in enumerate(chains):
            sh = s[rows(b, h)] + bias_ref[b, c]
            m_new = jnp.maximum(carry[n][0], jnp.max(sh, axis=-1, keepdims=True))
            alphas.append(jnp.exp(carry[n][0] - m_new))
            ps.append(jnp.exp(sh - m_new).astype(BF16))
            ms.append(m_new)
        pv = _dot(jnp.concatenate(ps, axis=0), vsx_ref[pl.ds(start, SLC_CHUNK), :])
        return tuple((ms[n], alphas[n] * carry[n][1] + pv[rows(b, h)]) for n, (b, h) in enumerate(chains))

    init = tuple((jnp.full((QB, 1), NEG, F32), jnp.zeros((QB, 2 * dh), F32)) for _ in chains)
    n_chunks = (qis[-1] * QB + QB + SLC_CHUNK - 1) // SLC_CHUNK
    slc = lax.fori_loop(0, n_chunks, slc_body, init)

    for b in range(NSA_QB):
        gates = _sigmoid(gl_ref[0, qrows(b), :])
        for h in range(HPG):
            acc_s = slc[b * HPG + h][1]
            o_s = acc_s[:, 0:dh] / jnp.maximum(acc_s[:, dh:2 * dh], TINY)
            out = (gates[:, 3 * h:3 * h + 1] * o_cmp[b, h] + gates[:, 3 * h + 1:3 * h + 2] * o_s
                   + gates[:, 3 * h + 2:3 * h + 3] * o_win[b, h])
            o_ref[0, qrows(b), h * dh:(h + 1) * dh] = out.astype(o_ref.dtype)


def nsa_attention(qkv, cmp_kv, gl, tabs_q, B, T):
    G, HPG, dh = NSA_KV_HEADS, NSA_HPG, HEAD_DIM
    tq = NSA_QB * Q_BLOCK
    nb = T // CMP_STRIDE
    n_cmp = (T - CMP_LEN) // CMP_STRIDE + 1
    n_slc = T // SLC_LEN
    assert nb == LANES and n_slc <= LANES and T % SLC_CHUNK == 0 and T >= WIN_SPAN and T % tq == 0
    kvb = (HPG * G)

    c_start = np.arange(nb) * CMP_STRIDE
    s_start = np.arange(LANES) * SLC_LEN
    c2s = ((c_start[:, None] < s_start[None, :] + SLC_LEN) & (c_start[:, None] + CMP_LEN > s_start[None, :])
           & (np.arange(nb)[:, None] < n_cmp) & (np.arange(LANES)[None, :] < n_slc)).astype(np.float32)
    expand = (np.arange(T)[None, :] // SLC_LEN == np.arange(LANES)[:, None]).astype(np.float32)

    cos, slo, shi = tabs_q
    kv_full = lambda blk: pl.BlockSpec((1, T, dh), lambda b, g, i: (b, 0, kvb + blk + g))
    qtab = pl.BlockSpec((tq, dh), lambda b, g, i: (i, 0))
    ktab = pl.BlockSpec((T, dh), lambda b, g, i: (0, 0))
    return pl.pallas_call(
        functools.partial(_nsa_kernel, n_slc=n_slc, n_cmp=n_cmp),
        out_shape=jax.ShapeDtypeStruct((B, T, G * HPG * dh), BF16),
        grid=(B, G, T // tq),
        in_specs=[
            pl.BlockSpec((1, tq, HPG * dh), lambda b, g, i: (b, i, g)),
            kv_full(2 * G), kv_full(3 * G), kv_full(4 * G), kv_full(5 * G),
            pl.BlockSpec((1, 1, 1, nb, dh), lambda b, g, i: (b, 0, g, 0, 0)),
            pl.BlockSpec((1, 1, 1, nb, dh), lambda b, g, i: (b, 1, g, 0, 0)),
            pl.BlockSpec((1, tq, LANES), lambda b, g, i: (b, i, g)),
            qtab, qtab, qtab, ktab, ktab, ktab,
            pl.BlockSpec((nb, LANES), lambda b, g, i: (0, 0)),
            pl.BlockSpec((LANES, T), lambda b, g, i: (0, 0)),
        ],
        out_specs=pl.BlockSpec((1, tq, HPG * dh), lambda b, g, i: (b, i, g)),
        scratch_shapes=[pltpu.VMEM((T, dh), BF16), pltpu.VMEM((T, 2 * dh), BF16),
                        pltpu.VMEM((T, dh), BF16), pltpu.VMEM((T, 2 * dh), BF16),
                        pltpu.VMEM((NSA_QB, T // SLC_CHUNK, Q_BLOCK, SLC_CHUNK), F32)],
        compiler_params=_cparams(("arbitrary", "arbitrary", "arbitrary")),
        name="nsa_attention",
    )(qkv, qkv, qkv, qkv, qkv, cmp_kv, cmp_kv, gl, cos, slo, shi, cos, slo, shi,
      jnp.asarray(c2s), jnp.asarray(expand, dtype=BF16))


CONV_PAD = 32
CONV_CHUNK = 256


def _conv_kernel(a_ref, b_ref, w_ref, cb_ref, g_ref, gb_ref, o_ref, u_ref):
    T = a_ref.shape[1]
    u_ref[0:CONV_PAD, :] = jnp.zeros((CONV_PAD, LANES), F32)
    u_ref[CONV_PAD:CONV_PAD + T, :] = a_ref[0].astype(F32) * _sigmoid(b_ref[0].astype(F32))
    base = CONV_PAD - (CONV_KERNEL - 1)
    for c in range(T // CONV_CHUNK):
        t0 = c * CONV_CHUNK
        acc = jnp.zeros((CONV_CHUNK, LANES), F32)
        for k in range(CONV_KERNEL):
            acc = acc + u_ref[t0 + base + k:t0 + base + k + CONV_CHUNK, :] * w_ref[k:k + 1, :]
        acc = acc + cb_ref[...]
        mu = jnp.mean(acc, axis=-1, keepdims=True)
        d = acc - mu
        var = jnp.mean(d * d, axis=-1, keepdims=True)
        y = d * lax.rsqrt(var + LN_EPS) * g_ref[...] + gb_ref[...]
        o_ref[0, t0:t0 + CONV_CHUNK, :] = (y * _sigmoid(y)).astype(o_ref.dtype)


def conv_module(glu, conv_w, conv_b, gn_g, gn_b, B, T):
    C = glu.shape[-1] // 2
    ng = C // LANES
    assert C // CONV_GROUPS == LANES
    vec = pl.BlockSpec((1, LANES), lambda b, g: (0, g))
    return pl.pallas_call(
        _conv_kernel,
        out_shape=jax.ShapeDtypeStruct((B, T, C), BF16),
        grid=(B, ng),
        in_specs=[
            pl.BlockSpec((1, T, LANES), lambda b, g: (b, 0, g)),
            pl.BlockSpec((1, T, LANES), lambda b, g: (b, 0, ng + g)),
            pl.BlockSpec((CONV_KERNEL, LANES), lambda b, g: (0, g)),
            vec, vec, vec,
        ],
        out_specs=pl.BlockSpec((1, T, LANES), lambda b, g: (b, 0, g)),
        scratch_shapes=[pltpu.VMEM((CONV_PAD + T, LANES), F32)],
        compiler_params=_cparams(("arbitrary", "arbitrary")),
        name="conformer_conv",
    )(glu, glu, conv_w, conv_b.reshape(1, C), gn_g.reshape(1, C), gn_b.reshape(1, C))


def _router_kernel(x_ref, wt_ref, bias_ref, idx_ref, wts_ref):
    aff = _sigmoid(lax.dot_general(wt_ref[...], x_ref[...], (((1,), (1,)), ((), ())),
                                   preferred_element_type=F32, precision=lax.Precision.HIGHEST))
    sel = aff + bias_ref[...]
    a = [aff[e:e + 1, :] for e in range(N_EXPERTS)]
    s = [sel[e:e + 1, :] for e in range(N_EXPERTS)]
    P = EXPERTS_PER_GROUP
    grp = []
    for g in range(N_EXPERT_GROUPS):
        v = s[g * P:(g + 1) * P]
        best = None
        for i in range(P):
            for j in range(i + 1, P):
                pair = v[i] + v[j]
                best = pair if best is None else jnp.maximum(best, pair)
        grp.append(best)
    gbest = jnp.zeros_like(grp[0], dtype=jnp.int32)
    gval = grp[0]
    for g in range(1, N_EXPERT_GROUPS):
        better = grp[g] > gval
        gbest = jnp.where(better, g, gbest)
        gval = jnp.where(better, grp[g], gval)
    cs, ca = [], []
    for p in range(P):
        sv, av = s[p], a[p]
        for g in range(1, N_EXPERT_GROUPS):
            sv = jnp.where(gbest == g, s[g * P + p], sv)
            av = jnp.where(gbest == g, a[g * P + p], av)
        cs.append(sv)
        ca.append(av)
    i1 = jnp.zeros_like(gbest)
    v1, a1 = cs[0], ca[0]
    for p in range(1, P):
        better = cs[p] > v1
        i1 = jnp.where(better, p, i1)
        v1 = jnp.where(better, cs[p], v1)
        a1 = jnp.where(better, ca[p], a1)
    i2 = jnp.full_like(gbest, -1)
    v2 = jnp.full_like(v1, -jnp.inf)
    a2 = jnp.zeros_like(a1)
    for p in range(P):
        better = (i1 != p) & (cs[p] > v2)
        i2 = jnp.where(better, p, i2)
        v2 = jnp.where(better, cs[p], v2)
        a2 = jnp.where(better, ca[p], a2)
    den = a1 + a2
    idx_ref[...] = jnp.zeros(idx_ref.shape, jnp.int32)
    wts_ref[...] = jnp.zeros(wts_ref.shape, F32)
    idx_ref[0:1, :] = gbest * P + i1
    idx_ref[1:2, :] = gbest * P + i2
    wts_ref[0:1, :] = a1 / den
    wts_ref[1:2, :] = a2 / den


def router(x, router_w, router_bias, *, tm=1024):
    N, D = x.shape
    E = N_EXPERTS
    return pl.pallas_call(
        _router_kernel,
        out_shape=(jax.ShapeDtypeStruct((8, N), jnp.int32), jax.ShapeDtypeStruct((8, N), F32)),
        grid=(N // tm,),
        in_specs=[
            pl.BlockSpec((tm, D), lambda i: (i, 0)),
            pl.BlockSpec((E, D), lambda i: (0, 0)),
            pl.BlockSpec((E, 1), lambda i: (0, 0)),
        ],
        out_specs=(pl.BlockSpec((8, tm), lambda i: (0, i)), pl.BlockSpec((8, tm), lambda i: (0, i))),
        compiler_params=_cparams(("arbitrary",)),
        name="moe_router",
    )(x, router_w.T, router_bias.reshape(E, 1))


PLAN_CHUNK = 512


def _plan_kernel(idx_ref, dest_ref, cnt_ref, *, n_tok):
    E = N_EXPERTS
    sub = lax.broadcasted_iota(jnp.int32, (TOP_K * E, n_tok), 0)
    tgt = jnp.where(sub < E, idx_ref[0:1, :], idx_ref[1:2, :])
    onehot = ((sub & (E - 1)) == tgt).astype(F32)
    onehot_b = onehot.astype(BF16)
    tri = (lax.broadcasted_iota(jnp.int32, (PLAN_CHUNK, PLAN_CHUNK), 0)
           <= lax.broadcasted_iota(jnp.int32, (PLAN_CHUNK, PLAN_CHUNK), 1)).astype(F32).astype(BF16)
    carry = jnp.zeros((TOP_K * E, 1), F32)
    parts = []
    for c in range(n_tok // PLAN_CHUNK):
        pre = _dot(onehot_b[:, c * PLAN_CHUNK:(c + 1) * PLAN_CHUNK], tri) + carry
        parts.append(pre)
        carry = pre[:, PLAN_CHUNK - 1:PLAN_CHUNK]
    excl = jnp.concatenate(parts, axis=1) - onehot
    cnt0 = carry[0:E]
    tot = cnt0 + carry[E:2 * E]
    lower = (lax.broadcasted_iota(jnp.int32, (E, E), 1)
             < lax.broadcasted_iota(jnp.int32, (E, E), 0)).astype(F32)
    offs = jnp.dot(lower, jnp.broadcast_to(tot, (E, LANES)), preferred_element_type=F32,
                   precision=lax.Precision.HIGHEST)[:, 0:1]
    base = jnp.concatenate([offs, offs + cnt0], axis=0)
    val = onehot * (base + excl)
    dest_ref[...] = jnp.zeros(dest_ref.shape, jnp.int32)
    dest_ref[0:1, :] = jnp.sum(val[0:E], axis=0, keepdims=True).astype(jnp.int32)
    dest_ref[1:2, :] = jnp.sum(val[E:2 * E], axis=0, keepdims=True).astype(jnp.int32)
    cnt_ref[...] = jnp.broadcast_to(tot, (E, LANES)).astype(jnp.int32)


def moe_plan(idx):
    n_tok = idx.shape[1]
    assert n_tok % PLAN_CHUNK == 0 and TOP_K == 2
    return pl.pallas_call(
        functools.partial(_plan_kernel, n_tok=n_tok),
        out_shape=(jax.ShapeDtypeStruct((8, n_tok), jnp.int32),
                   jax.ShapeDtypeStruct((N_EXPERTS, LANES), jnp.int32)),
        compiler_params=_cparams(None),
        name="moe_plan",
    )(idx)


def _scatter_kernel(dest_ref, x_ref, xs_hbm, sem, *, tm, n_tok):
    base = pl.program_id(0) * tm

    def copies(start):
        def make(r, k, d):
            cp = pltpu.make_async_copy(x_ref.at[pl.ds(r, 1)], xs_hbm.at[pl.ds(d, 1)], sem)
            if start:
                cp.start(priority=k)
            else:
                cp.wait()
        _row_copies(dest_ref, n_tok, base, tm, make)

    copies(True)
    copies(False)


def moe_scatter(x, dest, *, tm=256):
    N, D = x.shape
    grid_spec = pltpu.PrefetchScalarGridSpec(
        num_scalar_prefetch=1,
        grid=(N // tm,),
        in_specs=[pl.BlockSpec((tm, D), lambda i, d: (i, 0))],
        out_specs=pl.BlockSpec(memory_space=pl.ANY),
        scratch_shapes=[pltpu.SemaphoreType.DMA(())],
    )
    return pl.pallas_call(
        functools.partial(_scatter_kernel, tm=tm, n_tok=N),
        out_shape=jax.ShapeDtypeStruct((TOP_K * N, D), x.dtype),
        grid_spec=grid_spec,
        compiler_params=_cparams(("arbitrary",)),
        name="moe_scatter",
    )(dest, x)


def _moe_kernel(ti_ref, te_ref, lo_ref, hi_ref, first_ref, head_ref, slot_ref, nxt_ref,
                xs_ref, wg_hbm, wu_hbm, wd_hbm, y_ref,
                wg32_ref, wu32_ref, wd32_ref, wgb_ref, wub_ref, wdb_ref, sem, *, layer):
    i = pl.program_id(0)

    def weight_copies(e, slot):
        return [pltpu.make_async_copy(src.at[layer, e], dst.at[slot], sem.at[slot, n])
                for n, (src, dst) in enumerate(((wg_hbm, wg32_ref), (wu_hbm, wu32_ref), (wd_hbm, wd32_ref)))]

    @pl.when(i == 0)
    def _():
        for cp in weight_copies(te_ref[0], 0):
            cp.start()

    @pl.when(head_ref[i] == 1)
    def _():
        slot = slot_ref[i]
        for cp in weight_copies(te_ref[i], slot):
            cp.wait()
        wgb_ref[...] = wg32_ref[slot].astype(BF16)
        wub_ref[...] = wu32_ref[slot].astype(BF16)
        wdb_ref[...] = wd32_ref[slot].astype(BF16)

        @pl.when(nxt_ref[i] >= 0)
        def _():
            for cp in weight_copies(nxt_ref[i], 1 - slot):
                cp.start()

    lo, hi = lo_ref[i], hi_ref[i]

    @pl.when(hi > lo)
    def _():
        x = xs_ref[...].astype(BF16)
        g = _dot(x, wgb_ref[...])
        u = _dot(x, wub_ref[...])
        rows = lax.broadcasted_iota(jnp.int32, (x.shape[0], 1), 0)
        h = jnp.where((rows >= lo) & (rows < hi), (g * _sigmoid(g)) * u, 0.0)
        y = _dot(h.astype(BF16), wdb_ref[...])

        @pl.when(first_ref[i] == 1)
        def _():
            y_ref[...] = y

        @pl.when(first_ref[i] == 0)
        def _():
            y_ref[...] += y


def moe_experts(xs, items, w_gate, w_up, w_down, layer, *, tm=MOE_TM):
    P, D = xs.shape
    Fh = w_gate.shape[-1]
    n_items = items[0].shape[0]
    xmap = lambda i, ti, *_: (ti[i], 0)
    hbm = pl.BlockSpec(memory_space=pl.ANY)
    grid_spec = pltpu.PrefetchScalarGridSpec(
        num_scalar_prefetch=len(items),
        grid=(n_items,),
        in_specs=[pl.BlockSpec((tm, D), xmap), hbm, hbm, hbm],
        out_specs=pl.BlockSpec((tm, D), xmap),
        scratch_shapes=[pltpu.VMEM((2, D, Fh), F32), pltpu.VMEM((2, D, Fh), F32), pltpu.VMEM((2, Fh, D), F32),
                        pltpu.VMEM((D, Fh), BF16), pltpu.VMEM((D, Fh), BF16), pltpu.VMEM((Fh, D), BF16),
                        pltpu.SemaphoreType.DMA((2, 3))],
    )
    return pl.pallas_call(
        functools.partial(_moe_kernel, layer=layer),
        out_shape=jax.ShapeDtypeStruct((P, D), F32),
        grid_spec=grid_spec,
        compiler_params=_cparams(("arbitrary",)),
        name="moe_experts",
    )(*items, xs, w_gate, w_up, w_down)


def moe_items(tot, n_rows, tm=MOE_TM):
    E = N_EXPERTS
    n_max = n_rows // tm + E
    ar = jnp.arange(E, dtype=jnp.int32)
    ends = jnp.cumsum(tot)
    offs = ends - tot
    first_tile = offs // tm
    n_e = jnp.where(tot > 0, (ends - 1) // tm - first_tile + 1, 0)
    s_end = jnp.cumsum(n_e)
    s_beg = s_end - n_e
    n_items = s_end[-1]
    i = jnp.arange(n_max, dtype=jnp.int32)
    ic = jnp.minimum(i, n_items - 1)
    e_i = jnp.sum((ic[:, None] >= s_end[None, :]).astype(jnp.int32), axis=1)
    pick = (e_i[:, None] == ar[None, :]).astype(jnp.int32)
    at = lambda v: jnp.sum(pick * v[None, :], axis=1)
    tile = at(first_tile) + ic - at(s_beg)
    live = i < n_items
    lo = jnp.where(live, jnp.maximum(at(offs), tile * tm) - tile * tm, 0)
    hi = jnp.where(live, jnp.minimum(at(ends), tile * tm + tm) - tile * tm, 0)
    prev_tile = jnp.concatenate([jnp.full((1,), -1, jnp.int32), tile[:-1]])
    first = live & (tile != prev_tile)
    head = live & (ic == at(s_beg))
    used = (tot > 0).astype(jnp.int32)
    slot = at(jnp.cumsum(used) - used) & 1
    later = jnp.where((ar[None, :] > ar[:, None]) & (tot[None, :] > 0), ar[None, :], E)
    nxt_e = jnp.min(later, axis=1)
    nxt = at(jnp.where(nxt_e < E, nxt_e, -1))
    return tuple(v.astype(jnp.int32) for v in (tile, e_i, lo, hi, first, head, slot, nxt))


def _rope_tables(pos):
    inv = ROPE_THETA ** (-jnp.arange(0, ROPE_DIM, 2, dtype=F32) / ROPE_DIM)
    ang = pos.astype(F32)[:, None] * inv[None, :]
    cos, sin = jnp.cos(ang), jnp.sin(ang)
    n = pos.shape[0]
    rest = HEAD_DIM - ROPE_DIM
    cos_t = jnp.concatenate([cos, cos, jnp.ones((n, rest), F32)], axis=1)
    sin_lo = jnp.concatenate([-sin, jnp.zeros((n, HEAD_DIM - ROPE_HALF), F32)], axis=1)
    sin_hi = jnp.concatenate([jnp.zeros((n, ROPE_HALF), F32), sin, jnp.zeros((n, rest), F32)], axis=1)
    return cos_t, sin_lo, sin_hi


def _even_mixer(xb, B, T, w_in_all, j, pos, w1, w2, conv_w, conv_b, gn_g, gn_b):
    N, D = xb.shape
    G, HPG, dh = NSA_KV_HEADS, NSA_HPG, HEAD_DIM
    nsa_w = G * HPG * dh
    kv_cols = 3 * 2 * G * dh
    n_gate = 3 * G * HPG
    conv_c = (w_in_all.shape[2] - nsa_w - kv_cols - n_gate) // 2
    qkv = linear(xb, w_in_all, nsa_w + kv_cols, F32, layer=j, name="even_qkv")
    w_gl = w_in_all[j, :, nsa_w + kv_cols:nsa_w + kv_cols + n_gate].reshape(D, G, 3 * HPG)
    w_gl = jnp.pad(w_gl, ((0, 0), (0, 0), (0, LANES - 3 * HPG))).reshape(D, G * LANES)
    gl = linear(xb, w_gl, G * LANES, F32, tn=G * LANES, name="even_gates")
    glu = linear(xb, w_in_all[j, :, nsa_w + kv_cols + n_gate:], 2 * conv_c, BF16, name="even_glu")

    qkv3 = qkv.reshape(B, T, nsa_w + kv_cols)
    n_cmp = (T - CMP_LEN) // CMP_STRIDE + 1
    nb = T // CMP_STRIDE
    cmp_end = jnp.arange(nb) * CMP_STRIDE + (CMP_LEN - 1)
    cmp_kv = compress(qkv3, pos, w1, w2, _rope_tables(cmp_end), B, T, G * HPG)
    o_nsa = nsa_attention(qkv3, cmp_kv, gl.reshape(B, T, G * LANES), _rope_tables(jnp.arange(T)), B, T)
    u = conv_module(glu.reshape(B, T, 2 * conv_c), conv_w, conv_b, gn_g, gn_b, B, T)
    return o_nsa.reshape(N, nsa_w), u.reshape(N, conv_c)


def _fox_mixer(xb, B, T, w_in_all, j, f_bias, q_gain, k_gain):
    N, D = xb.shape
    H = D // HEAD_DIM
    proj = linear(xb, w_in_all, 4 * D, BF16, layer=j, name="fox_qkvg")
    w_f = jnp.pad(w_in_all[j, :, 4 * D:], ((0, 0), (0, LANES - H)))
    fl = linear(xb, w_f, LANES, F32, tn=LANES, name="fox_forget")
    c = fox_decay(fl, jnp.pad(f_bias, (0, LANES - H)), B, T)[:, :H].reshape(B, T, H)
    return fox_attention(proj.reshape(B, T, 4 * D), c, q_gain, k_gain, B, T, H).reshape(N, D)


def _moe(xf, router_w, router_bias, w_gate, w_up, w_down, layer):
    N, D = xf.shape
    idx, wts = router(xf, router_w, router_bias)
    dest8, cnt = moe_plan(idx)
    dest = dest8[:TOP_K].reshape(-1)
    xs = moe_scatter(xf, dest)
    y = moe_experts(xs, moe_items(cnt[:, 0], TOP_K * N), w_gate, w_up, w_down, layer)
    return y, dest, wts[:TOP_K].T


def kernel(x, even_w_in, even_w_out, nsa_cmp_pos, nsa_cmp_w1, nsa_cmp_w2, conv_w, conv_b, conv_gn_g, conv_gn_b, fox_w_in, fox_f_bias, fox_q_gain, fox_k_gain, fox_w_out, ln_mix_g, ln_mix_b, ln_ffn_g, ln_ffn_b, router_w, router_bias, exp_w_gate, exp_w_up, exp_w_down):
    B, T, D = x.shape
    depth = ln_mix_g.shape[0]
    alpha = (2.0 * depth) ** 0.25
    N = B * T
    xf = x.reshape(N, D)
    xb = xf
    for layer in range(depth):
        j = layer // 2
        if layer % 2 == 0:
            a1, a2 = _even_mixer(xb, B, T, even_w_in, j, nsa_cmp_pos[j], nsa_cmp_w1[j], nsa_cmp_w2[j],
                                 conv_w[j], conv_b[j], conv_gn_g[j], conv_gn_b[j])
            xf, xb = outproj_ln(a1, a2, 0, even_w_out, j, xf, ln_mix_g[layer], ln_mix_b[layer], alpha)
        else:
            a = _fox_mixer(xb, B, T, fox_w_in, j, fox_f_bias[j], fox_q_gain[j], fox_k_gain[j])
            xf, xb = outproj_ln(a, a, 1, fox_w_out, j, xf, ln_mix_g[layer], ln_mix_b[layer], alpha)
        y, dest, wts = _moe(xf, router_w, router_bias, exp_w_gate, exp_w_up, exp_w_down, layer)
        xf, xb = combine_ln(xf, y, dest, wts, ln_ffn_g[layer], ln_ffn_b[layer], alpha)
    return xf.reshape(B, T, D)
```

```python
import functools

import numpy as np
import jax
import jax.numpy as jnp
from jax import lax
from jax.experimental import pallas as pl
from jax.experimental.pallas import tpu as pltpu

F32 = jnp.float32
BF16 = jnp.bfloat16

HEAD_DIM = 128
ROPE_THETA = 500000.0
ROPE_DIM = HEAD_DIM // 4
ROPE_HALF = ROPE_DIM // 2
Q_BLOCK = 128

NSA_KV_HEADS = 2
NSA_HPG = 4
CMP_LEN = 32
CMP_STRIDE = 16
SLC_LEN = 64
N_SLC = 8
WIN = 512
CONV_KERNEL = 31
CONV_GROUPS = 8

N_EXPERTS = 16
N_EXPERT_GROUPS = 4
EXPERTS_PER_GROUP = 4
TOP_K = 2

LN_EPS = 1e-5
NEG = -1e30
BIG = 1e9
TINY = 1e-30

LANES = 128
VMEM_LIMIT = 56 * 1024 * 1024

MOE_TM = 256


def _cparams(sem, vmem=VMEM_LIMIT):
    return pltpu.CompilerParams(dimension_semantics=sem, vmem_limit_bytes=vmem)


def _dot(a, b):
    return jnp.dot(a, b, preferred_element_type=F32)


def _dot_nt(a, b):
    return lax.dot_general(a, b, (((1,), (1,)), ((), ())), preferred_element_type=F32)


def _sigmoid(x):
    return 1.0 / (1.0 + jnp.exp(-x))


def _linear_kernel(x_ref, w_ref, o_ref, wb_ref):
    @pl.when(pl.program_id(1) == 0)
    def _():
        wb_ref[...] = w_ref[...].astype(BF16)

    o_ref[...] = _dot(x_ref[...].astype(BF16), wb_ref[...]).astype(o_ref.dtype)


def linear(x, w, n_cols, out_dtype, *, layer=None, tm=1024, tn=512, name="linear"):
    M, K = x.shape
    tm = min(tm, M)
    tn = min(tn, n_cols)
    assert M % tm == 0 and n_cols % tn == 0
    row_block = 0
    if layer is not None:
        w = w.reshape(-1, w.shape[-1])
        row_block = layer
    w_spec = pl.BlockSpec((K, tn), lambda j, i: (row_block, j))
    return pl.pallas_call(
        _linear_kernel,
        out_shape=jax.ShapeDtypeStruct((M, n_cols), out_dtype),
        grid=(n_cols // tn, M // tm),
        in_specs=[
            pl.BlockSpec((tm, K), lambda j, i: (i, 0)),
            w_spec,
        ],
        out_specs=pl.BlockSpec((tm, tn), lambda j, i: (i, j)),
        scratch_shapes=[pltpu.VMEM((K, tn), BF16)],
        compiler_params=_cparams(("arbitrary", "arbitrary")),
        name=name,
    )(x, w)


def _layer_norm_rows(y, g, b):
    mu = jnp.mean(y, axis=-1, keepdims=True)
    d = y - mu
    var = jnp.mean(d * d, axis=-1, keepdims=True)
    return d * lax.rsqrt(var + LN_EPS) * g + b


def _outproj_ln_kernel(a1_ref, a2_ref, w_ref, x_ref, g_ref, b_ref, xo_ref, xb_ref, wb_ref, *, alpha):
    @pl.when(pl.program_id(0) == 0)
    def _():
        wb_ref[...] = w_ref[...].astype(BF16)

    half = a1_ref.shape[1]
    h = _dot(a1_ref[...].astype(BF16), wb_ref[0:half, :])
    h = h + _dot(a2_ref[...].astype(BF16), wb_ref[half:2 * half, :])
    out = _layer_norm_rows(alpha * x_ref[...] + h, g_ref[...], b_ref[...])
    xo_ref[...] = out
    xb_ref[...] = out.astype(BF16)


def outproj_ln(a1, a2, a2_col_block, w, layer, x, g, b, alpha, *, tm=512):
    M, D = x.shape
    half = D // 2
    return pl.pallas_call(
        functools.partial(_outproj_ln_kernel, alpha=alpha),
        out_shape=(jax.ShapeDtypeStruct((M, D), F32), jax.ShapeDtypeStruct((M, D), BF16)),
        grid=(M // tm,),
        in_specs=[
            pl.BlockSpec((tm, half), lambda i: (i, 0)),
            pl.BlockSpec((tm, half), lambda i: (i, a2_col_block)),
            pl.BlockSpec((None, D, D), lambda i: (layer, 0, 0), pipeline_mode=pl.Buffered(1)),
            pl.BlockSpec((tm, D), lambda i: (i, 0)),
            pl.BlockSpec((1, D), lambda i: (0, 0)),
            pl.BlockSpec((1, D), lambda i: (0, 0)),
        ],
        out_specs=(pl.BlockSpec((tm, D), lambda i: (i, 0)), pl.BlockSpec((tm, D), lambda i: (i, 0))),
        scratch_shapes=[pltpu.VMEM((D, D), BF16)],
        compiler_params=_cparams(("arbitrary",)),
        name="outproj_ln",
    )(a1, a2, w, x, g.reshape(1, D), b.reshape(1, D))


ROW_DMA_UNROLL = 8


def _row_copies(dest_ref, n_tok, base, tm, make):
    def body(r, carry):
        for k in range(TOP_K):
            make(r, k, dest_ref[k * n_tok + base + r])
        return carry

    lax.fori_loop(0, tm, body, 0, unroll=ROW_DMA_UNROLL)


def _combine_ln_kernel(dest_ref, x_ref, w_ref, g_ref, b_ref, y_hbm, xo_ref, xb_ref, buf_ref, sem, *,
                       alpha, tm, n_tok):
    i = pl.program_id(0)
    n_steps = pl.num_programs(0)

    def copies(step, slot, start):
        def make(r, k, d):
            cp = pltpu.make_async_copy(y_hbm.at[pl.ds(d, 1)], buf_ref.at[slot, k, pl.ds(r, 1)], sem.at[slot])
            if start:
                cp.start(priority=k)
            else:
                cp.wait()
        _row_copies(dest_ref, n_tok, step * tm, tm, make)

    @pl.when(i == 0)
    def _():
        copies(0, 0, True)

    @pl.when(i + 1 < n_steps)
    def _():
        copies(i + 1, (i + 1) % 2, True)

    slot = i % 2
    copies(i, slot, False)
    f = w_ref[:, 0:1] * buf_ref[slot, 0] + w_ref[:, 1:2] * buf_ref[slot, 1]
    out = _layer_norm_rows(alpha * x_ref[...] + f, g_ref[...], b_ref[...])
    xo_ref[...] = out
    xb_ref[...] = out.astype(BF16)


def combine_ln(x, y, dest, wts, g, b, alpha, *, tm=256):
    N, D = x.shape
    row = lambda i, d: (i, 0)
    grid_spec = pltpu.PrefetchScalarGridSpec(
        num_scalar_prefetch=1,
        grid=(N // tm,),
        in_specs=[
            pl.BlockSpec((tm, D), row),
            pl.BlockSpec((tm, TOP_K), row),
            pl.BlockSpec((1, D), lambda i, d: (0, 0)),
            pl.BlockSpec((1, D), lambda i, d: (0, 0)),
            pl.BlockSpec(memory_space=pl.ANY),
        ],
        out_specs=(pl.BlockSpec((tm, D), row), pl.BlockSpec((tm, D), row)),
        scratch_shapes=[pltpu.VMEM((2, TOP_K, tm, D), F32), pltpu.SemaphoreType.DMA((2,))],
    )
    return pl.pallas_call(
        functools.partial(_combine_ln_kernel, alpha=alpha, tm=tm, n_tok=N),
        out_shape=(jax.ShapeDtypeStruct((N, D), F32), jax.ShapeDtypeStruct((N, D), BF16)),
        grid_spec=grid_spec,
        compiler_params=_cparams(("arbitrary",)),
        name="moe_combine_ln",
    )(dest, x, wts, g.reshape(1, D), b.reshape(1, D), y)


def _rms(x, gain):
    return x * lax.rsqrt(jnp.mean(x * x, axis=-1, keepdims=True) + LN_EPS) * gain


CUM_CHUNK = 256


def _decay_kernel(fl_ref, bias_ref, c_ref):
    T = fl_ref.shape[0]
    tri = (lax.broadcasted_iota(jnp.int32, (CUM_CHUNK, CUM_CHUNK), 0)
           >= lax.broadcasted_iota(jnp.int32, (CUM_CHUNK, CUM_CHUNK), 1)).astype(F32)
    carry = jnp.zeros((1, LANES), F32)
    for c in range(T // CUM_CHUNK):
        z = fl_ref[c * CUM_CHUNK:(c + 1) * CUM_CHUNK, :] + bias_ref[...]
        log_f = jnp.minimum(z, 0.0) - jnp.log1p(jnp.exp(-jnp.abs(z)))
        cs = jnp.dot(tri, log_f, preferred_element_type=F32, precision=lax.Precision.HIGHEST) + carry
        c_ref[c * CUM_CHUNK:(c + 1) * CUM_CHUNK, :] = cs
        carry = cs[CUM_CHUNK - 1:CUM_CHUNK, :]


def fox_decay(fl, bias, B, T):
    assert T % CUM_CHUNK == 0
    return pl.pallas_call(
        _decay_kernel,
        out_shape=jax.ShapeDtypeStruct(fl.shape, F32),
        grid=(B,),
        in_specs=[pl.BlockSpec((T, LANES), lambda b: (b, 0)), pl.BlockSpec((1, LANES), lambda b: (0, 0))],
        out_specs=pl.BlockSpec((T, LANES), lambda b: (b, 0)),
        compiler_params=_cparams(("arbitrary",)),
        name="fox_decay",
    )(fl, bias.reshape(1, LANES))


LOG2E = 1.4426950408889634
FOX_FAST_BOUND = 38.0
FOX_AUX = 3


def _split3(a):
    hi = a.astype(BF16).astype(F32)
    r = a - hi
    mid = r.astype(BF16).astype(F32)
    return hi, mid, (r - mid).astype(BF16).astype(F32)


def _aux_lanes(pieces, ones_first, n):
    lane = lax.broadcasted_iota(jnp.int32, (n, LANES), 1)
    p0 = FOX_AUX if ones_first else 0
    o0 = 0 if ones_first else FOX_AUX
    out = jnp.where((lane >= o0) & (lane < o0 + FOX_AUX), 1.0, 0.0)
    for i, piece in enumerate(pieces):
        out = out + jnp.where(lane == p0 + i, piece, 0.0)
    return out


def _fox_kernel(fast_ref, q_ref, k_ref, v_ref, og_ref, ckrow_ref, ckcol_ref, cq_ref, bound_ref, qg_ref, kg_ref,
                o_ref, kx_ref, vx_ref, *, tq, tk, hb):
    qi = pl.program_id(2)
    dh = HEAD_DIM
    T = k_ref.shape[1]

    @pl.when(qi == 0)
    def _():
        for h in range(hb):
            kx_ref[h, :, 0:dh] = _rms(k_ref[0, :, h * dh:(h + 1) * dh].astype(F32), kg_ref[...]).astype(BF16)
            kx_ref[h, :, dh:2 * dh] = _aux_lanes(_split3(-LOG2E * ckcol_ref[0, 0, :, h:h + 1]), False, T
                                                 ).astype(BF16)
            vx_ref[h, :, 0:dh] = v_ref[0, :, h * dh:(h + 1) * dh].astype(BF16)
            vx_ref[h, :, dh:2 * dh] = jnp.ones((T, dh), BF16)

    row = qi * tq + lax.broadcasted_iota(jnp.int32, (tq, tk), 0)
    lane = lax.broadcasted_iota(jnp.int32, (tq, tk), 1)
    n_full = (qi * tq) // tk

    def finish(accs):
        for h in range(hb):
            o = accs[h][:, 0:dh] / jnp.maximum(accs[h][:, dh:2 * dh], TINY)
            gate = _sigmoid(og_ref[0, :, h * dh:(h + 1) * dh].astype(F32))
            o_ref[0, :, h * dh:(h + 1) * dh] = (o * gate).astype(o_ref.dtype)

    def q_normed(h):
        return _rms(q_ref[0, :, h * dh:(h + 1) * dh].astype(F32), qg_ref[...]) * (dh ** -0.5)

    @pl.when(fast_ref[0] == 1)
    def _():
        qx = []
        for h in range(hb):
            r = LOG2E * (cq_ref[0, 0, :, h:h + 1] - bound_ref[0:1, 0:1])
            qx.append(jnp.concatenate([(q_normed(h) * LOG2E).astype(BF16),
                                       _aux_lanes(_split3(r), True, tq).astype(BF16)], axis=1))

        def step(j, accs, masked):
            start = pl.multiple_of(j * tk, tk)
            out = []
            for h in range(hb):
                s = _dot_nt(qx[h], kx_ref[h, pl.ds(start, tk), :])
                if masked:
                    s = jnp.where(j * tk + lane <= row, s, NEG)
                out.append(accs[h] + _dot(jnp.exp2(s).astype(BF16), vx_ref[h, pl.ds(start, tk), :]))
            return tuple(out)

        accs = tuple(jnp.zeros((tq, 2 * dh), F32) for _ in range(hb))
        accs = lax.fori_loop(0, n_full, lambda j, a: step(j, a, False), accs)
        finish(step(n_full, accs, True))

    @pl.when(fast_ref[0] == 0)
    def _():
        qs = [q_normed(h).astype(BF16) for h in range(hb)]

        def step(j, carry, masked):
            start = pl.multiple_of(j * tk, tk)
            out = []
            for h in range(hb):
                m, acc = carry[h]
                s = _dot_nt(qs[h], kx_ref[h, pl.ds(start, tk), 0:dh]) - ckrow_ref[0, h, pl.ds(j, 1), :]
                if masked:
                    s = jnp.where(j * tk + lane <= row, s, NEG)
                m_new = jnp.maximum(m, jnp.max(s, axis=-1, keepdims=True))
                p = jnp.exp(s - m_new).astype(BF16)
                acc = jnp.exp(m - m_new) * acc + _dot(p, vx_ref[h, pl.ds(start, tk), :])
                out.append((m_new, acc))
            return tuple(out)

        init = tuple((jnp.full((tq, 1), NEG, F32), jnp.zeros((tq, 2 * dh), F32)) for _ in range(hb))
        carry = lax.fori_loop(0, n_full, lambda j, c: step(j, c, False), init)
        finish([c[1] for c in step(n_full, carry, True)])


def fox_attention(proj, c, q_gain, k_gain, B, T, H, *, tq=512, tk=512, hb=4):
    nq = T // tq
    nk = T // tk
    dh = HEAD_DIM
    hg = H // hb
    c_row = c.transpose(0, 2, 1).reshape(B, H, nk, tk)
    c_col = c.reshape(B, T, hg, hb).transpose(0, 2, 1, 3)
    bound = jnp.max(jnp.abs(q_gain)) * jnp.max(jnp.abs(k_gain)) * (dh ** 0.5) * 1.01
    fast = (bound <= FOX_FAST_BOUND).astype(jnp.int32).reshape(1)
    full = lambda off: pl.BlockSpec((1, T, hb * dh), lambda b, h, i, f: (b, 0, off + h))
    tile = lambda off: pl.BlockSpec((1, tq, hb * dh), lambda b, h, i, f: (b, i, off + h))
    vec = pl.BlockSpec((1, dh), lambda b, h, i, f: (0, 0))
    grid_spec = pltpu.PrefetchScalarGridSpec(
        num_scalar_prefetch=1,
        grid=(B, hg, nq),
        in_specs=[
            tile(0), full(hg), full(2 * hg), tile(3 * hg),
            pl.BlockSpec((1, hb, nk, tk), lambda b, h, i, f: (b, h, 0, 0)),
            pl.BlockSpec((1, 1, T, hb), lambda b, h, i, f: (b, h, 0, 0)),
            pl.BlockSpec((1, 1, tq, hb), lambda b, h, i, f: (b, h, i, 0)),
            vec, vec, vec,
        ],
        out_specs=pl.BlockSpec((1, tq, hb * dh), lambda b, h, i, f: (b, i, h)),
        scratch_shapes=[pltpu.VMEM((hb, T, 2 * dh), BF16), pltpu.VMEM((hb, T, 2 * dh), BF16)],
    )
    return pl.pallas_call(
        functools.partial(_fox_kernel, tq=tq, tk=tk, hb=hb),
        out_shape=jax.ShapeDtypeStruct((B, T, H * dh), BF16),
        grid_spec=grid_spec,
        compiler_params=_cparams(("arbitrary", "arbitrary", "arbitrary")),
        name="fox_attention",
    )(fast, proj, proj, proj, proj, c_row, c_col, c_col, jnp.full((1, dh), bound, F32),
      q_gain.reshape(1, dh), k_gain.reshape(1, dh))


def _rope(x, cos, sin_lo, sin_hi):
    return (x * cos + pltpu.roll(x, LANES - ROPE_HALF, 1) * sin_lo
            + pltpu.roll(x, ROPE_HALF, 1) * sin_hi)


def _gelu_tanh(x):
    return 0.5 * x * (1.0 + jnp.tanh(0.7978845608028654 * (x + 0.044715 * (x * x * x))))


def _compress_kernel(raw_ref, pos_ref, w1_ref, w2_ref, cos_ref, slo_ref, shi_ref, o_ref, *, n_cmp):
    nb = o_ref.shape[-2]
    half = CMP_LEN // 2
    acc_a = jnp.zeros((nb, HEAD_DIM), F32)
    acc_b = jnp.zeros((nb, HEAD_DIM), F32)
    for l in range(half):
        rl = raw_ref[0, pl.ds(l, nb, stride=CMP_STRIDE), :]
        wa = w1_ref[0, l * HEAD_DIM:(l + 1) * HEAD_DIM, :].astype(BF16)
        wb = w1_ref[0, (half + l) * HEAD_DIM:(half + l + 1) * HEAD_DIM, :].astype(BF16)
        acc_a = acc_a + _dot((rl + pos_ref[0, l:l + 1, :]).astype(BF16), wa)
        acc_b = acc_b + _dot((rl + pos_ref[0, half + l:half + l + 1, :]).astype(BF16), wb)
    pre = acc_a + pltpu.roll(acc_b, nb - 1, 0)
    out = _dot(_gelu_tanh(pre).astype(BF16), w2_ref[0].astype(BF16))
    roped = _rope(out, cos_ref[...], slo_ref[...], shi_ref[...])
    out = jnp.where(pl.program_id(1) == 0, roped, out)
    rows = lax.broadcasted_iota(jnp.int32, out.shape, 0)
    o_ref[0, 0, 0] = jnp.where(rows < n_cmp, out, 0.0)


def compress(qkv, pos, w1, w2, tabs, B, T, kv_block0):
    G = NSA_KV_HEADS
    nb = T // CMP_STRIDE
    n_cmp = (T - CMP_LEN) // CMP_STRIDE + 1
    cos, slo, shi = tabs
    tab = pl.BlockSpec((nb, HEAD_DIM), lambda b, kv, g: (0, 0))
    return pl.pallas_call(
        functools.partial(_compress_kernel, n_cmp=n_cmp),
        out_shape=jax.ShapeDtypeStruct((B, 2, G, nb, HEAD_DIM), F32),
        grid=(B, 2, G),
        in_specs=[
            pl.BlockSpec((1, T, HEAD_DIM), lambda b, kv, g: (b, 0, kv_block0 + kv * G + g)),
            pl.BlockSpec((1, CMP_LEN, HEAD_DIM), lambda b, kv, g: (kv, 0, 0)),
            pl.BlockSpec((1, CMP_LEN * HEAD_DIM, HEAD_DIM), lambda b, kv, g: (kv, 0, 0)),
            pl.BlockSpec((1, HEAD_DIM, HEAD_DIM), lambda b, kv, g: (kv, 0, 0)),
            tab, tab, tab,
        ],
        out_specs=pl.BlockSpec((1, 1, 1, nb, HEAD_DIM), lambda b, kv, g: (b, kv, g, 0, 0)),
        compiler_params=_cparams(("arbitrary", "arbitrary", "arbitrary")),
        name="nsa_compress",
    )(qkv, pos, w1, w2, cos, slo, shi)


SLC_CHUNK = 512
WIN_SPAN = WIN + Q_BLOCK
NSA_QB = 4


def _select_blocks(imp, qi, n_slc):
    QB = Q_BLOCK
    n_rows = -(-n_slc // 8) * 8
    imp_t = jnp.transpose(imp)[0:n_rows]
    blk = lax.broadcasted_iota(jnp.int32, (n_rows, QB), 0)
    t = qi * QB + lax.broadcasted_iota(jnp.int32, (n_rows, QB), 1)
    cur = jnp.right_shift(t, 6)
    forced = (blk == 0) | (blk == cur) | (blk == cur - 1)
    score = jnp.where(forced, BIG, jnp.where(blk * SLC_LEN <= t, imp_t, -BIG))
    score = jnp.where(blk < n_slc, score, -2.0 * BIG)
    cnt = jnp.zeros((n_rows, QB), F32)
    for jp in range(n_slc):
        r = score[jp:jp + 1, :]
        cnt = cnt + ((r > score) | ((r == score) & (jp < blk))).astype(F32)
    sel_t = ((cnt < float(min(N_SLC, n_slc))) & (blk < n_slc)).astype(F32)
    if n_rows < LANES:
        sel_t = jnp.concatenate([sel_t, jnp.zeros((LANES - n_rows, QB), F32)], axis=0)
    return jnp.transpose(sel_t).astype(BF16)


def _nsa_kernel(q_ref, ks_ref, vs_ref, kw_ref, vw_ref, kc_ref, vc_ref, gl_ref,
                cq_ref, sloq_ref, shiq_ref, ck_ref, slok_ref, shik_ref, c2s_ref, ex_ref,
                o_ref, ksb_ref, vsx_ref, kwb_ref, vwx_ref, bias_ref, *, n_slc, n_cmp):
    step = pl.program_id(2)
    QB, HPG, dh = Q_BLOCK, NSA_HPG, HEAD_DIM
    T = ks_ref.shape[1]

    @pl.when(step == 0)
    def _():
        ksb_ref[...] = _rope(ks_ref[0], ck_ref[...], slok_ref[...], shik_ref[...]).astype(BF16)
        kwb_ref[...] = _rope(kw_ref[0], ck_ref[...], slok_ref[...], shik_ref[...]).astype(BF16)
        ones = jnp.ones((T, dh), BF16)
        vsx_ref[:, 0:dh] = vs_ref[0].astype(BF16)
        vsx_ref[:, dh:2 * dh] = ones
        vwx_ref[:, 0:dh] = vw_ref[0].astype(BF16)
        vwx_ref[:, dh:2 * dh] = ones

    qis = [step * NSA_QB + b for b in range(NSA_QB)]
    rows = lambda b, h: slice((b * HPG + h) * QB, (b * HPG + h + 1) * QB)
    qrows = lambda b: slice(b * QB, (b + 1) * QB)
    pieces = []
    for b in range(NSA_QB):
        for h in range(HPG):
            x = q_ref[0, qrows(b), h * dh:(h + 1) * dh]
            pieces.append(_rope(x, cq_ref[qrows(b), :], sloq_ref[qrows(b), :], shiq_ref[qrows(b), :]) * (dh ** -0.5))
    q = jnp.concatenate(pieces, axis=0).astype(BF16)

    lane = lax.broadcasted_iota(jnp.int32, (QB, LANES), 1)
    sub = lax.broadcasted_iota(jnp.int32, (QB, LANES), 0)

    kc = kc_ref[0, 0, 0].astype(BF16)
    vc = vc_ref[0, 0, 0].astype(BF16)
    sc = _dot_nt(q, kc)
    o_cmp, sels = {}, []
    for b in range(NSA_QB):
        t = qis[b] * QB + sub
        mask_c = (lane * CMP_STRIDE + (CMP_LEN - 1) <= t) & (lane < n_cmp)
        mask_cf = mask_c.astype(F32)
        imp_c = jnp.zeros((QB, LANES), F32)
        for h in range(HPG):
            s = jnp.where(mask_c, sc[rows(b, h)], NEG)
            e = jnp.exp(s - jnp.max(s, axis=-1, keepdims=True)) * mask_cf
            p = e / jnp.maximum(jnp.sum(e, axis=-1, keepdims=True), TINY)
            imp_c = imp_c + p
            o_cmp[b, h] = _dot(p.astype(BF16), vc)
        imp = jnp.dot(imp_c, c2s_ref[...], preferred_element_type=F32, precision=lax.Precision.HIGHEST)
        sels.append(_select_blocks(imp, qis[b], n_slc))

    sel_all = jnp.concatenate(sels, axis=0)
    for c in range(T // SLC_CHUNK):
        selx = _dot(sel_all, ex_ref[:, c * SLC_CHUNK:(c + 1) * SLC_CHUNK])
        kpos = c * SLC_CHUNK + lax.broadcasted_iota(jnp.int32, (QB, SLC_CHUNK), 1)
        for b in range(NSA_QB):
            tq_ = qis[b] * QB + lax.broadcasted_iota(jnp.int32, (QB, SLC_CHUNK), 0)
            bias_ref[b, c] = jnp.where((selx[qrows(b)] > 0.5) & (kpos <= tq_), 0.0, NEG)

    o_win = {}
    for b in range(NSA_QB):
        wstart = pl.multiple_of(jnp.clip(qis[b] * QB - WIN, 0, T - WIN_SPAN), Q_BLOCK)
        qb = q[b * HPG * QB:(b + 1) * HPG * QB]
        sw = _dot_nt(qb, kwb_ref[pl.ds(wstart, WIN_SPAN), :])
        kpos = wstart + lax.broadcasted_iota(jnp.int32, (QB, WIN_SPAN), 1)
        tw = qis[b] * QB + lax.broadcasted_iota(jnp.int32, (QB, WIN_SPAN), 0)
        bias_w = jnp.where((kpos <= tw) & (kpos > tw - WIN), 0.0, NEG)
        pw = []
        for h in range(HPG):
            sh = sw[h * QB:(h + 1) * QB] + bias_w
            pw.append(jnp.exp(sh - jnp.max(sh, axis=-1, keepdims=True)).astype(BF16))
        ow = _dot(jnp.concatenate(pw, axis=0), vwx_ref[pl.ds(wstart, WIN_SPAN), :])
        for h in range(HPG):
            acc_w = ow[h * QB:(h + 1) * QB]
            o_win[b, h] = acc_w[:, 0:dh] / jnp.maximum(acc_w[:, dh:2 * dh], TINY)

    chains = [(b, h) for b in range(NSA_QB) for h in range(HPG)]

    def slc_body(c, carry):
        start = pl.multiple_of(c * SLC_CHUNK, SLC_CHUNK)
        s = _dot_nt(q, ksb_ref[pl.ds(start, SLC_CHUNK), :])
        ms, alphas, ps = [], [], []
        for n, (b, h) in enumerate(chains):
            sh = s[rows(b, h)] + bias_ref[b, c]
            m_new = jnp.maximum(carry[n][0], jnp.max(sh, axis=-1, keepdims=True))
            alphas.append(jnp.exp(carry[n][0] - m_new))
            ps.append(jnp.exp(sh - m_new).astype(BF16))
            ms.append(m_new)
        pv = _dot(jnp.concatenate(ps, axis=0), vsx_ref[pl.ds(start, SLC_CHUNK), :])
        return tuple((ms[n], alphas[n] * carry[n][1] + pv[rows(b, h)]) for n, (b, h) in enumerate(chains))

    init = tuple((jnp.full((QB, 1), NEG, F32), jnp.zeros((QB, 2 * dh), F32)) for _ in chains)
    n_chunks = (qis[-1] * QB + QB + SLC_CHUNK - 1) // SLC_CHUNK
    slc = lax.fori_loop(0, n_chunks, slc_body, init)

    for b in range(NSA_QB):
        gates = _sigmoid(gl_ref[0, qrows(b), :])
        for h in range(HPG):
            acc_s = slc[b * HPG + h][1]
            o_s = acc_s[:, 0:dh] / jnp.maximum(acc_s[:, dh:2 * dh], TINY)
            out = (gates[:, 3 * h:3 * h + 1] * o_cmp[b, h] + gates[:, 3 * h + 1:3 * h + 2] * o_s
                   + gates[:, 3 * h + 2:3 * h + 3] * o_win[b, h])
            o_ref[0, qrows(b), h * dh:(h + 1) * dh] = out.astype(o_ref.dtype)


def nsa_attention(qkv, cmp_kv, gl, tabs_q, B, T):
    G, HPG, dh = NSA_KV_HEADS, NSA_HPG, HEAD_DIM
    tq = NSA_QB * Q_BLOCK
    nb = T // CMP_STRIDE
    n_cmp = (T - CMP_LEN) // CMP_STRIDE + 1
    n_slc = T // SLC_LEN
    assert nb == LANES and n_slc <= LANES and T % SLC_CHUNK == 0 and T >= WIN_SPAN and T % tq == 0
    kvb = (HPG * G)

    c_start = np.arange(nb) * CMP_STRIDE
    s_start = np.arange(LANES) * SLC_LEN
    c2s = ((c_start[:, None] < s_start[None, :] + SLC_LEN) & (c_start[:, None] + CMP_LEN > s_start[None, :])
           & (np.arange(nb)[:, None] < n_cmp) & (np.arange(LANES)[None, :] < n_slc)).astype(np.float32)
    expand = (np.arange(T)[None, :] // SLC_LEN == np.arange(LANES)[:, None]).astype(np.float32)

    cos, slo, shi = tabs_q
    kv_full = lambda blk: pl.BlockSpec((1, T, dh), lambda b, g, i: (b, 0, kvb + blk + g))
    qtab = pl.BlockSpec((tq, dh), lambda b, g, i: (i, 0))
    ktab = pl.BlockSpec((T, dh), lambda b, g, i: (0, 0))
    return pl.pallas_call(
        functools.partial(_nsa_kernel, n_slc=n_slc, n_cmp=n_cmp),
        out_shape=jax.ShapeDtypeStruct((B, T, G * HPG * dh), BF16),
        grid=(B, G, T // tq),
        in_specs=[
            pl.BlockSpec((1, tq, HPG * dh), lambda b, g, i: (b, i, g)),
            kv_full(2 * G), kv_full(3 * G), kv_full(4 * G), kv_full(5 * G),
            pl.BlockSpec((1, 1, 1, nb, dh), lambda b, g, i: (b, 0, g, 0, 0)),
            pl.BlockSpec((1, 1, 1, nb, dh), lambda b, g, i: (b, 1, g, 0, 0)),
            pl.BlockSpec((1, tq, LANES), lambda b, g, i: (b, i, g)),
            qtab, qtab, qtab, ktab, ktab, ktab,
            pl.BlockSpec((nb, LANES), lambda b, g, i: (0, 0)),
            pl.BlockSpec((LANES, T), lambda b, g, i: (0, 0)),
        ],
        out_specs=pl.BlockSpec((1, tq, HPG * dh), lambda b, g, i: (b, i, g)),
        scratch_shapes=[pltpu.VMEM((T, dh), BF16), pltpu.VMEM((T, 2 * dh), BF16),
                        pltpu.VMEM((T, dh), BF16), pltpu.VMEM((T, 2 * dh), BF16),
                        pltpu.VMEM((NSA_QB, T // SLC_CHUNK, Q_BLOCK, SLC_CHUNK), F32)],
        compiler_params=_cparams(("arbitrary", "arbitrary", "arbitrary")),
        name="nsa_attention",
    )(qkv, qkv, qkv, qkv, qkv, cmp_kv, cmp_kv, gl, cos, slo, shi, cos, slo, shi,
      jnp.asarray(c2s), jnp.asarray(expand, dtype=BF16))


CONV_PAD = 32
CONV_CHUNK = 256


def _conv_kernel(a_ref, b_ref, w_ref, cb_ref, g_ref, gb_ref, o_ref, u_ref):
    T = a_ref.shape[1]
    u_ref[0:CONV_PAD, :] = jnp.zeros((CONV_PAD, LANES), F32)
    u_ref[CONV_PAD:CONV_PAD + T, :] = a_ref[0].astype(F32) * _sigmoid(b_ref[0].astype(F32))
    base = CONV_PAD - (CONV_KERNEL - 1)
    for c in range(T // CONV_CHUNK):
        t0 = c * CONV_CHUNK
        acc = jnp.zeros((CONV_CHUNK, LANES), F32)
        for k in range(CONV_KERNEL):
            acc = acc + u_ref[t0 + base + k:t0 + base + k + CONV_CHUNK, :] * w_ref[k:k + 1, :]
        acc = acc + cb_ref[...]
        mu = jnp.mean(acc, axis=-1, keepdims=True)
        d = acc - mu
        var = jnp.mean(d * d, axis=-1, keepdims=True)
        y = d * lax.rsqrt(var + LN_EPS) * g_ref[...] + gb_ref[...]
        o_ref[0, t0:t0 + CONV_CHUNK, :] = (y * _sigmoid(y)).astype(o_ref.dtype)


def conv_module(glu, conv_w, conv_b, gn_g, gn_b, B, T):
    C = glu.shape[-1] // 2
    ng = C // LANES
    assert C // CONV_GROUPS == LANES
    vec = pl.BlockSpec((1, LANES), lambda b, g: (0, g))
    return pl.pallas_call(
        _conv_kernel,
        out_shape=jax.ShapeDtypeStruct((B, T, C), BF16),
        grid=(B, ng),
        in_specs=[
            pl.BlockSpec((1, T, LANES), lambda b, g: (b, 0, g)),
            pl.BlockSpec((1, T, LANES), lambda b, g: (b, 0, ng + g)),
            pl.BlockSpec((CONV_KERNEL, LANES), lambda b, g: (0, g)),
            vec, vec, vec,
        ],
        out_specs=pl.BlockSpec((1, T, LANES), lambda b, g: (b, 0, g)),
        scratch_shapes=[pltpu.VMEM((CONV_PAD + T, LANES), F32)],
        compiler_params=_cparams(("arbitrary", "arbitrary")),
        name="conformer_conv",
    )(glu, glu, conv_w, conv_b.reshape(1, C), gn_g.reshape(1, C), gn_b.reshape(1, C))


def _split2(a):
    hi = a.astype(BF16)
    return hi, (a - hi.astype(F32)).astype(BF16)


def _router_kernel(x_ref, w_ref, bias_ref, idx_ref, wts_ref):
    xh, xl = _split2(x_ref[...])
    wh, wl = _split2(w_ref[...])
    logits = _dot(xh, wh) + (_dot(xh, wl) + _dot(xl, wh))
    aff = _sigmoid(jnp.concatenate(
        [jnp.transpose(logits[c * LANES:(c + 1) * LANES])[0:N_EXPERTS] for c in range(logits.shape[0] // LANES)],
        axis=1))
    sel = aff + bias_ref[...]
    a = [aff[e:e + 1, :] for e in range(N_EXPERTS)]
    s = [sel[e:e + 1, :] for e in range(N_EXPERTS)]
    P = EXPERTS_PER_GROUP
    grp = []
    for g in range(N_EXPERT_GROUPS):
        v = s[g * P:(g + 1) * P]
        best = None
        for i in range(P):
            for j in range(i + 1, P):
                pair = v[i] + v[j]
                best = pair if best is None else jnp.maximum(best, pair)
        grp.append(best)
    gbest = jnp.zeros_like(grp[0], dtype=jnp.int32)
    gval = grp[0]
    for g in range(1, N_EXPERT_GROUPS):
        better = grp[g] > gval
        gbest = jnp.where(better, g, gbest)
        gval = jnp.where(better, grp[g], gval)
    cs, ca = [], []
    for p in range(P):
        sv, av = s[p], a[p]
        for g in range(1, N_EXPERT_GROUPS):
            sv = jnp.where(gbest == g, s[g * P + p], sv)
            av = jnp.where(gbest == g, a[g * P + p], av)
        cs.append(sv)
        ca.append(av)
    i1 = jnp.zeros_like(gbest)
    v1, a1 = cs[0], ca[0]
    for p in range(1, P):
        better = cs[p] > v1
        i1 = jnp.where(better, p, i1)
        v1 = jnp.where(better, cs[p], v1)
        a1 = jnp.where(better, ca[p], a1)
    i2 = jnp.full_like(gbest, -1)
    v2 = jnp.full_like(v1, -jnp.inf)
    a2 = jnp.zeros_like(a1)
    for p in range(P):
        better = (i1 != p) & (cs[p] > v2)
        i2 = jnp.where(better, p, i2)
        v2 = jnp.where(better, cs[p], v2)
        a2 = jnp.where(better, ca[p], a2)
    den = a1 + a2
    idx_ref[...] = jnp.zeros(idx_ref.shape, jnp.int32)
    wts_ref[...] = jnp.zeros(wts_ref.shape, F32)
    idx_ref[0:1, :] = gbest * P + i1
    idx_ref[1:2, :] = gbest * P + i2
    wts_ref[0:1, :] = a1 / den
    wts_ref[1:2, :] = a2 / den


def router(x, router_w, router_bias, *, tm=1024):
    N, D = x.shape
    E = N_EXPERTS
    return pl.pallas_call(
        _router_kernel,
        out_shape=(jax.ShapeDtypeStruct((8, N), jnp.int32), jax.ShapeDtypeStruct((8, N), F32)),
        grid=(N // tm,),
        in_specs=[
            pl.BlockSpec((tm, D), lambda i: (i, 0)),
            pl.BlockSpec((D, LANES), lambda i: (0, 0)),
            pl.BlockSpec((E, 1), lambda i: (0, 0)),
        ],
        out_specs=(pl.BlockSpec((8, tm), lambda i: (0, i)), pl.BlockSpec((8, tm), lambda i: (0, i))),
        compiler_params=_cparams(("arbitrary",)),
        name="moe_router",
    )(x, jnp.pad(router_w, ((0, 0), (0, LANES - E))), router_bias.reshape(E, 1))


PLAN_CHUNK = 512


def _plan_kernel(idx_ref, dest_ref, cnt_ref, *, n_tok):
    E = N_EXPERTS
    sub = lax.broadcasted_iota(jnp.int32, (TOP_K * E, n_tok), 0)
    tgt = jnp.where(sub < E, idx_ref[0:1, :], idx_ref[1:2, :])
    onehot = ((sub & (E - 1)) == tgt).astype(F32)
    onehot_b = onehot.astype(BF16)
    tri = (lax.broadcasted_iota(jnp.int32, (PLAN_CHUNK, PLAN_CHUNK), 0)
           <= lax.broadcasted_iota(jnp.int32, (PLAN_CHUNK, PLAN_CHUNK), 1)).astype(F32).astype(BF16)
    carry = jnp.zeros((TOP_K * E, 1), F32)
    parts = []
    for c in range(n_tok // PLAN_CHUNK):
        pre = _dot(onehot_b[:, c * PLAN_CHUNK:(c + 1) * PLAN_CHUNK], tri) + carry
        parts.append(pre)
        carry = pre[:, PLAN_CHUNK - 1:PLAN_CHUNK]
    excl = jnp.concatenate(parts, axis=1) - onehot
    cnt0 = carry[0:E]
    tot = cnt0 + carry[E:2 * E]
    lower = (lax.broadcasted_iota(jnp.int32, (E, E), 1)
             < lax.broadcasted_iota(jnp.int32, (E, E), 0)).astype(F32)
    offs = jnp.dot(lower, jnp.broadcast_to(tot, (E, LANES)), preferred_element_type=F32,
                   precision=lax.Precision.HIGHEST)[:, 0:1]
    base = jnp.concatenate([offs, offs + cnt0], axis=0)
    val = onehot * (base + excl)
    dest_ref[...] = jnp.zeros(dest_ref.shape, jnp.int32)
    dest_ref[0:1, :] = jnp.sum(val[0:E], axis=0, keepdims=True).astype(jnp.int32)
    dest_ref[1:2, :] = jnp.sum(val[E:2 * E], axis=0, keepdims=True).astype(jnp.int32)
    cnt_ref[...] = jnp.broadcast_to(tot, (E, LANES)).astype(jnp.int32)


def moe_plan(idx):
    n_tok = idx.shape[1]
    assert n_tok % PLAN_CHUNK == 0 and TOP_K == 2
    return pl.pallas_call(
        functools.partial(_plan_kernel, n_tok=n_tok),
        out_shape=(jax.ShapeDtypeStruct((8, n_tok), jnp.int32),
                   jax.ShapeDtypeStruct((N_EXPERTS, LANES), jnp.int32)),
        compiler_params=_cparams(None),
        name="moe_plan",
    )(idx)


def _scatter_kernel(dest_ref, x_ref, xs_hbm, sem, *, tm, n_tok):
    base = pl.program_id(0) * tm

    def copies(start):
        def make(r, k, d):
            cp = pltpu.make_async_copy(x_ref.at[pl.ds(r, 1)], xs_hbm.at[pl.ds(d, 1)], sem)
            if start:
                cp.start(priority=k)
            else:
                cp.wait()
        _row_copies(dest_ref, n_tok, base, tm, make)

    copies(True)
    copies(False)


def moe_scatter(x, dest, *, tm=256):
    N, D = x.shape
    grid_spec = pltpu.PrefetchScalarGridSpec(
        num_scalar_prefetch=1,
        grid=(N // tm,),
        in_specs=[pl.BlockSpec((tm, D), lambda i, d: (i, 0))],
        out_specs=pl.BlockSpec(memory_space=pl.ANY),
        scratch_shapes=[pltpu.SemaphoreType.DMA(())],
    )
    return pl.pallas_call(
        functools.partial(_scatter_kernel, tm=tm, n_tok=N),
        out_shape=jax.ShapeDtypeStruct((TOP_K * N, D), x.dtype),
        grid_spec=grid_spec,
        compiler_params=_cparams(("arbitrary",)),
        name="moe_scatter",
    )(dest, x)


def _moe_kernel(ti_ref, te_ref, lo_ref, hi_ref, first_ref, head_ref, slot_ref, nxt_ref,
                xs_ref, wg_hbm, wu_hbm, wd_hbm, y_ref,
                wg32_ref, wu32_ref, wd32_ref, wgb_ref, wub_ref, wdb_ref, sem, *, layer):
    i = pl.program_id(0)

    def weight_copies(e, slot):
        return [pltpu.make_async_copy(src.at[layer, e], dst.at[slot], sem.at[slot, n])
                for n, (src, dst) in enumerate(((wg_hbm, wg32_ref), (wu_hbm, wu32_ref), (wd_hbm, wd32_ref)))]

    @pl.when(i == 0)
    def _():
        for cp in weight_copies(te_ref[0], 0):
            cp.start()

    @pl.when(head_ref[i] == 1)
    def _():
        slot = slot_ref[i]
        for cp in weight_copies(te_ref[i], slot):
            cp.wait()
        wgb_ref[...] = wg32_ref[slot].astype(BF16)
        wub_ref[...] = wu32_ref[slot].astype(BF16)
        wdb_ref[...] = wd32_ref[slot].astype(BF16)

        @pl.when(nxt_ref[i] >= 0)
        def _():
            for cp in weight_copies(nxt_ref[i], 1 - slot):
                cp.start()

    lo, hi = lo_ref[i], hi_ref[i]

    @pl.when(hi > lo)
    def _():
        x = xs_ref[...].astype(BF16)
        g = _dot(x, wgb_ref[...])
        u = _dot(x, wub_ref[...])
        rows = lax.broadcasted_iota(jnp.int32, (x.shape[0], 1), 0)
        h = jnp.where((rows >= lo) & (rows < hi), (g * _sigmoid(g)) * u, 0.0)
        y = _dot(h.astype(BF16), wdb_ref[...])

        @pl.when(first_ref[i] == 1)
        def _():
            y_ref[...] = y

        @pl.when(first_ref[i] == 0)
        def _():
            y_ref[...] += y


def moe_experts(xs, items, w_gate, w_up, w_down, layer, *, tm=MOE_TM):
    P, D = xs.shape
    Fh = w_gate.shape[-1]
    n_items = items[0].shape[0]
    xmap = lambda i, ti, *_: (ti[i], 0)
    hbm = pl.BlockSpec(memory_space=pl.ANY)
    grid_spec = pltpu.PrefetchScalarGridSpec(
        num_scalar_prefetch=len(items),
        grid=(n_items,),
        in_specs=[pl.BlockSpec((tm, D), xmap), hbm, hbm, hbm],
        out_specs=pl.BlockSpec((tm, D), xmap),
        scratch_shapes=[pltpu.VMEM((2, D, Fh), F32), pltpu.VMEM((2, D, Fh), F32), pltpu.VMEM((2, Fh, D), F32),
                        pltpu.VMEM((D, Fh), BF16), pltpu.VMEM((D, Fh), BF16), pltpu.VMEM((Fh, D), BF16),
                        pltpu.SemaphoreType.DMA((2, 3))],
    )
    return pl.pallas_call(
        functools.partial(_moe_kernel, layer=layer),
        out_shape=jax.ShapeDtypeStruct((P, D), F32),
        grid_spec=grid_spec,
        compiler_params=_cparams(("arbitrary",)),
        name="moe_experts",
    )(*items, xs, w_gate, w_up, w_down)


def moe_items(tot, n_rows, tm=MOE_TM):
    E = N_EXPERTS
    n_max = n_rows // tm + E
    ar = jnp.arange(E, dtype=jnp.int32)
    ends = jnp.cumsum(tot)
    offs = ends - tot
    first_tile = offs // tm
    n_e = jnp.where(tot > 0, (ends - 1) // tm - first_tile + 1, 0)
    s_end = jnp.cumsum(n_e)
    s_beg = s_end - n_e
    n_items = s_end[-1]
    i = jnp.arange(n_max, dtype=jnp.int32)
    ic = jnp.minimum(i, n_items - 1)
    e_i = jnp.sum((ic[:, None] >= s_end[None, :]).astype(jnp.int32), axis=1)
    pick = (e_i[:, None] == ar[None, :]).astype(jnp.int32)
    at = lambda v: jnp.sum(pick * v[None, :], axis=1)
    tile = at(first_tile) + ic - at(s_beg)
    live = i < n_items
    lo = jnp.where(live, jnp.maximum(at(offs), tile * tm) - tile * tm, 0)
    hi = jnp.where(live, jnp.minimum(at(ends), tile * tm + tm) - tile * tm, 0)
    prev_tile = jnp.concatenate([jnp.full((1,), -1, jnp.int32), tile[:-1]])
    first = live & (tile != prev_tile)
    head = live & (ic == at(s_beg))
    used = (tot > 0).astype(jnp.int32)
    slot = at(jnp.cumsum(used) - used) & 1
    later = jnp.where((ar[None, :] > ar[:, None]) & (tot[None, :] > 0), ar[None, :], E)
    nxt_e = jnp.min(later, axis=1)
    nxt = at(jnp.where(nxt_e < E, nxt_e, -1))
    return tuple(v.astype(jnp.int32) for v in (tile, e_i, lo, hi, first, head, slot, nxt))


def _rope_tables(pos):
    inv = ROPE_THETA ** (-jnp.arange(0, ROPE_DIM, 2, dtype=F32) / ROPE_DIM)
    ang = pos.astype(F32)[:, None] * inv[None, :]
    cos, sin = jnp.cos(ang), jnp.sin(ang)
    n = pos.shape[0]
    rest = HEAD_DIM - ROPE_DIM
    cos_t = jnp.concatenate([cos, cos, jnp.ones((n, rest), F32)], axis=1)
    sin_lo = jnp.concatenate([-sin, jnp.zeros((n, HEAD_DIM - ROPE_HALF), F32)], axis=1)
    sin_hi = jnp.concatenate([jnp.zeros((n, ROPE_HALF), F32), sin, jnp.zeros((n, rest), F32)], axis=1)
    return cos_t, sin_lo, sin_hi


def _even_mixer(xb, B, T, w_in_all, j, pos, w1, w2, conv_w, conv_b, gn_g, gn_b):
    N, D = xb.shape
    G, HPG, dh = NSA_KV_HEADS, NSA_HPG, HEAD_DIM
    nsa_w = G * HPG * dh
    kv_cols = 3 * 2 * G * dh
    n_gate = 3 * G * HPG
    conv_c = (w_in_all.shape[2] - nsa_w - kv_cols - n_gate) // 2
    qkv = linear(xb, w_in_all, nsa_w + kv_cols, F32, layer=j, name="even_qkv")
    w_gl = w_in_all[j, :, nsa_w + kv_cols:nsa_w + kv_cols + n_gate].reshape(D, G, 3 * HPG)
    w_gl = jnp.pad(w_gl, ((0, 0), (0, 0), (0, LANES - 3 * HPG))).reshape(D, G * LANES)
    gl = linear(xb, w_gl, G * LANES, F32, tn=G * LANES, name="even_gates")
    glu = linear(xb, w_in_all[j, :, nsa_w + kv_cols + n_gate:], 2 * conv_c, BF16, name="even_glu")

    qkv3 = qkv.reshape(B, T, nsa_w + kv_cols)
    n_cmp = (T - CMP_LEN) // CMP_STRIDE + 1
    nb = T // CMP_STRIDE
    cmp_end = jnp.arange(nb) * CMP_STRIDE + (CMP_LEN - 1)
    cmp_kv = compress(qkv3, pos, w1, w2, _rope_tables(cmp_end), B, T, G * HPG)
    o_nsa = nsa_attention(qkv3, cmp_kv, gl.reshape(B, T, G * LANES), _rope_tables(jnp.arange(T)), B, T)
    u = conv_module(glu.reshape(B, T, 2 * conv_c), conv_w, conv_b, gn_g, gn_b, B, T)
    return o_nsa.reshape(N, nsa_w), u.reshape(N, conv_c)


def _fox_mixer(xb, B, T, w_in_all, j, f_bias, q_gain, k_gain):
    N, D = xb.shape
    H = D // HEAD_DIM
    proj = linear(xb, w_in_all, 4 * D, BF16, layer=j, name="fox_qkvg")
    w_f = jnp.pad(w_in_all[j, :, 4 * D:], ((0, 0), (0, LANES - H)))
    fl = linear(xb, w_f, LANES, F32, tn=LANES, name="fox_forget")
    c = fox_decay(fl, jnp.pad(f_bias, (0, LANES - H)), B, T)[:, :H].reshape(B, T, H)
    return fox_attention(proj.reshape(B, T, 4 * D), c, q_gain, k_gain, B, T, H).reshape(N, D)


def _moe(xf, router_w, router_bias, w_gate, w_up, w_down, layer):
    N, D = xf.shape
    idx, wts = router(xf, router_w, router_bias)
    dest8, cnt = moe_plan(idx)
    dest = dest8[:TOP_K].reshape(-1)
    xs = moe_scatter(xf, dest)
    y = moe_experts(xs, moe_items(cnt[:, 0], TOP_K * N), w_gate, w_up, w_down, layer)
    return y, dest, wts[:TOP_K].T


def kernel(x, even_w_in, even_w_out, nsa_cmp_pos, nsa_cmp_w1, nsa_cmp_w2, conv_w, conv_b, conv_gn_g, conv_gn_b, fox_w_in, fox_f_bias, fox_q_gain, fox_k_gain, fox_w_out, ln_mix_g, ln_mix_b, ln_ffn_g, ln_ffn_b, router_w, router_bias, exp_w_gate, exp_w_up, exp_w_down):
    B, T, D = x.shape
    depth = ln_mix_g.shape[0]
    alpha = (2.0 * depth) ** 0.25
    N = B * T
    xf = x.reshape(N, D)
    xb = xf
    for layer in range(depth):
        j = layer // 2
        if layer % 2 == 0:
            a1, a2 = _even_mixer(xb, B, T, even_w_in, j, nsa_cmp_pos[j], nsa_cmp_w1[j], nsa_cmp_w2[j],
                                 conv_w[j], conv_b[j], conv_gn_g[j], conv_gn_b[j])
            xf, xb = outproj_ln(a1, a2, 0, even_w_out, j, xf, ln_mix_g[layer], ln_mix_b[layer], alpha)
        else:
            a = _fox_mixer(xb, B, T, fox_w_in, j, fox_f_bias[j], fox_q_gain[j], fox_k_gain[j])
            xf, xb = outproj_ln(a, a, 1, fox_w_out, j, xf, ln_mix_g[layer], ln_mix_b[layer], alpha)
        y, dest, wts = _moe(xf, router_w, router_bias, exp_w_gate, exp_w_up, exp_w_down, layer)
        xf, xb = combine_ln(xf, y, dest, wts, ln_ffn_g[layer], ln_ffn_b[layer], alpha)
    return xf.reshape(B, T, D)
```

```python
import functools

import numpy as np
import jax
import jax.numpy as jnp
from jax import lax
from jax.experimental import pallas as pl
from jax.experimental.pallas import tpu as pltpu

F32 = jnp.float32
BF16 = jnp.bfloat16

HEAD_DIM = 128
ROPE_THETA = 500000.0
ROPE_DIM = HEAD_DIM // 4
ROPE_HALF = ROPE_DIM // 2
Q_BLOCK = 128

NSA_KV_HEADS = 2
NSA_HPG = 4
CMP_LEN = 32
CMP_STRIDE = 16
SLC_LEN = 64
N_SLC = 8
WIN = 512
CONV_KERNEL = 31
CONV_GROUPS = 8

N_EXPERTS = 16
N_EXPERT_GROUPS = 4
EXPERTS_PER_GROUP = 4
TOP_K = 2

LN_EPS = 1e-5
NEG = -1e30
BIG = 1e9
TINY = 1e-30

LANES = 128
VMEM_LIMIT = 56 * 1024 * 1024

MOE_TM = 256


def _cparams(sem, vmem=VMEM_LIMIT):
    return pltpu.CompilerParams(dimension_semantics=sem, vmem_limit_bytes=vmem)


def _dot(a, b):
    return jnp.dot(a, b, preferred_element_type=F32)


def _dot_nt(a, b):
    return lax.dot_general(a, b, (((1,), (1,)), ((), ())), preferred_element_type=F32)


def _sigmoid(x):
    return 1.0 / (1.0 + jnp.exp(-x))


def _linear_kernel(x_ref, w_ref, o_ref, wb_ref):
    @pl.when(pl.program_id(1) == 0)
    def _():
        wb_ref[...] = w_ref[...].astype(BF16)

    o_ref[...] = _dot(x_ref[...].astype(BF16), wb_ref[...]).astype(o_ref.dtype)


def linear(x, w, n_cols, out_dtype, *, layer=None, tm=1024, tn=512, name="linear"):
    M, K = x.shape
    tm = min(tm, M)
    tn = min(tn, n_cols)
    assert M % tm == 0 and n_cols % tn == 0
    row_block = 0
    if layer is not None:
        w = w.reshape(-1, w.shape[-1])
        row_block = layer
    w_spec = pl.BlockSpec((K, tn), lambda j, i: (row_block, j))
    return pl.pallas_call(
        _linear_kernel,
        out_shape=jax.ShapeDtypeStruct((M, n_cols), out_dtype),
        grid=(n_cols // tn, M // tm),
        in_specs=[
            pl.BlockSpec((tm, K), lambda j, i: (i, 0)),
            w_spec,
        ],
        out_specs=pl.BlockSpec((tm, tn), lambda j, i: (i, j)),
        scratch_shapes=[pltpu.VMEM((K, tn), BF16)],
        compiler_params=_cparams(("arbitrary", "arbitrary")),
        name=name,
    )(x, w)


def _layer_norm_rows(y, g, b):
    mu = jnp.mean(y, axis=-1, keepdims=True)
    d = y - mu
    var = jnp.mean(d * d, axis=-1, keepdims=True)
    return d * lax.rsqrt(var + LN_EPS) * g + b


def _outproj_ln_kernel(a1_ref, a2_ref, w_ref, x_ref, g_ref, b_ref, xo_ref, xb_ref, wb_ref, *, alpha):
    @pl.when(pl.program_id(0) == 0)
    def _():
        wb_ref[...] = w_ref[...].astype(BF16)

    half = a1_ref.shape[1]
    h = _dot(a1_ref[...].astype(BF16), wb_ref[0:half, :])
    h = h + _dot(a2_ref[...].astype(BF16), wb_ref[half:2 * half, :])
    out = _layer_norm_rows(alpha * x_ref[...] + h, g_ref[...], b_ref[...])
    xo_ref[...] = out
    xb_ref[...] = out.astype(BF16)


def outproj_ln(a1, a2, a2_col_block, w, layer, x, g, b, alpha, *, tm=512):
    M, D = x.shape
    half = D // 2
    return pl.pallas_call(
        functools.partial(_outproj_ln_kernel, alpha=alpha),
        out_shape=(jax.ShapeDtypeStruct((M, D), F32), jax.ShapeDtypeStruct((M, D), BF16)),
        grid=(M // tm,),
        in_specs=[
            pl.BlockSpec((tm, half), lambda i: (i, 0)),
            pl.BlockSpec((tm, half), lambda i: (i, a2_col_block)),
            pl.BlockSpec((None, D, D), lambda i: (layer, 0, 0), pipeline_mode=pl.Buffered(1)),
            pl.BlockSpec((tm, D), lambda i: (i, 0)),
            pl.BlockSpec((1, D), lambda i: (0, 0)),
            pl.BlockSpec((1, D), lambda i: (0, 0)),
        ],
        out_specs=(pl.BlockSpec((tm, D), lambda i: (i, 0)), pl.BlockSpec((tm, D), lambda i: (i, 0))),
        scratch_shapes=[pltpu.VMEM((D, D), BF16)],
        compiler_params=_cparams(("arbitrary",)),
        name="outproj_ln",
    )(a1, a2, w, x, g.reshape(1, D), b.reshape(1, D))


ROW_DMA_UNROLL = 8


def _row_copies(dest_ref, n_tok, base, tm, make):
    def body(r, carry):
        for k in range(TOP_K):
            make(r, k, dest_ref[k * n_tok + base + r])
        return carry

    lax.fori_loop(0, tm, body, 0, unroll=ROW_DMA_UNROLL)


def _combine_ln_kernel(dest_ref, x_ref, w_ref, g_ref, b_ref, y_hbm, xo_ref, xb_ref, buf_ref, sem, *,
                       alpha, tm, n_tok):
    i = pl.program_id(0)
    n_steps = pl.num_programs(0)

    def copies(step, slot, start):
        def make(r, k, d):
            cp = pltpu.make_async_copy(y_hbm.at[pl.ds(d, 1)], buf_ref.at[slot, k, pl.ds(r, 1)], sem.at[slot])
            if start:
                cp.start(priority=k)
            else:
                cp.wait()
        _row_copies(dest_ref, n_tok, step * tm, tm, make)

    @pl.when(i == 0)
    def _():
        copies(0, 0, True)

    @pl.when(i + 1 < n_steps)
    def _():
        copies(i + 1, (i + 1) % 2, True)

    slot = i % 2
    copies(i, slot, False)
    f = w_ref[:, 0:1] * buf_ref[slot, 0] + w_ref[:, 1:2] * buf_ref[slot, 1]
    out = _layer_norm_rows(alpha * x_ref[...] + f, g_ref[...], b_ref[...])
    xo_ref[...] = out
    xb_ref[...] = out.astype(BF16)


def combine_ln(x, y, dest, wts, g, b, alpha, *, tm=256):
    N, D = x.shape
    row = lambda i, d: (i, 0)
    grid_spec = pltpu.PrefetchScalarGridSpec(
        num_scalar_prefetch=1,
        grid=(N // tm,),
        in_specs=[
            pl.BlockSpec((tm, D), row),
            pl.BlockSpec((tm, TOP_K), row),
            pl.BlockSpec((1, D), lambda i, d: (0, 0)),
            pl.BlockSpec((1, D), lambda i, d: (0, 0)),
            pl.BlockSpec(memory_space=pl.ANY),
        ],
        out_specs=(pl.BlockSpec((tm, D), row), pl.BlockSpec((tm, D), row)),
        scratch_shapes=[pltpu.VMEM((2, TOP_K, tm, D), F32), pltpu.SemaphoreType.DMA((2,))],
    )
    return pl.pallas_call(
        functools.partial(_combine_ln_kernel, alpha=alpha, tm=tm, n_tok=N),
        out_shape=(jax.ShapeDtypeStruct((N, D), F32), jax.ShapeDtypeStruct((N, D), BF16)),
        grid_spec=grid_spec,
        compiler_params=_cparams(("arbitrary",)),
        name="moe_combine_ln",
    )(dest, x, wts, g.reshape(1, D), b.reshape(1, D), y)


def _rms(x, gain):
    return x * lax.rsqrt(jnp.mean(x * x, axis=-1, keepdims=True) + LN_EPS) * gain


CUM_CHUNK = 256


def _decay_kernel(fl_ref, bias_ref, c_ref):
    T = fl_ref.shape[0]
    tri = (lax.broadcasted_iota(jnp.int32, (CUM_CHUNK, CUM_CHUNK), 0)
           >= lax.broadcasted_iota(jnp.int32, (CUM_CHUNK, CUM_CHUNK), 1)).astype(F32)
    carry = jnp.zeros((1, LANES), F32)
    for c in range(T // CUM_CHUNK):
        z = fl_ref[c * CUM_CHUNK:(c + 1) * CUM_CHUNK, :] + bias_ref[...]
        log_f = jnp.minimum(z, 0.0) - jnp.log1p(jnp.exp(-jnp.abs(z)))
        cs = jnp.dot(tri, log_f, preferred_element_type=F32, precision=lax.Precision.HIGHEST) + carry
        c_ref[c * CUM_CHUNK:(c + 1) * CUM_CHUNK, :] = cs
        carry = cs[CUM_CHUNK - 1:CUM_CHUNK, :]


def fox_decay(fl, bias, B, T):
    assert T % CUM_CHUNK == 0
    return pl.pallas_call(
        _decay_kernel,
        out_shape=jax.ShapeDtypeStruct(fl.shape, F32),
        grid=(B,),
        in_specs=[pl.BlockSpec((T, LANES), lambda b: (b, 0)), pl.BlockSpec((1, LANES), lambda b: (0, 0))],
        out_specs=pl.BlockSpec((T, LANES), lambda b: (b, 0)),
        compiler_params=_cparams(("arbitrary",)),
        name="fox_decay",
    )(fl, bias.reshape(1, LANES))


LOG2E = 1.4426950408889634
FOX_FAST_BOUND = 38.0
FOX_AUX = 3


def _split3(a):
    hi = a.astype(BF16).astype(F32)
    r = a - hi
    mid = r.astype(BF16).astype(F32)
    return hi, mid, (r - mid).astype(BF16).astype(F32)


def _aux_lanes(pieces, ones_first, n):
    lane = lax.broadcasted_iota(jnp.int32, (n, LANES), 1)
    p0 = FOX_AUX if ones_first else 0
    o0 = 0 if ones_first else FOX_AUX
    out = jnp.where((lane >= o0) & (lane < o0 + FOX_AUX), 1.0, 0.0)
    for i, piece in enumerate(pieces):
        out = out + jnp.where(lane == p0 + i, piece, 0.0)
    return out


def _fox_kernel(fast_ref, q_ref, k_ref, v_ref, og_ref, ckrow_ref, ckcol_ref, cq_ref, bound_ref, qg_ref, kg_ref,
                o_ref, kx_ref, vx_ref, *, tq, tk, hb):
    qi = pl.program_id(2)
    dh = HEAD_DIM
    T = k_ref.shape[1]

    @pl.when(qi == 0)
    def _():
        for h in range(hb):
            kx_ref[h, :, 0:dh] = _rms(k_ref[0, :, h * dh:(h + 1) * dh].astype(F32), kg_ref[...]).astype(BF16)
            kx_ref[h, :, dh:2 * dh] = _aux_lanes(_split3(-LOG2E * ckcol_ref[0, 0, :, h:h + 1]), False, T
                                                 ).astype(BF16)
            vx_ref[h, :, 0:dh] = v_ref[0, :, h * dh:(h + 1) * dh].astype(BF16)
            vx_ref[h, :, dh:2 * dh] = jnp.ones((T, dh), BF16)

    row = qi * tq + lax.broadcasted_iota(jnp.int32, (tq, tk), 0)
    lane = lax.broadcasted_iota(jnp.int32, (tq, tk), 1)
    n_full = (qi * tq) // tk

    def finish(accs):
        for h in range(hb):
            o = accs[h][:, 0:dh] / jnp.maximum(accs[h][:, dh:2 * dh], TINY)
            gate = _sigmoid(og_ref[0, :, h * dh:(h + 1) * dh].astype(F32))
            o_ref[0, :, h * dh:(h + 1) * dh] = (o * gate).astype(o_ref.dtype)

    def q_normed(h):
        return _rms(q_ref[0, :, h * dh:(h + 1) * dh].astype(F32), qg_ref[...]) * (dh ** -0.5)

    @pl.when(fast_ref[0] == 1)
    def _():
        qx = []
        for h in range(hb):
            r = LOG2E * (cq_ref[0, 0, :, h:h + 1] - bound_ref[0:1, 0:1])
            qx.append(jnp.concatenate([(q_normed(h) * LOG2E).astype(BF16),
                                       _aux_lanes(_split3(r), True, tq).astype(BF16)], axis=1))

        def step(j, accs, masked):
            start = pl.multiple_of(j * tk, tk)
            out = []
            for h in range(hb):
                s = _dot_nt(qx[h], kx_ref[h, pl.ds(start, tk), :])
                if masked:
                    s = jnp.where(j * tk + lane <= row, s, NEG)
                out.append(accs[h] + _dot(jnp.exp2(s).astype(BF16), vx_ref[h, pl.ds(start, tk), :]))
            return tuple(out)

        accs = tuple(jnp.zeros((tq, 2 * dh), F32) for _ in range(hb))
        accs = lax.fori_loop(0, n_full, lambda j, a: step(j, a, False), accs)
        finish(step(n_full, accs, True))

    @pl.when(fast_ref[0] == 0)
    def _():
        qs = [q_normed(h).astype(BF16) for h in range(hb)]

        def step(j, carry, masked):
            start = pl.multiple_of(j * tk, tk)
            out = []
            for h in range(hb):
                m, acc = carry[h]
                s = _dot_nt(qs[h], kx_ref[h, pl.ds(start, tk), 0:dh]) - ckrow_ref[0, h, pl.ds(j, 1), :]
                if masked:
                    s = jnp.where(j * tk + lane <= row, s, NEG)
                m_new = jnp.maximum(m, jnp.max(s, axis=-1, keepdims=True))
                p = jnp.exp(s - m_new).astype(BF16)
                acc = jnp.exp(m - m_new) * acc + _dot(p, vx_ref[h, pl.ds(start, tk), :])
                out.append((m_new, acc))
            return tuple(out)

        init = tuple((jnp.full((tq, 1), NEG, F32), jnp.zeros((tq, 2 * dh), F32)) for _ in range(hb))
        carry = lax.fori_loop(0, n_full, lambda j, c: step(j, c, False), init)
        finish([c[1] for c in step(n_full, carry, True)])


def fox_attention(proj, c, q_gain, k_gain, B, T, H, *, tq=512, tk=512, hb=4):
    nq = T // tq
    nk = T // tk
    dh = HEAD_DIM
    hg = H // hb
    c_row = c.transpose(0, 2, 1).reshape(B, H, nk, tk)
    c_col = c.reshape(B, T, hg, hb).transpose(0, 2, 1, 3)
    bound = jnp.max(jnp.abs(q_gain)) * jnp.max(jnp.abs(k_gain)) * (dh ** 0.5) * 1.01
    fast = (bound <= FOX_FAST_BOUND).astype(jnp.int32).reshape(1)
    full = lambda off: pl.BlockSpec((1, T, hb * dh), lambda b, h, i, f: (b, 0, off + h))
    tile = lambda off: pl.BlockSpec((1, tq, hb * dh), lambda b, h, i, f: (b, i, off + h))
    vec = pl.BlockSpec((1, dh), lambda b, h, i, f: (0, 0))
    grid_spec = pltpu.PrefetchScalarGridSpec(
        num_scalar_prefetch=1,
        grid=(B, hg, nq),
        in_specs=[
            tile(0), full(hg), full(2 * hg), tile(3 * hg),
            pl.BlockSpec((1, hb, nk, tk), lambda b, h, i, f: (b, h, 0, 0)),
            pl.BlockSpec((1, 1, T, hb), lambda b, h, i, f: (b, h, 0, 0)),
            pl.BlockSpec((1, 1, tq, hb), lambda b, h, i, f: (b, h, i, 0)),
            vec, vec, vec,
        ],
        out_specs=pl.BlockSpec((1, tq, hb * dh), lambda b, h, i, f: (b, i, h)),
        scratch_shapes=[pltpu.VMEM((hb, T, 2 * dh), BF16), pltpu.VMEM((hb, T, 2 * dh), BF16)],
    )
    return pl.pallas_call(
        functools.partial(_fox_kernel, tq=tq, tk=tk, hb=hb),
        out_shape=jax.ShapeDtypeStruct((B, T, H * dh), BF16),
        grid_spec=grid_spec,
        compiler_params=_cparams(("arbitrary", "arbitrary", "arbitrary")),
        name="fox_attention",
    )(fast, proj, proj, proj, proj, c_row, c_col, c_col, jnp.full((1, dh), bound, F32),
      q_gain.reshape(1, dh), k_gain.reshape(1, dh))


def _rope(x, cos, sin_lo, sin_hi):
    return (x * cos + pltpu.roll(x, LANES - ROPE_HALF, 1) * sin_lo
            + pltpu.roll(x, ROPE_HALF, 1) * sin_hi)


def _gelu_tanh(x):
    return 0.5 * x * (1.0 + jnp.tanh(0.7978845608028654 * (x + 0.044715 * (x * x * x))))


def _compress_kernel(raw_ref, pos_ref, w1_ref, w2_ref, cos_ref, slo_ref, shi_ref, o_ref, *, n_cmp):
    nb = o_ref.shape[-2]
    half = CMP_LEN // 2
    acc_a = jnp.zeros((nb, HEAD_DIM), F32)
    acc_b = jnp.zeros((nb, HEAD_DIM), F32)
    for l in range(half):
        rl = raw_ref[0, pl.ds(l, nb, stride=CMP_STRIDE), :]
        wa = w1_ref[0, l * HEAD_DIM:(l + 1) * HEAD_DIM, :].astype(BF16)
        wb = w1_ref[0, (half + l) * HEAD_DIM:(half + l + 1) * HEAD_DIM, :].astype(BF16)
        acc_a = acc_a + _dot((rl + pos_ref[0, l:l + 1, :]).astype(BF16), wa)
        acc_b = acc_b + _dot((rl + pos_ref[0, half + l:half + l + 1, :]).astype(BF16), wb)
    pre = acc_a + pltpu.roll(acc_b, nb - 1, 0)
    out = _dot(_gelu_tanh(pre).astype(BF16), w2_ref[0].astype(BF16))
    roped = _rope(out, cos_ref[...], slo_ref[...], shi_ref[...])
    out = jnp.where(pl.program_id(1) == 0, roped, out)
    rows = lax.broadcasted_iota(jnp.int32, out.shape, 0)
    o_ref[0, 0, 0] = jnp.where(rows < n_cmp, out, 0.0)


def compress(qkv, pos, w1, w2, tabs, B, T, kv_block0):
    G = NSA_KV_HEADS
    nb = T // CMP_STRIDE
    n_cmp = (T - CMP_LEN) // CMP_STRIDE + 1
    cos, slo, shi = tabs
    tab = pl.BlockSpec((nb, HEAD_DIM), lambda b, kv, g: (0, 0))
    return pl.pallas_call(
        functools.partial(_compress_kernel, n_cmp=n_cmp),
        out_shape=jax.ShapeDtypeStruct((B, 2, G, nb, HEAD_DIM), F32),
        grid=(B, 2, G),
        in_specs=[
            pl.BlockSpec((1, T, HEAD_DIM), lambda b, kv, g: (b, 0, kv_block0 + kv * G + g)),
            pl.BlockSpec((1, CMP_LEN, HEAD_DIM), lambda b, kv, g: (kv, 0, 0)),
            pl.BlockSpec((1, CMP_LEN * HEAD_DIM, HEAD_DIM), lambda b, kv, g: (kv, 0, 0)),
            pl.BlockSpec((1, HEAD_DIM, HEAD_DIM), lambda b, kv, g: (kv, 0, 0)),
            tab, tab, tab,
        ],
        out_specs=pl.BlockSpec((1, 1, 1, nb, HEAD_DIM), lambda b, kv, g: (b, kv, g, 0, 0)),
        compiler_params=_cparams(("arbitrary", "arbitrary", "arbitrary")),
        name="nsa_compress",
    )(qkv, pos, w1, w2, cos, slo, shi)


SLC_CHUNK = 512
WIN_SPAN = WIN + Q_BLOCK
NSA_QB = 4


def _select_blocks(imp, qi, n_slc):
    QB = Q_BLOCK
    n_rows = -(-n_slc // 8) * 8
    imp_t = jnp.transpose(imp)[0:n_rows]
    blk = lax.broadcasted_iota(jnp.int32, (n_rows, QB), 0)
    t = qi * QB + lax.broadcasted_iota(jnp.int32, (n_rows, QB), 1)
    cur = jnp.right_shift(t, 6)
    forced = (blk == 0) | (blk == cur) | (blk == cur - 1)
    score = jnp.where(forced, BIG, jnp.where(blk * SLC_LEN <= t, imp_t, -BIG))
    score = jnp.where(blk < n_slc, score, -2.0 * BIG)
    cnt = jnp.zeros((n_rows, QB), F32)
    for jp in range(n_slc):
        r = score[jp:jp + 1, :]
        cnt = cnt + ((r > score) | ((r == score) & (jp < blk))).astype(F32)
    sel_t = ((cnt < float(min(N_SLC, n_slc))) & (blk < n_slc)).astype(F32)
    if n_rows < LANES:
        sel_t = jnp.concatenate([sel_t, jnp.zeros((LANES - n_rows, QB), F32)], axis=0)
    return jnp.transpose(sel_t).astype(BF16)


def _nsa_kernel(q_ref, ks_ref, vs_ref, kw_ref, vw_ref, kc_ref, vc_ref, gl_ref,
                cq_ref, sloq_ref, shiq_ref, ck_ref, slok_ref, shik_ref, c2s_ref, ex_ref,
                o_ref, ksx_ref, vsx_ref, kwb_ref, vwx_ref, bias_ref, kmax_ref, acc_ref, *, n_slc, n_cmp):
    step = pl.program_id(2)
    QB, HPG, dh = Q_BLOCK, NSA_HPG, HEAD_DIM
    T = ks_ref.shape[1]

    @pl.when(step == 0)
    def _():
        ks = _rope(ks_ref[0], ck_ref[...], slok_ref[...], shik_ref[...])
        ksx_ref[:, 0:dh] = ks.astype(BF16)
        ksx_ref[:, dh:2 * dh] = _aux_lanes((), False, T).astype(BF16)
        kmax_ref[...] = jnp.full(kmax_ref.shape, jnp.max(jnp.sum(ks * ks, axis=-1, keepdims=True)), F32)
        kwb_ref[...] = _rope(kw_ref[0], ck_ref[...], slok_ref[...], shik_ref[...]).astype(BF16)
        ones = jnp.ones((T, dh), BF16)
        vsx_ref[:, 0:dh] = vs_ref[0].astype(BF16)
        vsx_ref[:, dh:2 * dh] = ones
        vwx_ref[:, 0:dh] = vw_ref[0].astype(BF16)
        vwx_ref[:, dh:2 * dh] = ones

    qis = [step * NSA_QB + b for b in range(NSA_QB)]
    rows = lambda b, h: slice((b * HPG + h) * QB, (b * HPG + h + 1) * QB)
    qrows = lambda b: slice(b * QB, (b + 1) * QB)
    pieces, norms2 = [], []
    for b in range(NSA_QB):
        for h in range(HPG):
            x = q_ref[0, qrows(b), h * dh:(h + 1) * dh]
            x = _rope(x, cq_ref[qrows(b), :], sloq_ref[qrows(b), :], shiq_ref[qrows(b), :]) * (dh ** -0.5)
            pieces.append(x)
            norms2.append(jnp.sum(x * x, axis=-1, keepdims=True))
    q = jnp.concatenate(pieces, axis=0).astype(BF16)

    lane = lax.broadcasted_iota(jnp.int32, (QB, LANES), 1)
    sub = lax.broadcasted_iota(jnp.int32, (QB, LANES), 0)

    kc = kc_ref[0, 0, 0].astype(BF16)
    vc = vc_ref[0, 0, 0].astype(BF16)
    sc = _dot_nt(q, kc)
    o_cmp, sels = {}, []
    for b in range(NSA_QB):
        t = qis[b] * QB + sub
        mask_c = (lane * CMP_STRIDE + (CMP_LEN - 1) <= t) & (lane < n_cmp)
        mask_cf = mask_c.astype(F32)
        imp_c = jnp.zeros((QB, LANES), F32)
        for h in range(HPG):
            s = jnp.where(mask_c, sc[rows(b, h)], NEG)
            e = jnp.exp(s - jnp.max(s, axis=-1, keepdims=True)) * mask_cf
            p = e / jnp.maximum(jnp.sum(e, axis=-1, keepdims=True), TINY)
            imp_c = imp_c + p
            o_cmp[b, h] = _dot(p.astype(BF16), vc)
        imp = jnp.dot(imp_c, c2s_ref[...], preferred_element_type=F32, precision=lax.Precision.HIGHEST)
        sels.append(_select_blocks(imp, qis[b], n_slc))

    sel_all = jnp.concatenate(sels, axis=0)
    for c in range(T // SLC_CHUNK):
        selx = _dot(sel_all, ex_ref[:, c * SLC_CHUNK:(c + 1) * SLC_CHUNK])
        kpos = c * SLC_CHUNK + lax.broadcasted_iota(jnp.int32, (QB, SLC_CHUNK), 1)
        for b in range(NSA_QB):
            tq_ = qis[b] * QB + lax.broadcasted_iota(jnp.int32, (QB, SLC_CHUNK), 0)
            bias_ref[b, c] = jnp.where((selx[qrows(b)] > 0.5) & (kpos <= tq_), 0.0, NEG)

    o_win = {}
    for b in range(NSA_QB):
        wstart = pl.multiple_of(jnp.clip(qis[b] * QB - WIN, 0, T - WIN_SPAN), Q_BLOCK)
        qb = q[b * HPG * QB:(b + 1) * HPG * QB]
        sw = _dot_nt(qb, kwb_ref[pl.ds(wstart, WIN_SPAN), :])
        kpos = wstart + lax.broadcasted_iota(jnp.int32, (QB, WIN_SPAN), 1)
        tw = qis[b] * QB + lax.broadcasted_iota(jnp.int32, (QB, WIN_SPAN), 0)
        bias_w = jnp.where((kpos <= tw) & (kpos > tw - WIN), 0.0, NEG)
        pw = []
        for h in range(HPG):
            sh = sw[h * QB:(h + 1) * QB] + bias_w
            pw.append(jnp.exp(sh - jnp.max(sh, axis=-1, keepdims=True)).astype(BF16))
        ow = _dot(jnp.concatenate(pw, axis=0), vwx_ref[pl.ds(wstart, WIN_SPAN), :])
        for h in range(HPG):
            acc_w = ow[h * QB:(h + 1) * QB]
            o_win[b, h] = acc_w[:, 0:dh] / jnp.maximum(acc_w[:, dh:2 * dh], TINY)

    chains = [(b, h) for b in range(NSA_QB) for h in range(HPG)]

    n_chunks = (qis[-1] * QB + QB + SLC_CHUNK - 1) // SLC_CHUNK
    kmax2 = kmax_ref[0:1, 0:1]
    bounds = [jnp.sqrt(n2 * kmax2) * 1.01 for n2 in norms2]
    worst = functools.reduce(jnp.maximum, [jnp.max(bd) for bd in bounds])

    @pl.when(worst <= FOX_FAST_BOUND)
    def _():
        qx = jnp.concatenate(
            [jnp.concatenate([(x * LOG2E).astype(BF16),
                              _aux_lanes(_split3(-LOG2E * bd), True, QB).astype(BF16)], axis=1)
             for x, bd in zip(pieces, bounds)], axis=0)

        def body(c, accs):
            start = pl.multiple_of(c * SLC_CHUNK, SLC_CHUNK)
            s = _dot_nt(qx, ksx_ref[pl.ds(start, SLC_CHUNK), :])
            ps = [jnp.exp2(s[rows(b, h)] + bias_ref[b, c]).astype(BF16) for (b, h) in chains]
            pv = _dot(jnp.concatenate(ps, axis=0), vsx_ref[pl.ds(start, SLC_CHUNK), :])
            return tuple(accs[n] + pv[rows(b, h)] for n, (b, h) in enumerate(chains))

        accs = lax.fori_loop(0, n_chunks, body, tuple(jnp.zeros((QB, 2 * dh), F32) for _ in chains))
        for n, (b, h) in enumerate(chains):
            acc_ref[rows(b, h), :] = accs[n]

    @pl.when(worst > FOX_FAST_BOUND)
    def _():
        def body(c, carry):
            start = pl.multiple_of(c * SLC_CHUNK, SLC_CHUNK)
            s = _dot_nt(q, ksx_ref[pl.ds(start, SLC_CHUNK), 0:dh])
            ms, alphas, ps = [], [], []
            for n, (b, h) in enumerate(chains):
                sh = s[rows(b, h)] + bias_ref[b, c]
                m_new = jnp.maximum(carry[n][0], jnp.max(sh, axis=-1, keepdims=True))
                alphas.append(jnp.exp(carry[n][0] - m_new))
                ps.append(jnp.exp(sh - m_new).astype(BF16))
                ms.append(m_new)
            pv = _dot(jnp.concatenate(ps, axis=0), vsx_ref[pl.ds(start, SLC_CHUNK), :])
            return tuple((ms[n], alphas[n] * carry[n][1] + pv[rows(b, h)]) for n, (b, h) in enumerate(chains))

        init = tuple((jnp.full((QB, 1), NEG, F32), jnp.zeros((QB, 2 * dh), F32)) for _ in chains)
        slc = lax.fori_loop(0, n_chunks, body, init)
        for n, (b, h) in enumerate(chains):
            acc_ref[rows(b, h), :] = slc[n][1]

    for b in range(NSA_QB):
        gates = _sigmoid(gl_ref[0, qrows(b), :])
        for h in range(HPG):
            acc_s = acc_ref[rows(b, h), :]
            o_s = acc_s[:, 0:dh] / jnp.maximum(acc_s[:, dh:2 * dh], TINY)
            out = (gates[:, 3 * h:3 * h + 1] * o_cmp[b, h] + gates[:, 3 * h + 1:3 * h + 2] * o_s
                   + gates[:, 3 * h + 2:3 * h + 3] * o_win[b, h])
            o_ref[0, qrows(b), h * dh:(h + 1) * dh] = out.astype(o_ref.dtype)


def nsa_attention(qkv, cmp_kv, gl, tabs_q, B, T):
    G, HPG, dh = NSA_KV_HEADS, NSA_HPG, HEAD_DIM
    tq = NSA_QB * Q_BLOCK
    nb = T // CMP_STRIDE
    n_cmp = (T - CMP_LEN) // CMP_STRIDE + 1
    n_slc = T // SLC_LEN
    assert nb == LANES and n_slc <= LANES and T % SLC_CHUNK == 0 and T >= WIN_SPAN and T % tq == 0
    kvb = (HPG * G)

    c_start = np.arange(nb) * CMP_STRIDE
    s_start = np.arange(LANES) * SLC_LEN
    c2s = ((c_start[:, None] < s_start[None, :] + SLC_LEN) & (c_start[:, None] + CMP_LEN > s_start[None, :])
           & (np.arange(nb)[:, None] < n_cmp) & (np.arange(LANES)[None, :] < n_slc)).astype(np.float32)
    expand = (np.arange(T)[None, :] // SLC_LEN == np.arange(LANES)[:, None]).astype(np.float32)

    cos, slo, shi = tabs_q
    kv_full = lambda blk: pl.BlockSpec((1, T, dh), lambda b, g, i: (b, 0, kvb + blk + g))
    qtab = pl.BlockSpec((tq, dh), lambda b, g, i: (i, 0))
    ktab = pl.BlockSpec((T, dh), lambda b, g, i: (0, 0))
    return pl.pallas_call(
        functools.partial(_nsa_kernel, n_slc=n_slc, n_cmp=n_cmp),
        out_shape=jax.ShapeDtypeStruct((B, T, G * HPG * dh), BF16),
        grid=(B, G, T // tq),
        in_specs=[
            pl.BlockSpec((1, tq, HPG * dh), lambda b, g, i: (b, i, g)),
            kv_full(2 * G), kv_full(3 * G), kv_full(4 * G), kv_full(5 * G),
            pl.BlockSpec((1, 1, 1, nb, dh), lambda b, g, i: (b, 0, g, 0, 0)),
            pl.BlockSpec((1, 1, 1, nb, dh), lambda b, g, i: (b, 1, g, 0, 0)),
            pl.BlockSpec((1, tq, LANES), lambda b, g, i: (b, i, g)),
            qtab, qtab, qtab, ktab, ktab, ktab,
            pl.BlockSpec((nb, LANES), lambda b, g, i: (0, 0)),
            pl.BlockSpec((LANES, T), lambda b, g, i: (0, 0)),
        ],
        out_specs=pl.BlockSpec((1, tq, HPG * dh), lambda b, g, i: (b, i, g)),
        scratch_shapes=[pltpu.VMEM((T, 2 * dh), BF16), pltpu.VMEM((T, 2 * dh), BF16),
                        pltpu.VMEM((T, dh), BF16), pltpu.VMEM((T, 2 * dh), BF16),
                        pltpu.VMEM((NSA_QB, T // SLC_CHUNK, Q_BLOCK, SLC_CHUNK), F32),
                        pltpu.VMEM((8, LANES), F32),
                        pltpu.VMEM((NSA_QB * HPG * Q_BLOCK, 2 * dh), F32)],
        compiler_params=_cparams(("arbitrary", "arbitrary", "arbitrary")),
        name="nsa_attention",
    )(qkv, qkv, qkv, qkv, qkv, cmp_kv, cmp_kv, gl, cos, slo, shi, cos, slo, shi,
      jnp.asarray(c2s), jnp.asarray(expand, dtype=BF16))


CONV_PAD = 32
CONV_CHUNK = 256


def _conv_kernel(a_ref, b_ref, w_ref, cb_ref, g_ref, gb_ref, o_ref, u_ref):
    T = a_ref.shape[1]
    u_ref[0:CONV_PAD, :] = jnp.zeros((CONV_PAD, LANES), F32)
    u_ref[CONV_PAD:CONV_PAD + T, :] = a_ref[0].astype(F32) * _sigmoid(b_ref[0].astype(F32))
    base = CONV_PAD - (CONV_KERNEL - 1)
    for c in range(T // CONV_CHUNK):
        t0 = c * CONV_CHUNK
        acc = jnp.zeros((CONV_CHUNK, LANES), F32)
        for k in range(CONV_KERNEL):
            acc = acc + u_ref[t0 + base + k:t0 + base + k + CONV_CHUNK, :] * w_ref[k:k + 1, :]
        acc = acc + cb_ref[...]
        mu = jnp.mean(acc, axis=-1, keepdims=True)
        d = acc - mu
        var = jnp.mean(d * d, axis=-1, keepdims=True)
        y = d * lax.rsqrt(var + LN_EPS) * g_ref[...] + gb_ref[...]
        o_ref[0, t0:t0 + CONV_CHUNK, :] = (y * _sigmoid(y)).astype(o_ref.dtype)


def conv_module(glu, conv_w, conv_b, gn_g, gn_b, B, T):
    C = glu.shape[-1] // 2
    ng = C // LANES
    assert C // CONV_GROUPS == LANES
    vec = pl.BlockSpec((1, LANES), lambda b, g: (0, g))
    return pl.pallas_call(
        _conv_kernel,
        out_shape=jax.ShapeDtypeStruct((B, T, C), BF16),
        grid=(B, ng),
        in_specs=[
            pl.BlockSpec((1, T, LANES), lambda b, g: (b, 0, g)),
            pl.BlockSpec((1, T, LANES), lambda b, g: (b, 0, ng + g)),
            pl.BlockSpec((CONV_KERNEL, LANES), lambda b, g: (0, g)),
            vec, vec, vec,
        ],
        out_specs=pl.BlockSpec((1, T, LANES), lambda b, g: (b, 0, g)),
        scratch_shapes=[pltpu.VMEM((CONV_PAD + T, LANES), F32)],
        compiler_params=_cparams(("arbitrary", "arbitrary")),
        name="conformer_conv",
    )(glu, glu, conv_w, conv_b.reshape(1, C), gn_g.reshape(1, C), gn_b.reshape(1, C))


def _split2(a):
    hi = a.astype(BF16)
    return hi, (a - hi.astype(F32)).astype(BF16)


def _router_kernel(x_ref, w_ref, bias_ref, idx_ref, wts_ref):
    xh, xl = _split2(x_ref[...])
    wh, wl = _split2(w_ref[...])
    logits = _dot(xh, wh) + (_dot(xh, wl) + _dot(xl, wh))
    aff = _sigmoid(jnp.concatenate(
        [jnp.transpose(logits[c * LANES:(c + 1) * LANES])[0:N_EXPERTS] for c in range(logits.shape[0] // LANES)],
        axis=1))
    sel = aff + bias_ref[...]
    a = [aff[e:e + 1, :] for e in range(N_EXPERTS)]
    s = [sel[e:e + 1, :] for e in range(N_EXPERTS)]
    P = EXPERTS_PER_GROUP
    grp = []
    for g in range(N_EXPERT_GROUPS):
        v = s[g * P:(g + 1) * P]
        best = None
        for i in range(P):
            for j in range(i + 1, P):
                pair = v[i] + v[j]
                best = pair if best is None else jnp.maximum(best, pair)
        grp.append(best)
    gbest = jnp.zeros_like(grp[0], dtype=jnp.int32)
    gval = grp[0]
    for g in range(1, N_EXPERT_GROUPS):
        better = grp[g] > gval
        gbest = jnp.where(better, g, gbest)
        gval = jnp.where(better, grp[g], gval)
    cs, ca = [], []
    for p in range(P):
        sv, av = s[p], a[p]
        for g in range(1, N_EXPERT_GROUPS):
            sv = jnp.where(gbest == g, s[g * P + p], sv)
            av = jnp.where(gbest == g, a[g * P + p], av)
        cs.append(sv)
        ca.append(av)
    i1 = jnp.zeros_like(gbest)
    v1, a1 = cs[0], ca[0]
    for p in range(1, P):
        better = cs[p] > v1
        i1 = jnp.where(better, p, i1)
        v1 = jnp.where(better, cs[p], v1)
        a1 = jnp.where(better, ca[p], a1)
    i2 = jnp.full_like(gbest, -1)
    v2 = jnp.full_like(v1, -jnp.inf)
    a2 = jnp.zeros_like(a1)
    for p in range(P):
        better = (i1 != p) & (cs[p] > v2)
        i2 = jnp.where(better, p, i2)
        v2 = jnp.where(better, cs[p], v2)
        a2 = jnp.where(better, ca[p], a2)
    den = a1 + a2
    idx_ref[...] = jnp.zeros(idx_ref.shape, jnp.int32)
    wts_ref[...] = jnp.zeros(wts_ref.shape, F32)
    idx_ref[0:1, :] = gbest * P + i1
    idx_ref[1:2, :] = gbest * P + i2
    wts_ref[0:1, :] = a1 / den
    wts_ref[1:2, :] = a2 / den


def router(x, router_w, router_bias, *, tm=1024):
    N, D = x.shape
    E = N_EXPERTS
    return pl.pallas_call(
        _router_kernel,
        out_shape=(jax.ShapeDtypeStruct((8, N), jnp.int32), jax.ShapeDtypeStruct((8, N), F32)),
        grid=(N // tm,),
        in_specs=[
            pl.BlockSpec((tm, D), lambda i: (i, 0)),
            pl.BlockSpec((D, LANES), lambda i: (0, 0)),
            pl.BlockSpec((E, 1), lambda i: (0, 0)),
        ],
        out_specs=(pl.BlockSpec((8, tm), lambda i: (0, i)), pl.BlockSpec((8, tm), lambda i: (0, i))),
        compiler_params=_cparams(("arbitrary",)),
        name="moe_router",
    )(x, jnp.pad(router_w, ((0, 0), (0, LANES - E))), router_bias.reshape(E, 1))


PLAN_CHUNK = 512


def _plan_kernel(idx_ref, dest_ref, cnt_ref, *, n_tok):
    E = N_EXPERTS
    sub = lax.broadcasted_iota(jnp.int32, (TOP_K * E, n_tok), 0)
    tgt = jnp.where(sub < E, idx_ref[0:1, :], idx_ref[1:2, :])
    onehot = ((sub & (E - 1)) == tgt).astype(F32)
    onehot_b = onehot.astype(BF16)
    tri = (lax.broadcasted_iota(jnp.int32, (PLAN_CHUNK, PLAN_CHUNK), 0)
           <= lax.broadcasted_iota(jnp.int32, (PLAN_CHUNK, PLAN_CHUNK), 1)).astype(F32).astype(BF16)
    carry = jnp.zeros((TOP_K * E, 1), F32)
    parts = []
    for c in range(n_tok // PLAN_CHUNK):
        pre = _dot(onehot_b[:, c * PLAN_CHUNK:(c + 1) * PLAN_CHUNK], tri) + carry
        parts.append(pre)
        carry = pre[:, PLAN_CHUNK - 1:PLAN_CHUNK]
    excl = jnp.concatenate(parts, axis=1) - onehot
    cnt0 = carry[0:E]
    tot = cnt0 + carry[E:2 * E]
    lower = (lax.broadcasted_iota(jnp.int32, (E, E), 1)
             < lax.broadcasted_iota(jnp.int32, (E, E), 0)).astype(F32)
    offs = jnp.dot(lower, jnp.broadcast_to(tot, (E, LANES)), preferred_element_type=F32,
                   precision=lax.Precision.HIGHEST)[:, 0:1]
    base = jnp.concatenate([offs, offs + cnt0], axis=0)
    val = onehot * (base + excl)
    dest_ref[...] = jnp.zeros(dest_ref.shape, jnp.int32)
    dest_ref[0:1, :] = jnp.sum(val[0:E], axis=0, keepdims=True).astype(jnp.int32)
    dest_ref[1:2, :] = jnp.sum(val[E:2 * E], axis=0, keepdims=True).astype(jnp.int32)
    cnt_ref[...] = jnp.broadcast_to(tot, (E, LANES)).astype(jnp.int32)


def moe_plan(idx):
    n_tok = idx.shape[1]
    assert n_tok % PLAN_CHUNK == 0 and TOP_K == 2
    return pl.pallas_call(
        functools.partial(_plan_kernel, n_tok=n_tok),
        out_shape=(jax.ShapeDtypeStruct((8, n_tok), jnp.int32),
                   jax.ShapeDtypeStruct((N_EXPERTS, LANES), jnp.int32)),
        compiler_params=_cparams(None),
        name="moe_plan",
    )(idx)


def _scatter_kernel(dest_ref, x_ref, xs_hbm, sem, *, tm, n_tok):
    base = pl.program_id(0) * tm

    def copies(start):
        def make(r, k, d):
            cp = pltpu.make_async_copy(x_ref.at[pl.ds(r, 1)], xs_hbm.at[pl.ds(d, 1)], sem)
            if start:
                cp.start(priority=k)
            else:
                cp.wait()
        _row_copies(dest_ref, n_tok, base, tm, make)

    copies(True)
    copies(False)


def moe_scatter(x, dest, *, tm=256):
    N, D = x.shape
    grid_spec = pltpu.PrefetchScalarGridSpec(
        num_scalar_prefetch=1,
        grid=(N // tm,),
        in_specs=[pl.BlockSpec((tm, D), lambda i, d: (i, 0))],
        out_specs=pl.BlockSpec(memory_space=pl.ANY),
        scratch_shapes=[pltpu.SemaphoreType.DMA(())],
    )
    return pl.pallas_call(
        functools.partial(_scatter_kernel, tm=tm, n_tok=N),
        out_shape=jax.ShapeDtypeStruct((TOP_K * N, D), x.dtype),
        grid_spec=grid_spec,
        compiler_params=_cparams(("arbitrary",)),
        name="moe_scatter",
    )(dest, x)


def _moe_kernel(ti_ref, te_ref, lo_ref, hi_ref, first_ref, head_ref, slot_ref, nxt_ref,
                xs_ref, wg_hbm, wu_hbm, wd_hbm, y_ref,
                wg32_ref, wu32_ref, wd32_ref, wgb_ref, wub_ref, wdb_ref, sem, *, layer):
    i = pl.program_id(0)

    def weight_copies(e, slot):
        return [pltpu.make_async_copy(src.at[layer, e], dst.at[slot], sem.at[slot, n])
                for n, (src, dst) in enumerate(((wg_hbm, wg32_ref), (wu_hbm, wu32_ref), (wd_hbm, wd32_ref)))]

    @pl.when(i == 0)
    def _():
        for cp in weight_copies(te_ref[0], 0):
            cp.start()

    @pl.when(head_ref[i] == 1)
    def _():
        slot = slot_ref[i]
        for cp in weight_copies(te_ref[i], slot):
            cp.wait()
        wgb_ref[...] = wg32_ref[slot].astype(BF16)
        wub_ref[...] = wu32_ref[slot].astype(BF16)
        wdb_ref[...] = wd32_ref[slot].astype(BF16)

        @pl.when(nxt_ref[i] >= 0)
        def _():
            for cp in weight_copies(nxt_ref[i], 1 - slot):
                cp.start()

    lo, hi = lo_ref[i], hi_ref[i]

    @pl.when(hi > lo)
    def _():
        x = xs_ref[...].astype(BF16)
        g = _dot(x, wgb_ref[...])
        u = _dot(x, wub_ref[...])
        rows = lax.broadcasted_iota(jnp.int32, (x.shape[0], 1), 0)
        h = jnp.where((rows >= lo) & (rows < hi), (g * _sigmoid(g)) * u, 0.0)
        y = _dot(h.astype(BF16), wdb_ref[...])

        @pl.when(first_ref[i] == 1)
        def _():
            y_ref[...] = y

        @pl.when(first_ref[i] == 0)
        def _():
            y_ref[...] += y


def moe_experts(xs, items, w_gate, w_up, w_down, layer, *, tm=MOE_TM):
    P, D = xs.shape
    Fh = w_gate.shape[-1]
    n_items = items[0].shape[0]
    xmap = lambda i, ti, *_: (ti[i], 0)
    hbm = pl.BlockSpec(memory_space=pl.ANY)
    grid_spec = pltpu.PrefetchScalarGridSpec(
        num_scalar_prefetch=len(items),
        grid=(n_items,),
        in_specs=[pl.BlockSpec((tm, D), xmap), hbm, hbm, hbm],
        out_specs=pl.BlockSpec((tm, D), xmap),
        scratch_shapes=[pltpu.VMEM((2, D, Fh), F32), pltpu.VMEM((2, D, Fh), F32), pltpu.VMEM((2, Fh, D), F32),
                        pltpu.VMEM((D, Fh), BF16), pltpu.VMEM((D, Fh), BF16), pltpu.VMEM((Fh, D), BF16),
                        pltpu.SemaphoreType.DMA((2, 3))],
    )
    return pl.pallas_call(
        functools.partial(_moe_kernel, layer=layer),
        out_shape=jax.ShapeDtypeStruct((P, D), F32),
        grid_spec=grid_spec,
        compiler_params=_cparams(("arbitrary",)),
        name="moe_experts",
    )(*items, xs, w_gate, w_up, w_down)


def moe_items(tot, n_rows, tm=MOE_TM):
    E = N_EXPERTS
    n_max = n_rows // tm + E
    ar = jnp.arange(E, dtype=jnp.int32)
    ends = jnp.cumsum(tot)
    offs = ends - tot
    first_tile = offs // tm
    n_e = jnp.where(tot > 0, (ends - 1) // tm - first_tile + 1, 0)
    s_end = jnp.cumsum(n_e)
    s_beg = s_end - n_e
    n_items = s_end[-1]
    i = jnp.arange(n_max, dtype=jnp.int32)
    ic = jnp.minimum(i, n_items - 1)
    e_i = jnp.sum((ic[:, None] >= s_end[None, :]).astype(jnp.int32), axis=1)
    pick = (e_i[:, None] == ar[None, :]).astype(jnp.int32)
    at = lambda v: jnp.sum(pick * v[None, :], axis=1)
    tile = at(first_tile) + ic - at(s_beg)
    live = i < n_items
    lo = jnp.where(live, jnp.maximum(at(offs), tile * tm) - tile * tm, 0)
    hi = jnp.where(live, jnp.minimum(at(ends), tile * tm + tm) - tile * tm, 0)
    prev_tile = jnp.concatenate([jnp.full((1,), -1, jnp.int32), tile[:-1]])
    first = live & (tile != prev_tile)
    head = live & (ic == at(s_beg))
    used = (tot > 0).astype(jnp.int32)
    slot = at(jnp.cumsum(used) - used) & 1
    later = jnp.where((ar[None, :] > ar[:, None]) & (tot[None, :] > 0), ar[None, :], E)
    nxt_e = jnp.min(later, axis=1)
    nxt = at(jnp.where(nxt_e < E, nxt_e, -1))
    return tuple(v.astype(jnp.int32) for v in (tile, e_i, lo, hi, first, head, slot, nxt))


def _rope_tables(pos):
    inv = ROPE_THETA ** (-jnp.arange(0, ROPE_DIM, 2, dtype=F32) / ROPE_DIM)
    ang = pos.astype(F32)[:, None] * inv[None, :]
    cos, sin = jnp.cos(ang), jnp.sin(ang)
    n = pos.shape[0]
    rest = HEAD_DIM - ROPE_DIM
    cos_t = jnp.concatenate([cos, cos, jnp.ones((n, rest), F32)], axis=1)
    sin_lo = jnp.concatenate([-sin, jnp.zeros((n, HEAD_DIM - ROPE_HALF), F32)], axis=1)
    sin_hi = jnp.concatenate([jnp.zeros((n, ROPE_HALF), F32), sin, jnp.zeros((n, rest), F32)], axis=1)
    return cos_t, sin_lo, sin_hi


def _even_mixer(xb, B, T, w_in_all, j, pos, w1, w2, conv_w, conv_b, gn_g, gn_b):
    N, D = xb.shape
    G, HPG, dh = NSA_KV_HEADS, NSA_HPG, HEAD_DIM
    nsa_w = G * HPG * dh
    kv_cols = 3 * 2 * G * dh
    n_gate = 3 * G * HPG
    conv_c = (w_in_all.shape[2] - nsa_w - kv_cols - n_gate) // 2
    qkv = linear(xb, w_in_all, nsa_w + kv_cols, F32, layer=j, tm=512, tn=(nsa_w + kv_cols) // 2, name="even_qkv")
    w_gl = w_in_all[j, :, nsa_w + kv_cols:nsa_w + kv_cols + n_gate].reshape(D, G, 3 * HPG)
    w_gl = jnp.pad(w_gl, ((0, 0), (0, 0), (0, LANES - 3 * HPG))).reshape(D, G * LANES)
    gl = linear(xb, w_gl, G * LANES, F32, tn=G * LANES, name="even_gates")
    glu = linear(xb, w_in_all[j, :, nsa_w + kv_cols + n_gate:], 2 * conv_c, BF16, tn=1024, name="even_glu")

    qkv3 = qkv.reshape(B, T, nsa_w + kv_cols)
    n_cmp = (T - CMP_LEN) // CMP_STRIDE + 1
    nb = T // CMP_STRIDE
    cmp_end = jnp.arange(nb) * CMP_STRIDE + (CMP_LEN - 1)
    cmp_kv = compress(qkv3, pos, w1, w2, _rope_tables(cmp_end), B, T, G * HPG)
    o_nsa = nsa_attention(qkv3, cmp_kv, gl.reshape(B, T, G * LANES), _rope_tables(jnp.arange(T)), B, T)
    u = conv_module(glu.reshape(B, T, 2 * conv_c), conv_w, conv_b, gn_g, gn_b, B, T)
    return o_nsa.reshape(N, nsa_w), u.reshape(N, conv_c)


def _fox_mixer(xb, B, T, w_in_all, j, f_bias, q_gain, k_gain):
    N, D = xb.shape
    H = D // HEAD_DIM
    proj = linear(xb, w_in_all, 4 * D, BF16, layer=j, tn=1024, name="fox_qkvg")
    w_f = jnp.pad(w_in_all[j, :, 4 * D:], ((0, 0), (0, LANES - H)))
    fl = linear(xb, w_f, LANES, F32, tn=LANES, name="fox_forget")
    c = fox_decay(fl, jnp.pad(f_bias, (0, LANES - H)), B, T)[:, :H].reshape(B, T, H)
    return fox_attention(proj.reshape(B, T, 4 * D), c, q_gain, k_gain, B, T, H).reshape(N, D)


def _moe(xf, router_w, router_bias, w_gate, w_up, w_down, layer):
    N, D = xf.shape
    idx, wts = router(xf, router_w, router_bias)
    dest8, cnt = moe_plan(idx)
    dest = dest8[:TOP_K].reshape(-1)
    xs = moe_scatter(xf, dest)
    y = moe_experts(xs, moe_items(cnt[:, 0], TOP_K * N), w_gate, w_up, w_down, layer)
    return y, dest, wts[:TOP_K].T


def kernel(x, even_w_in, even_w_out, nsa_cmp_pos, nsa_cmp_w1, nsa_cmp_w2, conv_w, conv_b, conv_gn_g, conv_gn_b, fox_w_in, fox_f_bias, fox_q_gain, fox_k_gain, fox_w_out, ln_mix_g, ln_mix_b, ln_ffn_g, ln_ffn_b, router_w, router_bias, exp_w_gate, exp_w_up, exp_w_down):
    B, T, D = x.shape
    depth = ln_mix_g.shape[0]
    alpha = (2.0 * depth) ** 0.25
    N = B * T
    xf = x.reshape(N, D)
    xb = xf
    for layer in range(depth):
        j = layer // 2
        if layer % 2 == 0:
            a1, a2 = _even_mixer(xb, B, T, even_w_in, j, nsa_cmp_pos[j], nsa_cmp_w1[j], nsa_cmp_w2[j],
                                 conv_w[j], conv_b[j], conv_gn_g[j], conv_gn_b[j])
            xf, xb = outproj_ln(a1, a2, 0, even_w_out, j, xf, ln_mix_g[layer], ln_mix_b[layer], alpha)
        else:
            a = _fox_mixer(xb, B, T, fox_w_in, j, fox_f_bias[j], fox_q_gain[j], fox_k_gain[j])
            xf, xb = outproj_ln(a, a, 1, fox_w_out, j, xf, ln_mix_g[layer], ln_mix_b[layer], alpha)
        y, dest, wts = _moe(xf, router_w, router_bias, exp_w_gate, exp_w_up, exp_w_down, layer)
        xf, xb = combine_ln(xf, y, dest, wts, ln_ffn_g[layer], ln_ffn_b[layer], alpha)
    return xf.reshape(B, T, D)
```

```python
import functools

import numpy as np
import jax
import jax.numpy as jnp
from jax import lax
from jax.experimental import pallas as pl
from jax.experimental.pallas import tpu as pltpu

F32 = jnp.float32
BF16 = jnp.bfloat16

HEAD_DIM = 128
ROPE_THETA = 500000.0
ROPE_DIM = HEAD_DIM // 4
ROPE_HALF = ROPE_DIM // 2
Q_BLOCK = 128

NSA_KV_HEADS = 2
NSA_HPG = 4
CMP_LEN = 32
CMP_STRIDE = 16
SLC_LEN = 64
N_SLC = 8
WIN = 512
CONV_KERNEL = 31
CONV_GROUPS = 8

N_EXPERTS = 16
N_EXPERT_GROUPS = 4
EXPERTS_PER_GROUP = 4
TOP_K = 2

LN_EPS = 1e-5
NEG = -1e30
BIG = 1e9
TINY = 1e-30

LANES = 128
VMEM_LIMIT = 56 * 1024 * 1024

MOE_TM = 256


def _cparams(sem, vmem=VMEM_LIMIT):
    return pltpu.CompilerParams(dimension_semantics=sem, vmem_limit_bytes=vmem)


def _dot(a, b):
    return jnp.dot(a, b, preferred_element_type=F32)


def _dot_nt(a, b):
    return lax.dot_general(a, b, (((1,), (1,)), ((), ())), preferred_element_type=F32)


def _sigmoid(x):
    return 1.0 / (1.0 + jnp.exp(-x))


def _linear_kernel(x_ref, w_ref, o_ref, wb_ref):
    @pl.when(pl.program_id(1) == 0)
    def _():
        wb_ref[...] = w_ref[...].astype(BF16)

    o_ref[...] = _dot(x_ref[...].astype(BF16), wb_ref[...]).astype(o_ref.dtype)


def linear(x, w, n_cols, out_dtype, *, layer=None, tm=1024, tn=512, name="linear"):
    M, K = x.shape
    tm = min(tm, M)
    tn = min(tn, n_cols)
    assert M % tm == 0 and n_cols % tn == 0
    row_block = 0
    if layer is not None:
        w = w.reshape(-1, w.shape[-1])
        row_block = layer
    w_spec = pl.BlockSpec((K, tn), lambda j, i: (row_block, j))
    return pl.pallas_call(
        _linear_kernel,
        out_shape=jax.ShapeDtypeStruct((M, n_cols), out_dtype),
        grid=(n_cols // tn, M // tm),
        in_specs=[
            pl.BlockSpec((tm, K), lambda j, i: (i, 0)),
            w_spec,
        ],
        out_specs=pl.BlockSpec((tm, tn), lambda j, i: (i, j)),
        scratch_shapes=[pltpu.VMEM((K, tn), BF16)],
        compiler_params=_cparams(("arbitrary", "arbitrary")),
        name=name,
    )(x, w)


def _layer_norm_rows(y, g, b):
    mu = jnp.mean(y, axis=-1, keepdims=True)
    d = y - mu
    var = jnp.mean(d * d, axis=-1, keepdims=True)
    return d * lax.rsqrt(var + LN_EPS) * g + b


def _outproj_ln_kernel(a1_ref, a2_ref, w_ref, x_ref, g_ref, b_ref, xo_ref, xb_ref, wb_ref, *, alpha):
    @pl.when(pl.program_id(0) == 0)
    def _():
        wb_ref[...] = w_ref[...].astype(BF16)

    half = a1_ref.shape[1]
    h = _dot(a1_ref[...].astype(BF16), wb_ref[0:half, :])
    h = h + _dot(a2_ref[...].astype(BF16), wb_ref[half:2 * half, :])
    out = _layer_norm_rows(alpha * x_ref[...] + h, g_ref[...], b_ref[...])
    xo_ref[...] = out
    xb_ref[...] = out.astype(BF16)


def outproj_ln(a1, a2, a2_col_block, w, layer, x, g, b, alpha, *, tm=512):
    M, D = x.shape
    half = D // 2
    return pl.pallas_call(
        functools.partial(_outproj_ln_kernel, alpha=alpha),
        out_shape=(jax.ShapeDtypeStruct((M, D), F32), jax.ShapeDtypeStruct((M, D), BF16)),
        grid=(M // tm,),
        in_specs=[
            pl.BlockSpec((tm, half), lambda i: (i, 0)),
            pl.BlockSpec((tm, half), lambda i: (i, a2_col_block)),
            pl.BlockSpec((None, D, D), lambda i: (layer, 0, 0), pipeline_mode=pl.Buffered(1)),
            pl.BlockSpec((tm, D), lambda i: (i, 0)),
            pl.BlockSpec((1, D), lambda i: (0, 0)),
            pl.BlockSpec((1, D), lambda i: (0, 0)),
        ],
        out_specs=(pl.BlockSpec((tm, D), lambda i: (i, 0)), pl.BlockSpec((tm, D), lambda i: (i, 0))),
        scratch_shapes=[pltpu.VMEM((D, D), BF16)],
        compiler_params=_cparams(("arbitrary",)),
        name="outproj_ln",
    )(a1, a2, w, x, g.reshape(1, D), b.reshape(1, D))


ROW_DMA_UNROLL = 8


def _row_copies(dest_ref, n_tok, base, tm, make):
    def body(r, carry):
        for k in range(TOP_K):
            make(r, k, dest_ref[k * n_tok + base + r])
        return carry

    lax.fori_loop(0, tm, body, 0, unroll=ROW_DMA_UNROLL)


def _combine_ln_kernel(dest_ref, x_ref, w_ref, g_ref, b_ref, y_hbm, xo_ref, xb_ref, buf_ref, sem, *,
                       alpha, tm, n_tok):
    i = pl.program_id(0)
    n_steps = pl.num_programs(0)

    def copies(step, slot, start):
        def make(r, k, d):
            cp = pltpu.make_async_copy(y_hbm.at[pl.ds(d, 1)], buf_ref.at[slot, k, pl.ds(r, 1)], sem.at[slot])
            if start:
                cp.start(priority=k)
            else:
                cp.wait()
        _row_copies(dest_ref, n_tok, step * tm, tm, make)

    @pl.when(i == 0)
    def _():
        copies(0, 0, True)

    @pl.when(i + 1 < n_steps)
    def _():
        copies(i + 1, (i + 1) % 2, True)

    slot = i % 2
    copies(i, slot, False)
    f = w_ref[:, 0:1] * buf_ref[slot, 0] + w_ref[:, 1:2] * buf_ref[slot, 1]
    out = _layer_norm_rows(alpha * x_ref[...] + f, g_ref[...], b_ref[...])
    xo_ref[...] = out
    xb_ref[...] = out.astype(BF16)


def combine_ln(x, y, dest, wts, g, b, alpha, *, tm=256):
    N, D = x.shape
    row = lambda i, d: (i, 0)
    grid_spec = pltpu.PrefetchScalarGridSpec(
        num_scalar_prefetch=1,
        grid=(N // tm,),
        in_specs=[
            pl.BlockSpec((tm, D), row),
            pl.BlockSpec((tm, TOP_K), row),
            pl.BlockSpec((1, D), lambda i, d: (0, 0)),
            pl.BlockSpec((1, D), lambda i, d: (0, 0)),
            pl.BlockSpec(memory_space=pl.ANY),
        ],
        out_specs=(pl.BlockSpec((tm, D), row), pl.BlockSpec((tm, D), row)),
        scratch_shapes=[pltpu.VMEM((2, TOP_K, tm, D), F32), pltpu.SemaphoreType.DMA((2,))],
    )
    return pl.pallas_call(
        functools.partial(_combine_ln_kernel, alpha=alpha, tm=tm, n_tok=N),
        out_shape=(jax.ShapeDtypeStruct((N, D), F32), jax.ShapeDtypeStruct((N, D), BF16)),
        grid_spec=grid_spec,
        compiler_params=_cparams(("arbitrary",)),
        name="moe_combine_ln",
    )(dest, x, wts, g.reshape(1, D), b.reshape(1, D), y)


def _rms(x, gain):
    return x * lax.rsqrt(jnp.mean(x * x, axis=-1, keepdims=True) + LN_EPS) * gain


CUM_CHUNK = 256


def _decay_kernel(fl_ref, bias_ref, c_ref):
    T = fl_ref.shape[0]
    tri = (lax.broadcasted_iota(jnp.int32, (CUM_CHUNK, CUM_CHUNK), 0)
           >= lax.broadcasted_iota(jnp.int32, (CUM_CHUNK, CUM_CHUNK), 1)).astype(F32)
    carry = jnp.zeros((1, LANES), F32)
    for c in range(T // CUM_CHUNK):
        z = fl_ref[c * CUM_CHUNK:(c + 1) * CUM_CHUNK, :] + bias_ref[...]
        log_f = jnp.minimum(z, 0.0) - jnp.log1p(jnp.exp(-jnp.abs(z)))
        cs = jnp.dot(tri, log_f, preferred_element_type=F32, precision=lax.Precision.HIGHEST) + carry
        c_ref[c * CUM_CHUNK:(c + 1) * CUM_CHUNK, :] = cs
        carry = cs[CUM_CHUNK - 1:CUM_CHUNK, :]


def fox_decay(fl, bias, B, T):
    assert T % CUM_CHUNK == 0
    return pl.pallas_call(
        _decay_kernel,
        out_shape=jax.ShapeDtypeStruct(fl.shape, F32),
        grid=(B,),
        in_specs=[pl.BlockSpec((T, LANES), lambda b: (b, 0)), pl.BlockSpec((1, LANES), lambda b: (0, 0))],
        out_specs=pl.BlockSpec((T, LANES), lambda b: (b, 0)),
        compiler_params=_cparams(("arbitrary",)),
        name="fox_decay",
    )(fl, bias.reshape(1, LANES))


LOG2E = 1.4426950408889634
FOX_FAST_BOUND = 38.0
FOX_AUX = 3


def _split3(a):
    hi = a.astype(BF16).astype(F32)
    r = a - hi
    mid = r.astype(BF16).astype(F32)
    return hi, mid, (r - mid).astype(BF16).astype(F32)


def _aux_lanes(pieces, ones_first, n):
    lane = lax.broadcasted_iota(jnp.int32, (n, LANES), 1)
    p0 = FOX_AUX if ones_first else 0
    o0 = 0 if ones_first else FOX_AUX
    out = jnp.where((lane >= o0) & (lane < o0 + FOX_AUX), 1.0, 0.0)
    for i, piece in enumerate(pieces):
        out = out + jnp.where(lane == p0 + i, piece, 0.0)
    return out


def _fox_kernel(fast_ref, q_ref, k_ref, v_ref, og_ref, ckrow_ref, ckcol_ref, cq_ref, bound_ref, qg_ref, kg_ref,
                o_ref, kx_ref, vx_ref, *, tq, tk, hb):
    qi = pl.program_id(2)
    dh = HEAD_DIM
    T = k_ref.shape[1]

    @pl.when(qi == 0)
    def _():
        for h in range(hb):
            kx_ref[h, :, 0:dh] = _rms(k_ref[0, :, h * dh:(h + 1) * dh].astype(F32), kg_ref[...]).astype(BF16)
            kx_ref[h, :, dh:2 * dh] = _aux_lanes(_split3(-LOG2E * ckcol_ref[0, 0, :, h:h + 1]), False, T
                                                 ).astype(BF16)
            vx_ref[h, :, 0:dh] = v_ref[0, :, h * dh:(h + 1) * dh].astype(BF16)
            vx_ref[h, :, dh:2 * dh] = jnp.ones((T, dh), BF16)

    row = qi * tq + lax.broadcasted_iota(jnp.int32, (tq, tk), 0)
    lane = lax.broadcasted_iota(jnp.int32, (tq, tk), 1)
    n_full = (qi * tq) // tk

    def finish(accs):
        for h in range(hb):
            o = accs[h][:, 0:dh] / jnp.maximum(accs[h][:, dh:2 * dh], TINY)
            gate = _sigmoid(og_ref[0, :, h * dh:(h + 1) * dh].astype(F32))
            o_ref[0, :, h * dh:(h + 1) * dh] = (o * gate).astype(o_ref.dtype)

    def q_normed(h):
        return _rms(q_ref[0, :, h * dh:(h + 1) * dh].astype(F32), qg_ref[...]) * (dh ** -0.5)

    @pl.when(fast_ref[0] == 1)
    def _():
        qx = []
        for h in range(hb):
            r = LOG2E * (cq_ref[0, 0, :, h:h + 1] - bound_ref[0:1, 0:1])
            qx.append(jnp.concatenate([(q_normed(h) * LOG2E).astype(BF16),
                                       _aux_lanes(_split3(r), True, tq).astype(BF16)], axis=1))

        def step(j, accs, masked):
            start = pl.multiple_of(j * tk, tk)
            out = []
            for h in range(hb):
                s = _dot_nt(qx[h], kx_ref[h, pl.ds(start, tk), :])
                if masked:
                    s = jnp.where(j * tk + lane <= row, s, NEG)
                out.append(accs[h] + _dot(jnp.exp2(s).astype(BF16), vx_ref[h, pl.ds(start, tk), :]))
            return tuple(out)

        def diagonal(accs):
            hq = tq // 2
            start = pl.multiple_of(n_full * tk, tk)
            tri = (lax.broadcasted_iota(jnp.int32, (hq, hq), 1) <= lax.broadcasted_iota(jnp.int32, (hq, hq), 0))
            out = []
            for h in range(hb):
                k_lo, k_hi = kx_ref[h, pl.ds(start, hq), :], kx_ref[h, pl.ds(start + hq, hq), :]
                v_lo, v_all = vx_ref[h, pl.ds(start, hq), :], vx_ref[h, pl.ds(start, tk), :]
                p_top = jnp.exp2(jnp.where(tri, _dot_nt(qx[h][0:hq], k_lo), NEG)).astype(BF16)
                p_bot = jnp.concatenate(
                    [jnp.exp2(_dot_nt(qx[h][hq:tq], k_lo)).astype(BF16),
                     jnp.exp2(jnp.where(tri, _dot_nt(qx[h][hq:tq], k_hi), NEG)).astype(BF16)], axis=1)
                out.append(accs[h] + jnp.concatenate([_dot(p_top, v_lo), _dot(p_bot, v_all)], axis=0))
            return out

        accs = tuple(jnp.zeros((tq, 2 * dh), F32) for _ in range(hb))
        accs = lax.fori_loop(0, n_full, lambda j, a: step(j, a, False), accs)
        finish(diagonal(accs) if tq == tk else step(n_full, accs, True))

    @pl.when(fast_ref[0] == 0)
    def _():
        qs = [q_normed(h).astype(BF16) for h in range(hb)]

        def step(j, carry, masked):
            start = pl.multiple_of(j * tk, tk)
            out = []
            for h in range(hb):
                m, acc = carry[h]
                s = _dot_nt(qs[h], kx_ref[h, pl.ds(start, tk), 0:dh]) - ckrow_ref[0, h, pl.ds(j, 1), :]
                if masked:
                    s = jnp.where(j * tk + lane <= row, s, NEG)
                m_new = jnp.maximum(m, jnp.max(s, axis=-1, keepdims=True))
                p = jnp.exp(s - m_new).astype(BF16)
                acc = jnp.exp(m - m_new) * acc + _dot(p, vx_ref[h, pl.ds(start, tk), :])
                out.append((m_new, acc))
            return tuple(out)

        init = tuple((jnp.full((tq, 1), NEG, F32), jnp.zeros((tq, 2 * dh), F32)) for _ in range(hb))
        carry = lax.fori_loop(0, n_full, lambda j, c: step(j, c, False), init)
        finish([c[1] for c in step(n_full, carry, True)])


def fox_attention(proj, c, q_gain, k_gain, B, T, H, *, tq=512, tk=512, hb=4):
    nq = T // tq
    nk = T // tk
    dh = HEAD_DIM
    hg = H // hb
    c_row = c.transpose(0, 2, 1).reshape(B, H, nk, tk)
    c_col = c.reshape(B, T, hg, hb).transpose(0, 2, 1, 3)
    bound = jnp.max(jnp.abs(q_gain)) * jnp.max(jnp.abs(k_gain)) * (dh ** 0.5) * 1.01
    fast = (bound <= FOX_FAST_BOUND).astype(jnp.int32).reshape(1)
    full = lambda off: pl.BlockSpec((1, T, hb * dh), lambda b, h, i, f: (b, 0, off + h))
    tile = lambda off: pl.BlockSpec((1, tq, hb * dh), lambda b, h, i, f: (b, i, off + h))
    vec = pl.BlockSpec((1, dh), lambda b, h, i, f: (0, 0))
    grid_spec = pltpu.PrefetchScalarGridSpec(
        num_scalar_prefetch=1,
        grid=(B, hg, nq),
        in_specs=[
            tile(0), full(hg), full(2 * hg), tile(3 * hg),
            pl.BlockSpec((1, hb, nk, tk), lambda b, h, i, f: (b, h, 0, 0)),
            pl.BlockSpec((1, 1, T, hb), lambda b, h, i, f: (b, h, 0, 0)),
            pl.BlockSpec((1, 1, tq, hb), lambda b, h, i, f: (b, h, i, 0)),
            vec, vec, vec,
        ],
        out_specs=pl.BlockSpec((1, tq, hb * dh), lambda b, h, i, f: (b, i, h)),
        scratch_shapes=[pltpu.VMEM((hb, T, 2 * dh), BF16), pltpu.VMEM((hb, T, 2 * dh), BF16)],
    )
    return pl.pallas_call(
        functools.partial(_fox_kernel, tq=tq, tk=tk, hb=hb),
        out_shape=jax.ShapeDtypeStruct((B, T, H * dh), BF16),
        grid_spec=grid_spec,
        compiler_params=_cparams(("arbitrary", "arbitrary", "arbitrary")),
        name="fox_attention",
    )(fast, proj, proj, proj, proj, c_row, c_col, c_col, jnp.full((1, dh), bound, F32),
      q_gain.reshape(1, dh), k_gain.reshape(1, dh))


def _rope(x, cos, sin_lo, sin_hi):
    return (x * cos + pltpu.roll(x, LANES - ROPE_HALF, 1) * sin_lo
            + pltpu.roll(x, ROPE_HALF, 1) * sin_hi)


def _gelu_tanh(x):
    return 0.5 * x * (1.0 + jnp.tanh(0.7978845608028654 * (x + 0.044715 * (x * x * x))))


def _compress_kernel(raw_ref, pos_ref, w1_ref, w2_ref, cos_ref, slo_ref, shi_ref, o_ref, *, n_cmp):
    nb = o_ref.shape[-2]
    half = CMP_LEN // 2
    acc_a = jnp.zeros((nb, HEAD_DIM), F32)
    acc_b = jnp.zeros((nb, HEAD_DIM), F32)
    for l in range(half):
        rl = raw_ref[0, pl.ds(l, nb, stride=CMP_STRIDE), :]
        wa = w1_ref[0, l * HEAD_DIM:(l + 1) * HEAD_DIM, :].astype(BF16)
        wb = w1_ref[0, (half + l) * HEAD_DIM:(half + l + 1) * HEAD_DIM, :].astype(BF16)
        acc_a = acc_a + _dot((rl + pos_ref[0, l:l + 1, :]).astype(BF16), wa)
        acc_b = acc_b + _dot((rl + pos_ref[0, half + l:half + l + 1, :]).astype(BF16), wb)
    pre = acc_a + pltpu.roll(acc_b, nb - 1, 0)
    out = _dot(_gelu_tanh(pre).astype(BF16), w2_ref[0].astype(BF16))
    roped = _rope(out, cos_ref[...], slo_ref[...], shi_ref[...])
    out = jnp.where(pl.program_id(1) == 0, roped, out)
    rows = lax.broadcasted_iota(jnp.int32, out.shape, 0)
    o_ref[0, 0, 0] = jnp.where(rows < n_cmp, out, 0.0)


def compress(qkv, pos, w1, w2, tabs, B, T, kv_block0):
    G = NSA_KV_HEADS
    nb = T // CMP_STRIDE
    n_cmp = (T - CMP_LEN) // CMP_STRIDE + 1
    cos, slo, shi = tabs
    tab = pl.BlockSpec((nb, HEAD_DIM), lambda b, kv, g: (0, 0))
    return pl.pallas_call(
        functools.partial(_compress_kernel, n_cmp=n_cmp),
        out_shape=jax.ShapeDtypeStruct((B, 2, G, nb, HEAD_DIM), F32),
        grid=(B, 2, G),
        in_specs=[
            pl.BlockSpec((1, T, HEAD_DIM), lambda b, kv, g: (b, 0, kv_block0 + kv * G + g)),
            pl.BlockSpec((1, CMP_LEN, HEAD_DIM), lambda b, kv, g: (kv, 0, 0)),
            pl.BlockSpec((1, CMP_LEN * HEAD_DIM, HEAD_DIM), lambda b, kv, g: (kv, 0, 0)),
            pl.BlockSpec((1, HEAD_DIM, HEAD_DIM), lambda b, kv, g: (kv, 0, 0)),
            tab, tab, tab,
        ],
        out_specs=pl.BlockSpec((1, 1, 1, nb, HEAD_DIM), lambda b, kv, g: (b, kv, g, 0, 0)),
        compiler_params=_cparams(("arbitrary", "arbitrary", "arbitrary")),
        name="nsa_compress",
    )(qkv, pos, w1, w2, cos, slo, shi)


SLC_CHUNK = 512
WIN_SPAN = WIN + Q_BLOCK
NSA_QB = 4


def _select_blocks(imp, qi, n_slc):
    QB = Q_BLOCK
    n_rows = -(-n_slc // 8) * 8
    imp_t = jnp.transpose(imp)[0:n_rows]
    blk = lax.broadcasted_iota(jnp.int32, (n_rows, QB), 0)
    t = qi * QB + lax.broadcasted_iota(jnp.int32, (n_rows, QB), 1)
    cur = jnp.right_shift(t, 6)
    forced = (blk == 0) | (blk == cur) | (blk == cur - 1)
    score = jnp.where(forced, BIG, jnp.where(blk * SLC_LEN <= t, imp_t, -BIG))
    score = jnp.where(blk < n_slc, score, -2.0 * BIG)
    cnt = jnp.zeros((n_rows, QB), F32)
    for jp in range(n_slc):
        r = score[jp:jp + 1, :]
        cnt = cnt + ((r > score) | ((r == score) & (jp < blk))).astype(F32)
    sel_t = ((cnt < float(min(N_SLC, n_slc))) & (blk < n_slc)).astype(F32)
    if n_rows < LANES:
        sel_t = jnp.concatenate([sel_t, jnp.zeros((LANES - n_rows, QB), F32)], axis=0)
    return jnp.transpose(sel_t).astype(BF16)


def _nsa_kernel(q_ref, ks_ref, vs_ref, kw_ref, vw_ref, kc_ref, vc_ref, gl_ref,
                cq_ref, sloq_ref, shiq_ref, ck_ref, slok_ref, shik_ref, c2s_ref, ex_ref,
                o_ref, ksx_ref, vsx_ref, kwb_ref, vwx_ref, bias_ref, kmax_ref, acc_ref, *, n_slc, n_cmp):
    step = pl.program_id(2)
    QB, HPG, dh = Q_BLOCK, NSA_HPG, HEAD_DIM
    T = ks_ref.shape[1]

    @pl.when(step == 0)
    def _():
        ks = _rope(ks_ref[0], ck_ref[...], slok_ref[...], shik_ref[...])
        ksx_ref[:, 0:dh] = ks.astype(BF16)
        ksx_ref[:, dh:2 * dh] = _aux_lanes((), False, T).astype(BF16)
        kmax_ref[...] = jnp.full(kmax_ref.shape, jnp.max(jnp.sum(ks * ks, axis=-1, keepdims=True)), F32)
        kwb_ref[...] = _rope(kw_ref[0], ck_ref[...], slok_ref[...], shik_ref[...]).astype(BF16)
        ones = jnp.ones((T, dh), BF16)
        vsx_ref[:, 0:dh] = vs_ref[0].astype(BF16)
        vsx_ref[:, dh:2 * dh] = ones
        vwx_ref[:, 0:dh] = vw_ref[0].astype(BF16)
        vwx_ref[:, dh:2 * dh] = ones

    qis = [step * NSA_QB + b for b in range(NSA_QB)]
    rows = lambda b, h: slice((b * HPG + h) * QB, (b * HPG + h + 1) * QB)
    qrows = lambda b: slice(b * QB, (b + 1) * QB)
    pieces, norms2 = [], []
    for b in range(NSA_QB):
        for h in range(HPG):
            x = q_ref[0, qrows(b), h * dh:(h + 1) * dh]
            x = _rope(x, cq_ref[qrows(b), :], sloq_ref[qrows(b), :], shiq_ref[qrows(b), :]) * (dh ** -0.5)
            pieces.append(x)
            norms2.append(jnp.sum(x * x, axis=-1, keepdims=True))
    q = jnp.concatenate(pieces, axis=0).astype(BF16)

    lane = lax.broadcasted_iota(jnp.int32, (QB, LANES), 1)
    sub = lax.broadcasted_iota(jnp.int32, (QB, LANES), 0)

    kc = kc_ref[0, 0, 0].astype(BF16)
    vc = vc_ref[0, 0, 0].astype(BF16)
    sc = _dot_nt(q, kc)
    o_cmp, sels = {}, []
    for b in range(NSA_QB):
        t = qis[b] * QB + sub
        mask_c = (lane * CMP_STRIDE + (CMP_LEN - 1) <= t) & (lane < n_cmp)
        mask_cf = mask_c.astype(F32)
        imp_c = jnp.zeros((QB, LANES), F32)
        for h in range(HPG):
            s = jnp.where(mask_c, sc[rows(b, h)], NEG)
            e = jnp.exp(s - jnp.max(s, axis=-1, keepdims=True)) * mask_cf
            p = e / jnp.maximum(jnp.sum(e, axis=-1, keepdims=True), TINY)
            imp_c = imp_c + p
            o_cmp[b, h] = _dot(p.astype(BF16), vc)
        imp = jnp.dot(imp_c, c2s_ref[...], preferred_element_type=F32, precision=lax.Precision.HIGHEST)
        sels.append(_select_blocks(imp, qis[b], n_slc))

    sel_all = jnp.concatenate(sels, axis=0)
    for c in range(T // SLC_CHUNK):
        selx = _dot(sel_all, ex_ref[:, c * SLC_CHUNK:(c + 1) * SLC_CHUNK])
        kpos = c * SLC_CHUNK + lax.broadcasted_iota(jnp.int32, (QB, SLC_CHUNK), 1)
        for b in range(NSA_QB):
            tq_ = qis[b] * QB + lax.broadcasted_iota(jnp.int32, (QB, SLC_CHUNK), 0)
            bias_ref[b, c] = jnp.where((selx[qrows(b)] > 0.5) & (kpos <= tq_), 0.0, NEG)

    o_win = {}
    for b in range(NSA_QB):
        wstart = pl.multiple_of(jnp.clip(qis[b] * QB - WIN, 0, T - WIN_SPAN), Q_BLOCK)
        qb = q[b * HPG * QB:(b + 1) * HPG * QB]
        sw = _dot_nt(qb, kwb_ref[pl.ds(wstart, WIN_SPAN), :])
        kpos = wstart + lax.broadcasted_iota(jnp.int32, (QB, WIN_SPAN), 1)
        tw = qis[b] * QB + lax.broadcasted_iota(jnp.int32, (QB, WIN_SPAN), 0)
        bias_w = jnp.where((kpos <= tw) & (kpos > tw - WIN), 0.0, NEG)
        pw = []
        for h in range(HPG):
            sh = sw[h * QB:(h + 1) * QB] + bias_w
            pw.append(jnp.exp(sh - jnp.max(sh, axis=-1, keepdims=True)).astype(BF16))
        ow = _dot(jnp.concatenate(pw, axis=0), vwx_ref[pl.ds(wstart, WIN_SPAN), :])
        for h in range(HPG):
            acc_w = ow[h * QB:(h + 1) * QB]
            o_win[b, h] = acc_w[:, 0:dh] / jnp.maximum(acc_w[:, dh:2 * dh], TINY)

    chains = [(b, h) for b in range(NSA_QB) for h in range(HPG)]

    n_chunks = (qis[-1] * QB + QB + SLC_CHUNK - 1) // SLC_CHUNK
    kmax2 = kmax_ref[0:1, 0:1]
    bounds = [jnp.sqrt(n2 * kmax2) * 1.01 for n2 in norms2]
    worst = functools.reduce(jnp.maximum, [jnp.max(bd) for bd in bounds])

    @pl.when(worst <= FOX_FAST_BOUND)
    def _():
        qx = jnp.concatenate(
            [jnp.concatenate([(x * LOG2E).astype(BF16),
                              _aux_lanes(_split3(-LOG2E * bd), True, QB).astype(BF16)], axis=1)
             for x, bd in zip(pieces, bounds)], axis=0)

        def body(c, accs):
            start = pl.multiple_of(c * SLC_CHUNK, SLC_CHUNK)
            s = _dot_nt(qx, ksx_ref[pl.ds(start, SLC_CHUNK), :])
            ps = [jnp.exp2(s[rows(b, h)] + bias_ref[b, c]).astype(BF16) for (b, h) in chains]
            pv = _dot(jnp.concatenate(ps, axis=0), vsx_ref[pl.ds(start, SLC_CHUNK), :])
            return tuple(accs[n] + pv[rows(b, h)] for n, (b, h) in enumerate(chains))

        accs = lax.fori_loop(0, n_chunks, body, tuple(jnp.zeros((QB, 2 * dh), F32) for _ in chains))
        for n, (b, h) in enumerate(chains):
            acc_ref[rows(b, h), :] = accs[n]

    @pl.when(worst > FOX_FAST_BOUND)
    def _():
        def body(c, carry):
            start = pl.multiple_of(c * SLC_CHUNK, SLC_CHUNK)
            s = _dot_nt(q, ksx_ref[pl.ds(start, SLC_CHUNK), 0:dh])
            ms, alphas, ps = [], [], []
            for n, (b, h) in enumerate(chains):
                sh = s[rows(b, h)] + bias_ref[b, c]
                m_new = jnp.maximum(carry[n][0], jnp.max(sh, axis=-1, keepdims=True))
                alphas.append(jnp.exp(carry[n][0] - m_new))
                ps.append(jnp.exp(sh - m_new).astype(BF16))
                ms.append(m_new)
            pv = _dot(jnp.concatenate(ps, axis=0), vsx_ref[pl.ds(start, SLC_CHUNK), :])
            return tuple((ms[n], alphas[n] * carry[n][1] + pv[rows(b, h)]) for n, (b, h) in enumerate(chains))

        init = tuple((jnp.full((QB, 1), NEG, F32), jnp.zeros((QB, 2 * dh), F32)) for _ in chains)
        slc = lax.fori_loop(0, n_chunks, body, init)
        for n, (b, h) in enumerate(chains):
            acc_ref[rows(b, h), :] = slc[n][1]

    for b in range(NSA_QB):
        gates = _sigmoid(gl_ref[0, qrows(b), :])
        for h in range(HPG):
            acc_s = acc_ref[rows(b, h), :]
            o_s = acc_s[:, 0:dh] / jnp.maximum(acc_s[:, dh:2 * dh], TINY)
            out = (gates[:, 3 * h:3 * h + 1] * o_cmp[b, h] + gates[:, 3 * h + 1:3 * h + 2] * o_s
                   + gates[:, 3 * h + 2:3 * h + 3] * o_win[b, h])
            o_ref[0, qrows(b), h * dh:(h + 1) * dh] = out.astype(o_ref.dtype)


def nsa_attention(qkv, cmp_kv, gl, tabs_q, B, T):
    G, HPG, dh = NSA_KV_HEADS, NSA_HPG, HEAD_DIM
    tq = NSA_QB * Q_BLOCK
    nb = T // CMP_STRIDE
    n_cmp = (T - CMP_LEN) // CMP_STRIDE + 1
    n_slc = T // SLC_LEN
    assert nb == LANES and n_slc <= LANES and T % SLC_CHUNK == 0 and T >= WIN_SPAN and T % tq == 0
    kvb = (HPG * G)

    c_start = np.arange(nb) * CMP_STRIDE
    s_start = np.arange(LANES) * SLC_LEN
    c2s = ((c_start[:, None] < s_start[None, :] + SLC_LEN) & (c_start[:, None] + CMP_LEN > s_start[None, :])
           & (np.arange(nb)[:, None] < n_cmp) & (np.arange(LANES)[None, :] < n_slc)).astype(np.float32)
    expand = (np.arange(T)[None, :] // SLC_LEN == np.arange(LANES)[:, None]).astype(np.float32)

    cos, slo, shi = tabs_q
    kv_full = lambda blk: pl.BlockSpec((1, T, dh), lambda b, g, i: (b, 0, kvb + blk + g))
    qtab = pl.BlockSpec((tq, dh), lambda b, g, i: (i, 0))
    ktab = pl.BlockSpec((T, dh), lambda b, g, i: (0, 0))
    return pl.pallas_call(
        functools.partial(_nsa_kernel, n_slc=n_slc, n_cmp=n_cmp),
        out_shape=jax.ShapeDtypeStruct((B, T, G * HPG * dh), BF16),
        grid=(B, G, T // tq),
        in_specs=[
            pl.BlockSpec((1, tq, HPG * dh), lambda b, g, i: (b, i, g)),
            kv_full(2 * G), kv_full(3 * G), kv_full(4 * G), kv_full(5 * G),
            pl.BlockSpec((1, 1, 1, nb, dh), lambda b, g, i: (b, 0, g, 0, 0)),
            pl.BlockSpec((1, 1, 1, nb, dh), lambda b, g, i: (b, 1, g, 0, 0)),
            pl.BlockSpec((1, tq, LANES), lambda b, g, i: (b, i, g)),
            qtab, qtab, qtab, ktab, ktab, ktab,
            pl.BlockSpec((nb, LANES), lambda b, g, i: (0, 0)),
            pl.BlockSpec((LANES, T), lambda b, g, i: (0, 0)),
        ],
        out_specs=pl.BlockSpec((1, tq, HPG * dh), lambda b, g, i: (b, i, g)),
        scratch_shapes=[pltpu.VMEM((T, 2 * dh), BF16), pltpu.VMEM((T, 2 * dh), BF16),
                        pltpu.VMEM((T, dh), BF16), pltpu.VMEM((T, 2 * dh), BF16),
                        pltpu.VMEM((NSA_QB, T // SLC_CHUNK, Q_BLOCK, SLC_CHUNK), F32),
                        pltpu.VMEM((8, LANES), F32),
                        pltpu.VMEM((NSA_QB * HPG * Q_BLOCK, 2 * dh), F32)],
        compiler_params=_cparams(("arbitrary", "arbitrary", "arbitrary")),
        name="nsa_attention",
    )(qkv, qkv, qkv, qkv, qkv, cmp_kv, cmp_kv, gl, cos, slo, shi, cos, slo, shi,
      jnp.asarray(c2s), jnp.asarray(expand, dtype=BF16))


CONV_PAD = 32
CONV_CHUNK = 256


def _conv_kernel(a_ref, b_ref, w_ref, cb_ref, g_ref, gb_ref, o_ref, u_ref):
    T = a_ref.shape[1]
    u_ref[0:CONV_PAD, :] = jnp.zeros((CONV_PAD, LANES), F32)
    u_ref[CONV_PAD:CONV_PAD + T, :] = a_ref[0].astype(F32) * _sigmoid(b_ref[0].astype(F32))
    base = CONV_PAD - (CONV_KERNEL - 1)
    for c in range(T // CONV_CHUNK):
        t0 = c * CONV_CHUNK
        acc = jnp.zeros((CONV_CHUNK, LANES), F32)
        for k in range(CONV_KERNEL):
            acc = acc + u_ref[t0 + base + k:t0 + base + k + CONV_CHUNK, :] * w_ref[k:k + 1, :]
        acc = acc + cb_ref[...]
        mu = jnp.mean(acc, axis=-1, keepdims=True)
        d = acc - mu
        var = jnp.mean(d * d, axis=-1, keepdims=True)
        y = d * lax.rsqrt(var + LN_EPS) * g_ref[...] + gb_ref[...]
        o_ref[0, t0:t0 + CONV_CHUNK, :] = (y * _sigmoid(y)).astype(o_ref.dtype)


def conv_module(glu, conv_w, conv_b, gn_g, gn_b, B, T):
    C = glu.shape[-1] // 2
    ng = C // LANES
    assert C // CONV_GROUPS == LANES
    vec = pl.BlockSpec((1, LANES), lambda b, g: (0, g))
    return pl.pallas_call(
        _conv_kernel,
        out_shape=jax.ShapeDtypeStruct((B, T, C), BF16),
        grid=(B, ng),
        in_specs=[
            pl.BlockSpec((1, T, LANES), lambda b, g: (b, 0, g)),
            pl.BlockSpec((1, T, LANES), lambda b, g: (b, 0, ng + g)),
            pl.BlockSpec((CONV_KERNEL, LANES), lambda b, g: (0, g)),
            vec, vec, vec,
        ],
        out_specs=pl.BlockSpec((1, T, LANES), lambda b, g: (b, 0, g)),
        scratch_shapes=[pltpu.VMEM((CONV_PAD + T, LANES), F32)],
        compiler_params=_cparams(("arbitrary", "arbitrary")),
        name="conformer_conv",
    )(glu, glu, conv_w, conv_b.reshape(1, C), gn_g.reshape(1, C), gn_b.reshape(1, C))


def _split2(a):
    hi = a.astype(BF16)
    return hi, (a - hi.astype(F32)).astype(BF16)


def _router_kernel(x_ref, w_ref, bias_ref, idx_ref, wts_ref):
    xh, xl = _split2(x_ref[...])
    wh, wl = _split2(w_ref[...])
    logits = _dot(xh, wh) + (_dot(xh, wl) + _dot(xl, wh))
    aff = _sigmoid(jnp.concatenate(
        [jnp.transpose(logits[c * LANES:(c + 1) * LANES])[0:N_EXPERTS] for c in range(logits.shape[0] // LANES)],
        axis=1))
    sel = aff + bias_ref[...]
    a = [aff[e:e + 1, :] for e in range(N_EXPERTS)]
    s = [sel[e:e + 1, :] for e in range(N_EXPERTS)]
    P = EXPERTS_PER_GROUP
    grp = []
    for g in range(N_EXPERT_GROUPS):
        v = s[g * P:(g + 1) * P]
        best = None
        for i in range(P):
            for j in range(i + 1, P):
                pair = v[i] + v[j]
                best = pair if best is None else jnp.maximum(best, pair)
        grp.append(best)
    gbest = jnp.zeros_like(grp[0], dtype=jnp.int32)
    gval = grp[0]
    for g in range(1, N_EXPERT_GROUPS):
        better = grp[g] > gval
        gbest = jnp.where(better, g, gbest)
        gval = jnp.where(better, grp[g], gval)
    cs, ca = [], []
    for p in range(P):
        sv, av = s[p], a[p]
        for g in range(1, N_EXPERT_GROUPS):
            sv = jnp.where(gbest == g, s[g * P + p], sv)
            av = jnp.where(gbest == g, a[g * P + p], av)
        cs.append(sv)
        ca.append(av)
    i1 = jnp.zeros_like(gbest)
    v1, a1 = cs[0], ca[0]
    for p in range(1, P):
        better = cs[p] > v1
        i1 = jnp.where(better, p, i1)
        v1 = jnp.where(better, cs[p], v1)
        a1 = jnp.where(better, ca[p], a1)
    i2 = jnp.full_like(gbest, -1)
    v2 = jnp.full_like(v1, -jnp.inf)
    a2 = jnp.zeros_like(a1)
    for p in range(P):
        better = (i1 != p) & (cs[p] > v2)
        i2 = jnp.where(better, p, i2)
        v2 = jnp.where(better, cs[p], v2)
        a2 = jnp.where(better, ca[p], a2)
    den = a1 + a2
    idx_ref[...] = jnp.zeros(idx_ref.shape, jnp.int32)
    wts_ref[...] = jnp.zeros(wts_ref.shape, F32)
    idx_ref[0:1, :] = gbest * P + i1
    idx_ref[1:2, :] = gbest * P + i2
    wts_ref[0:1, :] = a1 / den
    wts_ref[1:2, :] = a2 / den


def router(x, router_w, router_bias, *, tm=1024):
    N, D = x.shape
    E = N_EXPERTS
    return pl.pallas_call(
        _router_kernel,
        out_shape=(jax.ShapeDtypeStruct((8, N), jnp.int32), jax.ShapeDtypeStruct((8, N), F32)),
        grid=(N // tm,),
        in_specs=[
            pl.BlockSpec((tm, D), lambda i: (i, 0)),
            pl.BlockSpec((D, LANES), lambda i: (0, 0)),
            pl.BlockSpec((E, 1), lambda i: (0, 0)),
        ],
        out_specs=(pl.BlockSpec((8, tm), lambda i: (0, i)), pl.BlockSpec((8, tm), lambda i: (0, i))),
        compiler_params=_cparams(("arbitrary",)),
        name="moe_router",
    )(x, jnp.pad(router_w, ((0, 0), (0, LANES - E))), router_bias.reshape(E, 1))


PLAN_CHUNK = 512


def _plan_kernel(idx_ref, dest_ref, cnt_ref, *, n_tok):
    E = N_EXPERTS
    sub = lax.broadcasted_iota(jnp.int32, (TOP_K * E, n_tok), 0)
    tgt = jnp.where(sub < E, idx_ref[0:1, :], idx_ref[1:2, :])
    onehot = ((sub & (E - 1)) == tgt).astype(F32)
    onehot_b = onehot.astype(BF16)
    tri = (lax.broadcasted_iota(jnp.int32, (PLAN_CHUNK, PLAN_CHUNK), 0)
           <= lax.broadcasted_iota(jnp.int32, (PLAN_CHUNK, PLAN_CHUNK), 1)).astype(F32).astype(BF16)
    carry = jnp.zeros((TOP_K * E, 1), F32)
    parts = []
    for c in range(n_tok // PLAN_CHUNK):
        pre = _dot(onehot_b[:, c * PLAN_CHUNK:(c + 1) * PLAN_CHUNK], tri) + carry
        parts.append(pre)
        carry = pre[:, PLAN_CHUNK - 1:PLAN_CHUNK]
    excl = jnp.concatenate(parts, axis=1) - onehot
    cnt0 = carry[0:E]
    tot = cnt0 + carry[E:2 * E]
    lower = (lax.broadcasted_iota(jnp.int32, (E, E), 1)
             < lax.broadcasted_iota(jnp.int32, (E, E), 0)).astype(F32)
    offs = jnp.dot(lower, jnp.broadcast_to(tot, (E, LANES)), preferred_element_type=F32,
                   precision=lax.Precision.HIGHEST)[:, 0:1]
    base = jnp.concatenate([offs, offs + cnt0], axis=0)
    val = onehot * (base + excl)
    dest_ref[...] = jnp.zeros(dest_ref.shape, jnp.int32)
    dest_ref[0:1, :] = jnp.sum(val[0:E], axis=0, keepdims=True).astype(jnp.int32)
    dest_ref[1:2, :] = jnp.sum(val[E:2 * E], axis=0, keepdims=True).astype(jnp.int32)
    cnt_ref[...] = jnp.broadcast_to(tot, (E, LANES)).astype(jnp.int32)


def moe_plan(idx):
    n_tok = idx.shape[1]
    assert n_tok % PLAN_CHUNK == 0 and TOP_K == 2
    return pl.pallas_call(
        functools.partial(_plan_kernel, n_tok=n_tok),
        out_shape=(jax.ShapeDtypeStruct((8, n_tok), jnp.int32),
                   jax.ShapeDtypeStruct((N_EXPERTS, LANES), jnp.int32)),
        compiler_params=_cparams(None),
        name="moe_plan",
    )(idx)


def _scatter_kernel(dest_ref, x_ref, xs_hbm, sem, *, tm, n_tok):
    base = pl.program_id(0) * tm

    def copies(start):
        def make(r, k, d):
            cp = pltpu.make_async_copy(x_ref.at[pl.ds(r, 1)], xs_hbm.at[pl.ds(d, 1)], sem)
            if start:
                cp.start(priority=k)
            else:
                cp.wait()
        _row_copies(dest_ref, n_tok, base, tm, make)

    copies(True)
    copies(False)


def moe_scatter(x, dest, *, tm=256):
    N, D = x.shape
    grid_spec = pltpu.PrefetchScalarGridSpec(
        num_scalar_prefetch=1,
        grid=(N // tm,),
        in_specs=[pl.BlockSpec((tm, D), lambda i, d: (i, 0))],
        out_specs=pl.BlockSpec(memory_space=pl.ANY),
        scratch_shapes=[pltpu.SemaphoreType.DMA(())],
    )
    return pl.pallas_call(
        functools.partial(_scatter_kernel, tm=tm, n_tok=N),
        out_shape=jax.ShapeDtypeStruct((TOP_K * N, D), x.dtype),
        grid_spec=grid_spec,
        compiler_params=_cparams(("arbitrary",)),
        name="moe_scatter",
    )(dest, x)


def _moe_kernel(ti_ref, te_ref, lo_ref, hi_ref, first_ref, head_ref, slot_ref, nxt_ref,
                xs_ref, wg_hbm, wu_hbm, wd_hbm, y_ref,
                wg32_ref, wu32_ref, wd32_ref, wgb_ref, wub_ref, wdb_ref, sem, *, layer):
    i = pl.program_id(0)

    def weight_copies(e, slot):
        return [pltpu.make_async_copy(src.at[layer, e], dst.at[slot], sem.at[slot, n])
                for n, (src, dst) in enumerate(((wg_hbm, wg32_ref), (wu_hbm, wu32_ref), (wd_hbm, wd32_ref)))]

    @pl.when(i == 0)
    def _():
        for cp in weight_copies(te_ref[0], 0):
            cp.start()

    @pl.when(head_ref[i] == 1)
    def _():
        slot = slot_ref[i]
        for cp in weight_copies(te_ref[i], slot):
            cp.wait()
        wgb_ref[...] = wg32_ref[slot].astype(BF16)
        wub_ref[...] = wu32_ref[slot].astype(BF16)
        wdb_ref[...] = wd32_ref[slot].astype(BF16)

        @pl.when(nxt_ref[i] >= 0)
        def _():
            for cp in weight_copies(nxt_ref[i], 1 - slot):
                cp.start()

    lo, hi = lo_ref[i], hi_ref[i]
    tm = xs_ref.shape[0]
    half = tm // 2

    def ffn(r0, n):
        x = xs_ref[r0:r0 + n, :].astype(BF16)
        g = _dot(x, wgb_ref[...])
        u = _dot(x, wub_ref[...])
        rows = r0 + lax.broadcasted_iota(jnp.int32, (n, 1), 0)
        h = jnp.where((rows >= lo) & (rows < hi), (g * _sigmoid(g)) * u, 0.0)
        y = _dot(h.astype(BF16), wdb_ref[...])

        @pl.when(first_ref[i] == 1)
        def _():
            y_ref[r0:r0 + n, :] = y

        @pl.when(first_ref[i] == 0)
        def _():
            y_ref[r0:r0 + n, :] += y

    lower, upper = lo < half, hi > half

    @pl.when(lower & upper)
    def _():
        ffn(0, tm)

    @pl.when(lower & jnp.logical_not(upper) & (hi > lo))
    def _():
        ffn(0, half)

        @pl.when(first_ref[i] == 1)
        def _():
            y_ref[half:tm, :] = jnp.zeros((tm - half, y_ref.shape[1]), y_ref.dtype)

    @pl.when(upper & jnp.logical_not(lower))
    def _():
        ffn(half, tm - half)


def moe_experts(xs, items, w_gate, w_up, w_down, layer, *, tm=MOE_TM):
    P, D = xs.shape
    Fh = w_gate.shape[-1]
    n_items = items[0].shape[0]
    xmap = lambda i, ti, *_: (ti[i], 0)
    hbm = pl.BlockSpec(memory_space=pl.ANY)
    grid_spec = pltpu.PrefetchScalarGridSpec(
        num_scalar_prefetch=len(items),
        grid=(n_items,),
        in_specs=[pl.BlockSpec((tm, D), xmap), hbm, hbm, hbm],
        out_specs=pl.BlockSpec((tm, D), xmap),
        scratch_shapes=[pltpu.VMEM((2, D, Fh), F32), pltpu.VMEM((2, D, Fh), F32), pltpu.VMEM((2, Fh, D), F32),
                        pltpu.VMEM((D, Fh), BF16), pltpu.VMEM((D, Fh), BF16), pltpu.VMEM((Fh, D), BF16),
                        pltpu.SemaphoreType.DMA((2, 3))],
    )
    return pl.pallas_call(
        functools.partial(_moe_kernel, layer=layer),
        out_shape=jax.ShapeDtypeStruct((P, D), F32),
        grid_spec=grid_spec,
        compiler_params=_cparams(("arbitrary",)),
        name="moe_experts",
    )(*items, xs, w_gate, w_up, w_down)


def moe_items(tot, n_rows, tm=MOE_TM):
    E = N_EXPERTS
    n_max = n_rows // tm + E
    ar = jnp.arange(E, dtype=jnp.int32)
    ends = jnp.cumsum(tot)
    offs = ends - tot
    first_tile = offs // tm
    n_e = jnp.where(tot > 0, (ends - 1) // tm - first_tile + 1, 0)
    s_end = jnp.cumsum(n_e)
    s_beg = s_end - n_e
    n_items = s_end[-1]
    i = jnp.arange(n_max, dtype=jnp.int32)
    ic = jnp.minimum(i, n_items - 1)
    e_i = jnp.sum((ic[:, None] >= s_end[None, :]).astype(jnp.int32), axis=1)
    pick = (e_i[:, None] == ar[None, :]).astype(jnp.int32)
    at = lambda v: jnp.sum(pick * v[None, :], axis=1)
    tile = at(first_tile) + ic - at(s_beg)
    live = i < n_items
    lo = jnp.where(live, jnp.maximum(at(offs), tile * tm) - tile * tm, 0)
    hi = jnp.where(live, jnp.minimum(at(ends), tile * tm + tm) - tile * tm, 0)
    prev_tile = jnp.concatenate([jnp.full((1,), -1, jnp.int32), tile[:-1]])
    first = live & (tile != prev_tile)
    head = live & (ic == at(s_beg))
    used = (tot > 0).astype(jnp.int32)
    slot = at(jnp.cumsum(used) - used) & 1
    later = jnp.where((ar[None, :] > ar[:, None]) & (tot[None, :] > 0), ar[None, :], E)
    nxt_e = jnp.min(later, axis=1)
    nxt = at(jnp.where(nxt_e < E, nxt_e, -1))
    return tuple(v.astype(jnp.int32) for v in (tile, e_i, lo, hi, first, head, slot, nxt))


def _rope_tables(pos):
    inv = ROPE_THETA ** (-jnp.arange(0, ROPE_DIM, 2, dtype=F32) / ROPE_DIM)
    ang = pos.astype(F32)[:, None] * inv[None, :]
    cos, sin = jnp.cos(ang), jnp.sin(ang)
    n = pos.shape[0]
    rest = HEAD_DIM - ROPE_DIM
    cos_t = jnp.concatenate([cos, cos, jnp.ones((n, rest), F32)], axis=1)
    sin_lo = jnp.concatenate([-sin, jnp.zeros((n, HEAD_DIM - ROPE_HALF), F32)], axis=1)
    sin_hi = jnp.concatenate([jnp.zeros((n, ROPE_HALF), F32), sin, jnp.zeros((n, rest), F32)], axis=1)
    return cos_t, sin_lo, sin_hi


def _even_mixer(xb, B, T, w_in_all, j, pos, w1, w2, conv_w, conv_b, gn_g, gn_b):
    N, D = xb.shape
    G, HPG, dh = NSA_KV_HEADS, NSA_HPG, HEAD_DIM
    nsa_w = G * HPG * dh
    kv_cols = 3 * 2 * G * dh
    n_gate = 3 * G * HPG
    conv_c = (w_in_all.shape[2] - nsa_w - kv_cols - n_gate) // 2
    qkv = linear(xb, w_in_all, nsa_w + kv_cols, F32, layer=j, tm=512, tn=(nsa_w + kv_cols) // 2, name="even_qkv")
    w_gl = w_in_all[j, :, nsa_w + kv_cols:nsa_w + kv_cols + n_gate].reshape(D, G, 3 * HPG)
    w_gl = jnp.pad(w_gl, ((0, 0), (0, 0), (0, LANES - 3 * HPG))).reshape(D, G * LANES)
    gl = linear(xb, w_gl, G * LANES, F32, tn=G * LANES, name="even_gates")
    glu = linear(xb, w_in_all[j, :, nsa_w + kv_cols + n_gate:], 2 * conv_c, BF16, tn=1024, name="even_glu")

    qkv3 = qkv.reshape(B, T, nsa_w + kv_cols)
    n_cmp = (T - CMP_LEN) // CMP_STRIDE + 1
    nb = T // CMP_STRIDE
    cmp_end = jnp.arange(nb) * CMP_STRIDE + (CMP_LEN - 1)
    cmp_kv = compress(qkv3, pos, w1, w2, _rope_tables(cmp_end), B, T, G * HPG)
    o_nsa = nsa_attention(qkv3, cmp_kv, gl.reshape(B, T, G * LANES), _rope_tables(jnp.arange(T)), B, T)
    u = conv_module(glu.reshape(B, T, 2 * conv_c), conv_w, conv_b, gn_g, gn_b, B, T)
    return o_nsa.reshape(N, nsa_w), u.reshape(N, conv_c)


def _fox_mixer(xb, B, T, w_in_all, j, f_bias, q_gain, k_gain):
    N, D = xb.shape
    H = D // HEAD_DIM
    proj = linear(xb, w_in_all, 4 * D, BF16, layer=j, tn=1024, name="fox_qkvg")
    w_f = jnp.pad(w_in_all[j, :, 4 * D:], ((0, 0), (0, LANES - H)))
    fl = linear(xb, w_f, LANES, F32, tn=LANES, name="fox_forget")
    c = fox_decay(fl, jnp.pad(f_bias, (0, LANES - H)), B, T)[:, :H].reshape(B, T, H)
    return fox_attention(proj.reshape(B, T, 4 * D), c, q_gain, k_gain, B, T, H).reshape(N, D)


def _moe(xf, router_w, router_bias, w_gate, w_up, w_down, layer):
    N, D = xf.shape
    idx, wts = router(xf, router_w, router_bias)
    dest8, cnt = moe_plan(idx)
    dest = dest8[:TOP_K].reshape(-1)
    xs = moe_scatter(xf, dest)
    y = moe_experts(xs, moe_items(cnt[:, 0], TOP_K * N), w_gate, w_up, w_down, layer)
    return y, dest, wts[:TOP_K].T


def kernel(x, even_w_in, even_w_out, nsa_cmp_pos, nsa_cmp_w1, nsa_cmp_w2, conv_w, conv_b, conv_gn_g, conv_gn_b, fox_w_in, fox_f_bias, fox_q_gain, fox_k_gain, fox_w_out, ln_mix_g, ln_mix_b, ln_ffn_g, ln_ffn_b, router_w, router_bias, exp_w_gate, exp_w_up, exp_w_down):
    B, T, D = x.shape
    depth = ln_mix_g.shape[0]
    alpha = (2.0 * depth) ** 0.25
    N = B * T
    xf = x.reshape(N, D)
    xb = xf
    for layer in range(depth):
        j = layer // 2
        if layer % 2 == 0:
            a1, a2 = _even_mixer(xb, B, T, even_w_in, j, nsa_cmp_pos[j], nsa_cmp_w1[j], nsa_cmp_w2[j],
                                 conv_w[j], conv_b[j], conv_gn_g[j], conv_gn_b[j])
            xf, xb = outproj_ln(a1, a2, 0, even_w_out, j, xf, ln_mix_g[layer], ln_mix_b[layer], alpha)
        else:
            a = _fox_mixer(xb, B, T, fox_w_in, j, fox_f_bias[j], fox_q_gain[j], fox_k_gain[j])
            xf, xb = outproj_ln(a, a, 1, fox_w_out, j, xf, ln_mix_g[layer], ln_mix_b[layer], alpha)
        y, dest, wts = _moe(xf, router_w, router_bias, exp_w_gate, exp_w_up, exp_w_down, layer)
        xf, xb = combine_ln(xf, y, dest, wts, ln_ffn_g[layer], ln_ffn_b[layer], alpha)
    return xf.reshape(B, T, D)
```

```python
import functools

import numpy as np
import jax
import jax.numpy as jnp
from jax import lax
from jax.experimental import pallas as pl
from jax.experimental.pallas import tpu as pltpu

F32 = jnp.float32
BF16 = jnp.bfloat16

HEAD_DIM = 128
ROPE_THETA = 500000.0
ROPE_DIM = HEAD_DIM // 4
ROPE_HALF = ROPE_DIM // 2
Q_BLOCK = 128

NSA_KV_HEADS = 2
NSA_HPG = 4
CMP_LEN = 32
CMP_STRIDE = 16
SLC_LEN = 64
N_SLC = 8
WIN = 512
CONV_KERNEL = 31
CONV_GROUPS = 8

N_EXPERTS = 16
N_EXPERT_GROUPS = 4
EXPERTS_PER_GROUP = 4
TOP_K = 2

LN_EPS = 1e-5
NEG = -1e30
BIG = 1e9
TINY = 1e-30

LANES = 128
VMEM_LIMIT = 56 * 1024 * 1024

MOE_TM = 256


def _cparams(sem, vmem=VMEM_LIMIT):
    return pltpu.CompilerParams(dimension_semantics=sem, vmem_limit_bytes=vmem)


def _dot(a, b):
    return jnp.dot(a, b, preferred_element_type=F32)


def _dot_nt(a, b):
    return lax.dot_general(a, b, (((1,), (1,)), ((), ())), preferred_element_type=F32)


def _sigmoid(x):
    return 1.0 / (1.0 + jnp.exp(-x))


def _linear_kernel(x_ref, w_ref, o_ref, wb_ref):
    @pl.when(pl.program_id(1) == 0)
    def _():
        wb_ref[...] = w_ref[...].astype(BF16)

    o_ref[...] = _dot(x_ref[...].astype(BF16), wb_ref[...]).astype(o_ref.dtype)


def linear(x, w, n_cols, out_dtype, *, layer=None, tm=1024, tn=512, name="linear"):
    M, K = x.shape
    tm = min(tm, M)
    tn = min(tn, n_cols)
    assert M % tm == 0 and n_cols % tn == 0
    row_block = 0
    if layer is not None:
        w = w.reshape(-1, w.shape[-1])
        row_block = layer
    w_spec = pl.BlockSpec((K, tn), lambda j, i: (row_block, j))
    return pl.pallas_call(
        _linear_kernel,
        out_shape=jax.ShapeDtypeStruct((M, n_cols), out_dtype),
        grid=(n_cols // tn, M // tm),
        in_specs=[
            pl.BlockSpec((tm, K), lambda j, i: (i, 0)),
            w_spec,
        ],
        out_specs=pl.BlockSpec((tm, tn), lambda j, i: (i, j)),
        scratch_shapes=[pltpu.VMEM((K, tn), BF16)],
        compiler_params=_cparams(("arbitrary", "arbitrary")),
        name=name,
    )(x, w)


def _layer_norm_rows(y, g, b):
    mu = jnp.mean(y, axis=-1, keepdims=True)
    d = y - mu
    var = jnp.mean(d * d, axis=-1, keepdims=True)
    return d * lax.rsqrt(var + LN_EPS) * g + b


def _outproj_ln_kernel(a1_ref, a2_ref, w_ref, x_ref, g_ref, b_ref, xo_ref, xb_ref, wb_ref, *, alpha):
    @pl.when(pl.program_id(0) == 0)
    def _():
        wb_ref[...] = w_ref[...].astype(BF16)

    half = a1_ref.shape[1]
    h = _dot(a1_ref[...].astype(BF16), wb_ref[0:half, :])
    h = h + _dot(a2_ref[...].astype(BF16), wb_ref[half:2 * half, :])
    out = _layer_norm_rows(alpha * x_ref[...] + h, g_ref[...], b_ref[...])
    xo_ref[...] = out
    xb_ref[...] = out.astype(BF16)


def outproj_ln(a1, a2, a2_col_block, w, layer, x, g, b, alpha, *, tm=512):
    M, D = x.shape
    half = D // 2
    return pl.pallas_call(
        functools.partial(_outproj_ln_kernel, alpha=alpha),
        out_shape=(jax.ShapeDtypeStruct((M, D), F32), jax.ShapeDtypeStruct((M, D), BF16)),
        grid=(M // tm,),
        in_specs=[
            pl.BlockSpec((tm, half), lambda i: (i, 0)),
            pl.BlockSpec((tm, half), lambda i: (i, a2_col_block)),
            pl.BlockSpec((None, D, D), lambda i: (layer, 0, 0), pipeline_mode=pl.Buffered(1)),
            pl.BlockSpec((tm, D), lambda i: (i, 0)),
            pl.BlockSpec((1, D), lambda i: (0, 0)),
            pl.BlockSpec((1, D), lambda i: (0, 0)),
        ],
        out_specs=(pl.BlockSpec((tm, D), lambda i: (i, 0)), pl.BlockSpec((tm, D), lambda i: (i, 0))),
        scratch_shapes=[pltpu.VMEM((D, D), BF16)],
        compiler_params=_cparams(("arbitrary",)),
        name="outproj_ln",
    )(a1, a2, w, x, g.reshape(1, D), b.reshape(1, D))


ROW_DMA_UNROLL = 8


def _row_copies(dest_ref, n_tok, base, tm, make):
    def body(g, carry):
        r0 = pl.multiple_of(g * ROW_DMA_UNROLL, ROW_DMA_UNROLL)
        for u in range(ROW_DMA_UNROLL):
            for k in range(TOP_K):
                make(r0 + u, k, dest_ref[k * n_tok + base + r0 + u])
        return carry

    lax.fori_loop(0, tm // ROW_DMA_UNROLL, body, 0)


def _combine_ln_kernel(dest_ref, x_ref, w_ref, g_ref, b_ref, y_hbm, xo_ref, xb_ref, buf_ref, sem, *,
                       alpha, tm, n_tok):
    i = pl.program_id(0)
    n_steps = pl.num_programs(0)

    def copies(step, slot, start):
        def make(r, k, d):
            cp = pltpu.make_async_copy(y_hbm.at[pl.ds(d, 1)], buf_ref.at[slot, k, pl.ds(r, 1)], sem.at[slot])
            if start:
                cp.start(priority=k)
            else:
                cp.wait()
        _row_copies(dest_ref, n_tok, step * tm, tm, make)

    @pl.when(i == 0)
    def _():
        copies(0, 0, True)

    @pl.when(i + 1 < n_steps)
    def _():
        copies(i + 1, (i + 1) % 2, True)

    slot = i % 2
    copies(i, slot, False)
    f = w_ref[:, 0:1] * buf_ref[slot, 0] + w_ref[:, 1:2] * buf_ref[slot, 1]
    out = _layer_norm_rows(alpha * x_ref[...] + f, g_ref[...], b_ref[...])
    xo_ref[...] = out
    xb_ref[...] = out.astype(BF16)


def combine_ln(x, y, dest, wts, g, b, alpha, *, tm=256):
    N, D = x.shape
    row = lambda i, d: (i, 0)
    grid_spec = pltpu.PrefetchScalarGridSpec(
        num_scalar_prefetch=1,
        grid=(N // tm,),
        in_specs=[
            pl.BlockSpec((tm, D), row),
            pl.BlockSpec((tm, TOP_K), row),
            pl.BlockSpec((1, D), lambda i, d: (0, 0)),
            pl.BlockSpec((1, D), lambda i, d: (0, 0)),
            pl.BlockSpec(memory_space=pl.ANY),
        ],
        out_specs=(pl.BlockSpec((tm, D), row), pl.BlockSpec((tm, D), row)),
        scratch_shapes=[pltpu.VMEM((2, TOP_K, tm, D), F32), pltpu.SemaphoreType.DMA((2,))],
    )
    return pl.pallas_call(
        functools.partial(_combine_ln_kernel, alpha=alpha, tm=tm, n_tok=N),
        out_shape=(jax.ShapeDtypeStruct((N, D), F32), jax.ShapeDtypeStruct((N, D), BF16)),
        grid_spec=grid_spec,
        compiler_params=_cparams(("arbitrary",)),
        name="moe_combine_ln",
    )(dest, x, wts, g.reshape(1, D), b.reshape(1, D), y)


def _rms(x, gain):
    return x * lax.rsqrt(jnp.mean(x * x, axis=-1, keepdims=True) + LN_EPS) * gain


CUM_CHUNK = 256


def _decay_kernel(fl_ref, bias_ref, c_ref):
    T = fl_ref.shape[0]
    tri = (lax.broadcasted_iota(jnp.int32, (CUM_CHUNK, CUM_CHUNK), 0)
           >= lax.broadcasted_iota(jnp.int32, (CUM_CHUNK, CUM_CHUNK), 1)).astype(F32)
    carry = jnp.zeros((1, LANES), F32)
    for c in range(T // CUM_CHUNK):
        z = fl_ref[c * CUM_CHUNK:(c + 1) * CUM_CHUNK, :] + bias_ref[...]
        log_f = jnp.minimum(z, 0.0) - jnp.log1p(jnp.exp(-jnp.abs(z)))
        cs = jnp.dot(tri, log_f, preferred_element_type=F32, precision=lax.Precision.HIGHEST) + carry
        c_ref[c * CUM_CHUNK:(c + 1) * CUM_CHUNK, :] = cs
        carry = cs[CUM_CHUNK - 1:CUM_CHUNK, :]


def fox_decay(fl, bias, B, T):
    assert T % CUM_CHUNK == 0
    return pl.pallas_call(
        _decay_kernel,
        out_shape=jax.ShapeDtypeStruct(fl.shape, F32),
        grid=(B,),
        in_specs=[pl.BlockSpec((T, LANES), lambda b: (b, 0)), pl.BlockSpec((1, LANES), lambda b: (0, 0))],
        out_specs=pl.BlockSpec((T, LANES), lambda b: (b, 0)),
        compiler_params=_cparams(("arbitrary",)),
        name="fox_decay",
    )(fl, bias.reshape(1, LANES))


LOG2E = 1.4426950408889634
FOX_FAST_BOUND = 38.0
FOX_AUX = 3


def _split3(a):
    hi = a.astype(BF16).astype(F32)
    r = a - hi
    mid = r.astype(BF16).astype(F32)
    return hi, mid, (r - mid).astype(BF16).astype(F32)


def _aux_lanes(pieces, ones_first, n):
    lane = lax.broadcasted_iota(jnp.int32, (n, LANES), 1)
    p0 = FOX_AUX if ones_first else 0
    o0 = 0 if ones_first else FOX_AUX
    out = jnp.where((lane >= o0) & (lane < o0 + FOX_AUX), 1.0, 0.0)
    for i, piece in enumerate(pieces):
        out = out + jnp.where(lane == p0 + i, piece, 0.0)
    return out


def _fox_kernel(fast_ref, q_ref, k_ref, v_ref, og_ref, ckrow_ref, ckcol_ref, cq_ref, bound_ref, qg_ref, kg_ref,
                o_ref, kx_ref, vx_ref, *, tq, tk, hb):
    qi = pl.program_id(2)
    dh = HEAD_DIM
    T = k_ref.shape[1]

    def piece_by_lane(a, lane):
        hi, mid, lo = _split3(a)
        in_group = lambda g: ((lane >= g * hb) & (lane < (g + 1) * hb)) | (
            (lane >= (FOX_AUX + g) * hb) & (lane < (FOX_AUX + g + 1) * hb))
        return jnp.where(in_group(0), hi, jnp.where(in_group(1), mid, lo))

    @pl.when(qi == 0)
    def _():
        lane_k = lax.broadcasted_iota(jnp.int32, (T, LANES), 1)
        k_aux = jnp.where(lane_k < FOX_AUX * hb, piece_by_lane(-LOG2E * ckcol_ref[0, 0], lane_k),
                          jnp.where(lane_k < 2 * FOX_AUX * hb, 1.0, 0.0)).astype(BF16)
        for h in range(hb):
            kx_ref[h, :, 0:dh] = _rms(k_ref[0, :, h * dh:(h + 1) * dh].astype(F32), kg_ref[...]).astype(BF16)
            kx_ref[h, :, dh:2 * dh] = k_aux
            vx_ref[h, :, 0:dh] = v_ref[0, :, h * dh:(h + 1) * dh].astype(BF16)
            vx_ref[h, :, dh:2 * dh] = jnp.ones((T, dh), BF16)

    row = qi * tq + lax.broadcasted_iota(jnp.int32, (tq, tk), 0)
    lane = lax.broadcasted_iota(jnp.int32, (tq, tk), 1)
    n_full = (qi * tq) // tk

    def finish(accs):
        for h in range(hb):
            o = accs[h][:, 0:dh] / jnp.maximum(accs[h][:, dh:2 * dh], TINY)
            gate = _sigmoid(og_ref[0, :, h * dh:(h + 1) * dh].astype(F32))
            o_ref[0, :, h * dh:(h + 1) * dh] = (o * gate).astype(o_ref.dtype)

    def q_normed(h):
        return _rms(q_ref[0, :, h * dh:(h + 1) * dh].astype(F32), qg_ref[...]) * (dh ** -0.5)

    @pl.when(fast_ref[0] == 1)
    def _():
        lane_q = lax.broadcasted_iota(jnp.int32, (tq, LANES), 1)
        r = piece_by_lane(LOG2E * (cq_ref[0, 0] - bound_ref[0:1, 0:1]), lane_q)
        qx = []
        for h in range(hb):
            mine = (lane_q & (hb - 1)) == h
            q_aux = jnp.where(mine & (lane_q < FOX_AUX * hb), 1.0,
                              jnp.where(mine & (lane_q < 2 * FOX_AUX * hb), r, 0.0))
            qx.append(jnp.concatenate([(q_normed(h) * LOG2E).astype(BF16), q_aux.astype(BF16)], axis=1))

        def step(j, accs, masked):
            start = pl.multiple_of(j * tk, tk)
            out = []
            for h in range(hb):
                s = _dot_nt(qx[h], kx_ref[h, pl.ds(start, tk), :])
                if masked:
                    s = jnp.where(j * tk + lane <= row, s, NEG)
                out.append(accs[h] + _dot(jnp.exp2(s).astype(BF16), vx_ref[h, pl.ds(start, tk), :]))
            return tuple(out)

        def diagonal(accs):
            hq = tq // 2
            start = pl.multiple_of(n_full * tk, tk)
            tri = (lax.broadcasted_iota(jnp.int32, (hq, hq), 1) <= lax.broadcasted_iota(jnp.int32, (hq, hq), 0))
            out = []
            for h in range(hb):
                k_lo, k_hi = kx_ref[h, pl.ds(start, hq), :], kx_ref[h, pl.ds(start + hq, hq), :]
                v_lo, v_all = vx_ref[h, pl.ds(start, hq), :], vx_ref[h, pl.ds(start, tk), :]
                p_top = jnp.exp2(jnp.where(tri, _dot_nt(qx[h][0:hq], k_lo), NEG)).astype(BF16)
                p_bot = jnp.concatenate(
                    [jnp.exp2(_dot_nt(qx[h][hq:tq], k_lo)).astype(BF16),
                     jnp.exp2(jnp.where(tri, _dot_nt(qx[h][hq:tq], k_hi), NEG)).astype(BF16)], axis=1)
                out.append(accs[h] + jnp.concatenate([_dot(p_top, v_lo), _dot(p_bot, v_all)], axis=0))
            return out

        accs = tuple(jnp.zeros((tq, 2 * dh), F32) for _ in range(hb))
        accs = lax.fori_loop(0, n_full, lambda j, a: step(j, a, False), accs)
        finish(diagonal(accs) if tq == tk else step(n_full, accs, True))

    @pl.when(fast_ref[0] == 0)
    def _():
        qs = [q_normed(h).astype(BF16) for h in range(hb)]

        def step(j, carry, masked):
            start = pl.multiple_of(j * tk, tk)
            out = []
            for h in range(hb):
                m, acc = carry[h]
                s = _dot_nt(qs[h], kx_ref[h, pl.ds(start, tk), 0:dh]) - ckrow_ref[0, h, pl.ds(j, 1), :]
                if masked:
                    s = jnp.where(j * tk + lane <= row, s, NEG)
                m_new = jnp.maximum(m, jnp.max(s, axis=-1, keepdims=True))
                p = jnp.exp(s - m_new).astype(BF16)
                acc = jnp.exp(m - m_new) * acc + _dot(p, vx_ref[h, pl.ds(start, tk), :])
                out.append((m_new, acc))
            return tuple(out)

        init = tuple((jnp.full((tq, 1), NEG, F32), jnp.zeros((tq, 2 * dh), F32)) for _ in range(hb))
        carry = lax.fori_loop(0, n_full, lambda j, c: step(j, c, False), init)
        finish([c[1] for c in step(n_full, carry, True)])


def fox_attention(proj, c, q_gain, k_gain, B, T, H, *, tq=512, tk=512, hb=4):
    nq = T // tq
    nk = T // tk
    dh = HEAD_DIM
    hg = H // hb
    c_row = c.transpose(0, 2, 1).reshape(B, H, nk, tk)
    assert hb & (hb - 1) == 0 and 2 * FOX_AUX * hb <= LANES
    c_col = c.reshape(B, T, hg, hb).transpose(0, 2, 1, 3)
    c_col = jnp.concatenate([jnp.tile(c_col, (1, 1, 1, 2 * FOX_AUX)),
                             jnp.zeros((B, hg, T, LANES - 2 * FOX_AUX * hb), F32)], axis=-1)
    bound = jnp.max(jnp.abs(q_gain)) * jnp.max(jnp.abs(k_gain)) * (dh ** 0.5) * 1.01
    fast = (bound <= FOX_FAST_BOUND).astype(jnp.int32).reshape(1)
    full = lambda off: pl.BlockSpec((1, T, hb * dh), lambda b, h, i, f: (b, 0, off + h))
    tile = lambda off: pl.BlockSpec((1, tq, hb * dh), lambda b, h, i, f: (b, i, off + h))
    vec = pl.BlockSpec((1, dh), lambda b, h, i, f: (0, 0))
    grid_spec = pltpu.PrefetchScalarGridSpec(
        num_scalar_prefetch=1,
        grid=(B, hg, nq),
        in_specs=[
            tile(0), full(hg), full(2 * hg), tile(3 * hg),
            pl.BlockSpec((1, hb, nk, tk), lambda b, h, i, f: (b, h, 0, 0)),
            pl.BlockSpec((1, 1, T, LANES), lambda b, h, i, f: (b, h, 0, 0)),
            pl.BlockSpec((1, 1, tq, LANES), lambda b, h, i, f: (b, h, i, 0)),
            vec, vec, vec,
        ],
        out_specs=pl.BlockSpec((1, tq, hb * dh), lambda b, h, i, f: (b, i, h)),
        scratch_shapes=[pltpu.VMEM((hb, T, 2 * dh), BF16), pltpu.VMEM((hb, T, 2 * dh), BF16)],
    )
    return pl.pallas_call(
        functools.partial(_fox_kernel, tq=tq, tk=tk, hb=hb),
        out_shape=jax.ShapeDtypeStruct((B, T, H * dh), BF16),
        grid_spec=grid_spec,
        compiler_params=_cparams(("arbitrary", "arbitrary", "arbitrary")),
        name="fox_attention",
    )(fast, proj, proj, proj, proj, c_row, c_col, c_col, jnp.full((1, dh), bound, F32),
      q_gain.reshape(1, dh), k_gain.reshape(1, dh))


def _rope(x, cos, sin_lo, sin_hi):
    return (x * cos + pltpu.roll(x, LANES - ROPE_HALF, 1) * sin_lo
            + pltpu.roll(x, ROPE_HALF, 1) * sin_hi)


def _gelu_tanh(x):
    return 0.5 * x * (1.0 + jnp.tanh(0.7978845608028654 * (x + 0.044715 * (x * x * x))))


def _compress_kernel(raw_ref, pos_ref, w1_ref, w2_ref, cos_ref, slo_ref, shi_ref, o_ref, *, n_cmp):
    nb = o_ref.shape[-2]
    half = CMP_LEN // 2
    acc_a = jnp.zeros((nb, HEAD_DIM), F32)
    acc_b = jnp.zeros((nb, HEAD_DIM), F32)
    for l in range(half):
        rl = raw_ref[0, pl.ds(l, nb, stride=CMP_STRIDE), :]
        wa = w1_ref[0, l * HEAD_DIM:(l + 1) * HEAD_DIM, :].astype(BF16)
        wb = w1_ref[0, (half + l) * HEAD_DIM:(half + l + 1) * HEAD_DIM, :].astype(BF16)
        acc_a = acc_a + _dot((rl + pos_ref[0, l:l + 1, :]).astype(BF16), wa)
        acc_b = acc_b + _dot((rl + pos_ref[0, half + l:half + l + 1, :]).astype(BF16), wb)
    pre = acc_a + pltpu.roll(acc_b, nb - 1, 0)
    out = _dot(_gelu_tanh(pre).astype(BF16), w2_ref[0].astype(BF16))
    roped = _rope(out, cos_ref[...], slo_ref[...], shi_ref[...])
    out = jnp.where(pl.program_id(1) == 0, roped, out)
    rows = lax.broadcasted_iota(jnp.int32, out.shape, 0)
    o_ref[0, 0, 0] = jnp.where(rows < n_cmp, out, 0.0)


def compress(qkv, pos, w1, w2, tabs, B, T, kv_block0):
    G = NSA_KV_HEADS
    nb = T // CMP_STRIDE
    n_cmp = (T - CMP_LEN) // CMP_STRIDE + 1
    cos, slo, shi = tabs
    tab = pl.BlockSpec((nb, HEAD_DIM), lambda b, kv, g: (0, 0))
    return pl.pallas_call(
        functools.partial(_compress_kernel, n_cmp=n_cmp),
        out_shape=jax.ShapeDtypeStruct((B, 2, G, nb, HEAD_DIM), F32),
        grid=(B, 2, G),
        in_specs=[
            pl.BlockSpec((1, T, HEAD_DIM), lambda b, kv, g: (b, 0, kv_block0 + kv * G + g)),
            pl.BlockSpec((1, CMP_LEN, HEAD_DIM), lambda b, kv, g: (kv, 0, 0)),
            pl.BlockSpec((1, CMP_LEN * HEAD_DIM, HEAD_DIM), lambda b, kv, g: (kv, 0, 0)),
            pl.BlockSpec((1, HEAD_DIM, HEAD_DIM), lambda b, kv, g: (kv, 0, 0)),
            tab, tab, tab,
        ],
        out_specs=pl.BlockSpec((1, 1, 1, nb, HEAD_DIM), lambda b, kv, g: (b, kv, g, 0, 0)),
        compiler_params=_cparams(("arbitrary", "arbitrary", "arbitrary")),
        name="nsa_compress",
    )(qkv, pos, w1, w2, cos, slo, shi)


SLC_CHUNK = 512
WIN_SPAN = WIN + Q_BLOCK
NSA_QB = 4


def _select_blocks(imp, qi, n_slc):
    QB = Q_BLOCK
    n_rows = -(-n_slc // 8) * 8
    imp_t = jnp.transpose(imp)[0:n_rows]
    blk = lax.broadcasted_iota(jnp.int32, (n_rows, QB), 0)
    t = qi * QB + lax.broadcasted_iota(jnp.int32, (n_rows, QB), 1)
    cur = jnp.right_shift(t, 6)
    forced = (blk == 0) | (blk == cur) | (blk == cur - 1)
    score = jnp.where(forced, BIG, jnp.where(blk * SLC_LEN <= t, imp_t, -BIG))
    score = jnp.where(blk < n_slc, score, -2.0 * BIG)
    cnt = jnp.zeros((n_rows, QB), F32)
    for jp in range(n_slc):
        r = score[jp:jp + 1, :]
        cnt = cnt + ((r > score) | ((r == score) & (jp < blk))).astype(F32)
    sel_t = ((cnt < float(min(N_SLC, n_slc))) & (blk < n_slc)).astype(F32)
    if n_rows < LANES:
        sel_t = jnp.concatenate([sel_t, jnp.zeros((LANES - n_rows, QB), F32)], axis=0)
    return jnp.transpose(sel_t).astype(BF16)


def _nsa_kernel(q_ref, ks_ref, vs_ref, kw_ref, vw_ref, kc_ref, vc_ref, gl_ref,
                cq_ref, sloq_ref, shiq_ref, ck_ref, slok_ref, shik_ref, c2s_ref, ex_ref,
                o_ref, ksx_ref, vsx_ref, kwb_ref, vwx_ref, bias_ref, kmax_ref, acc_ref, *, n_slc, n_cmp):
    step = pl.program_id(2)
    QB, HPG, dh = Q_BLOCK, NSA_HPG, HEAD_DIM
    T = ks_ref.shape[1]

    @pl.when(step == 0)
    def _():
        ks = _rope(ks_ref[0], ck_ref[...], slok_ref[...], shik_ref[...])
        ksx_ref[:, 0:dh] = ks.astype(BF16)
        ksx_ref[:, dh:2 * dh] = _aux_lanes((), False, T).astype(BF16)
        kmax_ref[...] = jnp.full(kmax_ref.shape, jnp.max(jnp.sum(ks * ks, axis=-1, keepdims=True)), F32)
        kwb_ref[...] = _rope(kw_ref[0], ck_ref[...], slok_ref[...], shik_ref[...]).astype(BF16)
        ones = jnp.ones((T, dh), BF16)
        vsx_ref[:, 0:dh] = vs_ref[0].astype(BF16)
        vsx_ref[:, dh:2 * dh] = ones
        vwx_ref[:, 0:dh] = vw_ref[0].astype(BF16)
        vwx_ref[:, dh:2 * dh] = ones

    qis = [step * NSA_QB + b for b in range(NSA_QB)]
    rows = lambda b, h: slice((b * HPG + h) * QB, (b * HPG + h + 1) * QB)
    qrows = lambda b: slice(b * QB, (b + 1) * QB)
    pieces, norms2 = [], []
    for b in range(NSA_QB):
        for h in range(HPG):
            x = q_ref[0, qrows(b), h * dh:(h + 1) * dh]
            x = _rope(x, cq_ref[qrows(b), :], sloq_ref[qrows(b), :], shiq_ref[qrows(b), :]) * (dh ** -0.5)
            pieces.append(x)
            norms2.append(jnp.sum(x * x, axis=-1, keepdims=True))
    q = jnp.concatenate(pieces, axis=0).astype(BF16)

    lane = lax.broadcasted_iota(jnp.int32, (QB, LANES), 1)
    sub = lax.broadcasted_iota(jnp.int32, (QB, LANES), 0)

    kc = kc_ref[0, 0, 0].astype(BF16)
    vc = vc_ref[0, 0, 0].astype(BF16)
    sc = _dot_nt(q, kc)
    o_cmp, sels = {}, []
    for b in range(NSA_QB):
        t = qis[b] * QB + sub
        mask_c = (lane * CMP_STRIDE + (CMP_LEN - 1) <= t) & (lane < n_cmp)
        mask_cf = mask_c.astype(F32)
        imp_c = jnp.zeros((QB, LANES), F32)
        for h in range(HPG):
            s = jnp.where(mask_c, sc[rows(b, h)], NEG)
            e = jnp.exp(s - jnp.max(s, axis=-1, keepdims=True)) * mask_cf
            p = e / jnp.maximum(jnp.sum(e, axis=-1, keepdims=True), TINY)
            imp_c = imp_c + p
            o_cmp[b, h] = _dot(p.astype(BF16), vc)
        imp = jnp.dot(imp_c, c2s_ref[...], preferred_element_type=F32, precision=lax.Precision.HIGHEST)
        sels.append(_select_blocks(imp, qis[b], n_slc))

    sel_all = jnp.concatenate(sels, axis=0)
    for c in range(T // SLC_CHUNK):
        selx = _dot(sel_all, ex_ref[:, c * SLC_CHUNK:(c + 1) * SLC_CHUNK])
        kpos = c * SLC_CHUNK + lax.broadcasted_iota(jnp.int32, (QB, SLC_CHUNK), 1)
        for b in range(NSA_QB):
            tq_ = qis[b] * QB + lax.broadcasted_iota(jnp.int32, (QB, SLC_CHUNK), 0)
            bias_ref[b, c] = jnp.where((selx[qrows(b)] > 0.5) & (kpos <= tq_), 0.0, NEG)

    o_win = {}
    for b in range(NSA_QB):
        wstart = pl.multiple_of(jnp.clip(qis[b] * QB - WIN, 0, T - WIN_SPAN), Q_BLOCK)
        qb = q[b * HPG * QB:(b + 1) * HPG * QB]
        sw = _dot_nt(qb, kwb_ref[pl.ds(wstart, WIN_SPAN), :])
        kpos = wstart + lax.broadcasted_iota(jnp.int32, (QB, WIN_SPAN), 1)
        tw = qis[b] * QB + lax.broadcasted_iota(jnp.int32, (QB, WIN_SPAN), 0)
        bias_w = jnp.where((kpos <= tw) & (kpos > tw - WIN), 0.0, NEG)
        pw = []
        for h in range(HPG):
            sh = sw[h * QB:(h + 1) * QB] + bias_w
            pw.append(jnp.exp(sh - jnp.max(sh, axis=-1, keepdims=True)).astype(BF16))
        ow = _dot(jnp.concatenate(pw, axis=0), vwx_ref[pl.ds(wstart, WIN_SPAN), :])
        for h in range(HPG):
            acc_w = ow[h * QB:(h + 1) * QB]
            o_win[b, h] = acc_w[:, 0:dh] / jnp.maximum(acc_w[:, dh:2 * dh], TINY)

    chains = [(b, h) for b in range(NSA_QB) for h in range(HPG)]

    n_chunks = (qis[-1] * QB + QB + SLC_CHUNK - 1) // SLC_CHUNK
    kmax2 = kmax_ref[0:1, 0:1]
    bounds = [jnp.sqrt(n2 * kmax2) * 1.01 for n2 in norms2]
    worst = functools.reduce(jnp.maximum, [jnp.max(bd) for bd in bounds])

    @pl.when(worst <= FOX_FAST_BOUND)
    def _():
        qx = jnp.concatenate(
            [jnp.concatenate([(x * LOG2E).astype(BF16),
                              _aux_lanes(_split3(-LOG2E * bd), True, QB).astype(BF16)], axis=1)
             for x, bd in zip(pieces, bounds)], axis=0)

        def body(c, accs):
            start = pl.multiple_of(c * SLC_CHUNK, SLC_CHUNK)
            s = _dot_nt(qx, ksx_ref[pl.ds(start, SLC_CHUNK), :])
            ps = [jnp.exp2(s[rows(b, h)] + bias_ref[b, c]).astype(BF16) for (b, h) in chains]
            pv = _dot(jnp.concatenate(ps, axis=0), vsx_ref[pl.ds(start, SLC_CHUNK), :])
            return tuple(accs[n] + pv[rows(b, h)] for n, (b, h) in enumerate(chains))

        accs = lax.fori_loop(0, n_chunks, body, tuple(jnp.zeros((QB, 2 * dh), F32) for _ in chains))
        for n, (b, h) in enumerate(chains):
            acc_ref[rows(b, h), :] = accs[n]

    @pl.when(worst > FOX_FAST_BOUND)
    def _():
        def body(c, carry):
            start = pl.multiple_of(c * SLC_CHUNK, SLC_CHUNK)
            s = _dot_nt(q, ksx_ref[pl.ds(start, SLC_CHUNK), 0:dh])
            ms, alphas, ps = [], [], []
            for n, (b, h) in enumerate(chains):
                sh = s[rows(b, h)] + bias_ref[b, c]
                m_new = jnp.maximum(carry[n][0], jnp.max(sh, axis=-1, keepdims=True))
                alphas.append(jnp.exp(carry[n][0] - m_new))
                ps.append(jnp.exp(sh - m_new).astype(BF16))
                ms.append(m_new)
            pv = _dot(jnp.concatenate(ps, axis=0), vsx_ref[pl.ds(start, SLC_CHUNK), :])
            return tuple((ms[n], alphas[n] * carry[n][1] + pv[rows(b, h)]) for n, (b, h) in enumerate(chains))

        init = tuple((jnp.full((QB, 1), NEG, F32), jnp.zeros((QB, 2 * dh), F32)) for _ in chains)
        slc = lax.fori_loop(0, n_chunks, body, init)
        for n, (b, h) in enumerate(chains):
            acc_ref[rows(b, h), :] = slc[n][1]

    for b in range(NSA_QB):
        gates = _sigmoid(gl_ref[0, qrows(b), :])
        for h in range(HPG):
            acc_s = acc_ref[rows(b, h), :]
            o_s = acc_s[:, 0:dh] / jnp.maximum(acc_s[:, dh:2 * dh], TINY)
            out = (gates[:, 3 * h:3 * h + 1] * o_cmp[b, h] + gates[:, 3 * h + 1:3 * h + 2] * o_s
                   + gates[:, 3 * h + 2:3 * h + 3] * o_win[b, h])
            o_ref[0, qrows(b), h * dh:(h + 1) * dh] = out.astype(o_ref.dtype)


def nsa_attention(qkv, cmp_kv, gl, tabs_q, B, T):
    G, HPG, dh = NSA_KV_HEADS, NSA_HPG, HEAD_DIM
    tq = NSA_QB * Q_BLOCK
    nb = T // CMP_STRIDE
    n_cmp = (T - CMP_LEN) // CMP_STRIDE + 1
    n_slc = T // SLC_LEN
    assert nb == LANES and n_slc <= LANES and T % SLC_CHUNK == 0 and T >= WIN_SPAN and T % tq == 0
    kvb = (HPG * G)

    c_start = np.arange(nb) * CMP_STRIDE
    s_start = np.arange(LANES) * SLC_LEN
    c2s = ((c_start[:, None] < s_start[None, :] + SLC_LEN) & (c_start[:, None] + CMP_LEN > s_start[None, :])
           & (np.arange(nb)[:, None] < n_cmp) & (np.arange(LANES)[None, :] < n_slc)).astype(np.float32)
    expand = (np.arange(T)[None, :] // SLC_LEN == np.arange(LANES)[:, None]).astype(np.float32)

    cos, slo, shi = tabs_q
    kv_full = lambda blk: pl.BlockSpec((1, T, dh), lambda b, g, i: (b, 0, kvb + blk + g))
    qtab = pl.BlockSpec((tq, dh), lambda b, g, i: (i, 0))
    ktab = pl.BlockSpec((T, dh), lambda b, g, i: (0, 0))
    return pl.pallas_call(
        functools.partial(_nsa_kernel, n_slc=n_slc, n_cmp=n_cmp),
        out_shape=jax.ShapeDtypeStruct((B, T, G * HPG * dh), BF16),
        grid=(B, G, T // tq),
        in_specs=[
            pl.BlockSpec((1, tq, HPG * dh), lambda b, g, i: (b, i, g)),
            kv_full(2 * G), kv_full(3 * G), kv_full(4 * G), kv_full(5 * G),
            pl.BlockSpec((1, 1, 1, nb, dh), lambda b, g, i: (b, 0, g, 0, 0)),
            pl.BlockSpec((1, 1, 1, nb, dh), lambda b, g, i: (b, 1, g, 0, 0)),
            pl.BlockSpec((1, tq, LANES), lambda b, g, i: (b, i, g)),
            qtab, qtab, qtab, ktab, ktab, ktab,
            pl.BlockSpec((nb, LANES), lambda b, g, i: (0, 0)),
            pl.BlockSpec((LANES, T), lambda b, g, i: (0, 0)),
        ],
        out_specs=pl.BlockSpec((1, tq, HPG * dh), lambda b, g, i: (b, i, g)),
        scratch_shapes=[pltpu.VMEM((T, 2 * dh), BF16), pltpu.VMEM((T, 2 * dh), BF16),
                        pltpu.VMEM((T, dh), BF16), pltpu.VMEM((T, 2 * dh), BF16),
                        pltpu.VMEM((NSA_QB, T // SLC_CHUNK, Q_BLOCK, SLC_CHUNK), F32),
                        pltpu.VMEM((8, LANES), F32),
                        pltpu.VMEM((NSA_QB * HPG * Q_BLOCK, 2 * dh), F32)],
        compiler_params=_cparams(("arbitrary", "arbitrary", "arbitrary")),
        name="nsa_attention",
    )(qkv, qkv, qkv, qkv, qkv, cmp_kv, cmp_kv, gl, cos, slo, shi, cos, slo, shi,
      jnp.asarray(c2s), jnp.asarray(expand, dtype=BF16))


CONV_PAD = 32
CONV_CHUNK = 256


def _conv_kernel(a_ref, b_ref, w_ref, cb_ref, g_ref, gb_ref, o_ref, u_ref):
    T = a_ref.shape[1]
    u_ref[0:CONV_PAD, :] = jnp.zeros((CONV_PAD, LANES), F32)
    u_ref[CONV_PAD:CONV_PAD + T, :] = a_ref[0].astype(F32) * _sigmoid(b_ref[0].astype(F32))
    base = CONV_PAD - (CONV_KERNEL - 1)
    for c in range(T // CONV_CHUNK):
        t0 = c * CONV_CHUNK
        acc = jnp.zeros((CONV_CHUNK, LANES), F32)
        for k in range(CONV_KERNEL):
            acc = acc + u_ref[t0 + base + k:t0 + base + k + CONV_CHUNK, :] * w_ref[k:k + 1, :]
        acc = acc + cb_ref[...]
        mu = jnp.mean(acc, axis=-1, keepdims=True)
        d = acc - mu
        var = jnp.mean(d * d, axis=-1, keepdims=True)
        y = d * lax.rsqrt(var + LN_EPS) * g_ref[...] + gb_ref[...]
        o_ref[0, t0:t0 + CONV_CHUNK, :] = (y * _sigmoid(y)).astype(o_ref.dtype)


def conv_module(glu, conv_w, conv_b, gn_g, gn_b, B, T):
    C = glu.shape[-1] // 2
    ng = C // LANES
    assert C // CONV_GROUPS == LANES
    vec = pl.BlockSpec((1, LANES), lambda b, g: (0, g))
    return pl.pallas_call(
        _conv_kernel,
        out_shape=jax.ShapeDtypeStruct((B, T, C), BF16),
        grid=(B, ng),
        in_specs=[
            pl.BlockSpec((1, T, LANES), lambda b, g: (b, 0, g)),
            pl.BlockSpec((1, T, LANES), lambda b, g: (b, 0, ng + g)),
            pl.BlockSpec((CONV_KERNEL, LANES), lambda b, g: (0, g)),
            vec, vec, vec,
        ],
        out_specs=pl.BlockSpec((1, T, LANES), lambda b, g: (b, 0, g)),
        scratch_shapes=[pltpu.VMEM((CONV_PAD + T, LANES), F32)],
        compiler_params=_cparams(("arbitrary", "arbitrary")),
        name="conformer_conv",
    )(glu, glu, conv_w, conv_b.reshape(1, C), gn_g.reshape(1, C), gn_b.reshape(1, C))


def _split2(a):
    hi = a.astype(BF16)
    return hi, (a - hi.astype(F32)).astype(BF16)


def _router_kernel(x_ref, w_ref, bias_ref, idx_ref, wts_ref):
    xh, xl = _split2(x_ref[...])
    wh, wl = _split2(w_ref[...])
    logits = _dot(xh, wh) + (_dot(xh, wl) + _dot(xl, wh))
    aff = _sigmoid(jnp.concatenate(
        [jnp.transpose(logits[c * LANES:(c + 1) * LANES])[0:N_EXPERTS] for c in range(logits.shape[0] // LANES)],
        axis=1))
    sel = aff + bias_ref[...]
    a = [aff[e:e + 1, :] for e in range(N_EXPERTS)]
    s = [sel[e:e + 1, :] for e in range(N_EXPERTS)]
    P = EXPERTS_PER_GROUP
    grp = []
    for g in range(N_EXPERT_GROUPS):
        v = s[g * P:(g + 1) * P]
        best = None
        for i in range(P):
            for j in range(i + 1, P):
                pair = v[i] + v[j]
                best = pair if best is None else jnp.maximum(best, pair)
        grp.append(best)
    gbest = jnp.zeros_like(grp[0], dtype=jnp.int32)
    gval = grp[0]
    for g in range(1, N_EXPERT_GROUPS):
        better = grp[g] > gval
        gbest = jnp.where(better, g, gbest)
        gval = jnp.where(better, grp[g], gval)
    cs, ca = [], []
    for p in range(P):
        sv, av = s[p], a[p]
        for g in range(1, N_EXPERT_GROUPS):
            sv = jnp.where(gbest == g, s[g * P + p], sv)
            av = jnp.where(gbest == g, a[g * P + p], av)
        cs.append(sv)
        ca.append(av)
    i1 = jnp.zeros_like(gbest)
    v1, a1 = cs[0], ca[0]
    for p in range(1, P):
        better = cs[p] > v1
        i1 = jnp.where(better, p, i1)
        v1 = jnp.where(better, cs[p], v1)
        a1 = jnp.where(better, ca[p], a1)
    i2 = jnp.full_like(gbest, -1)
    v2 = jnp.full_like(v1, -jnp.inf)
    a2 = jnp.zeros_like(a1)
    for p in range(P):
        better = (i1 != p) & (cs[p] > v2)
        i2 = jnp.where(better, p, i2)
        v2 = jnp.where(better, cs[p], v2)
        a2 = jnp.where(better, ca[p], a2)
    den = a1 + a2
    idx_ref[...] = jnp.zeros(idx_ref.shape, jnp.int32)
    wts_ref[...] = jnp.zeros(wts_ref.shape, F32)
    idx_ref[0:1, :] = gbest * P + i1
    idx_ref[1:2, :] = gbest * P + i2
    wts_ref[0:1, :] = a1 / den
    wts_ref[1:2, :] = a2 / den


def router(x, router_w, router_bias, *, tm=1024):
    N, D = x.shape
    E = N_EXPERTS
    return pl.pallas_call(
        _router_kernel,
        out_shape=(jax.ShapeDtypeStruct((8, N), jnp.int32), jax.ShapeDtypeStruct((8, N), F32)),
        grid=(N // tm,),
        in_specs=[
            pl.BlockSpec((tm, D), lambda i: (i, 0)),
            pl.BlockSpec((D, LANES), lambda i: (0, 0)),
            pl.BlockSpec((E, 1), lambda i: (0, 0)),
        ],
        out_specs=(pl.BlockSpec((8, tm), lambda i: (0, i)), pl.BlockSpec((8, tm), lambda i: (0, i))),
        compiler_params=_cparams(("arbitrary",)),
        name="moe_router",
    )(x, jnp.pad(router_w, ((0, 0), (0, LANES - E))), router_bias.reshape(E, 1))


PLAN_CHUNK = 512


def _plan_kernel(idx_ref, dest_ref, cnt_ref, *, n_tok):
    E = N_EXPERTS
    sub = lax.broadcasted_iota(jnp.int32, (TOP_K * E, n_tok), 0)
    tgt = jnp.where(sub < E, idx_ref[0:1, :], idx_ref[1:2, :])
    onehot = ((sub & (E - 1)) == tgt).astype(F32)
    onehot_b = onehot.astype(BF16)
    tri = (lax.broadcasted_iota(jnp.int32, (PLAN_CHUNK, PLAN_CHUNK), 0)
           <= lax.broadcasted_iota(jnp.int32, (PLAN_CHUNK, PLAN_CHUNK), 1)).astype(F32).astype(BF16)
    carry = jnp.zeros((TOP_K * E, 1), F32)
    parts = []
    for c in range(n_tok // PLAN_CHUNK):
        pre = _dot(onehot_b[:, c * PLAN_CHUNK:(c + 1) * PLAN_CHUNK], tri) + carry
        parts.append(pre)
        carry = pre[:, PLAN_CHUNK - 1:PLAN_CHUNK]
    excl = jnp.concatenate(parts, axis=1) - onehot
    cnt0 = carry[0:E]
    tot = cnt0 + carry[E:2 * E]
    lower = (lax.broadcasted_iota(jnp.int32, (E, E), 1)
             < lax.broadcasted_iota(jnp.int32, (E, E), 0)).astype(F32)
    offs = jnp.dot(lower, jnp.broadcast_to(tot, (E, LANES)), preferred_element_type=F32,
                   precision=lax.Precision.HIGHEST)[:, 0:1]
    base = jnp.concatenate([offs, offs + cnt0], axis=0)
    val = onehot * (base + excl)
    dest_ref[...] = jnp.zeros(dest_ref.shape, jnp.int32)
    dest_ref[0:1, :] = jnp.sum(val[0:E], axis=0, keepdims=True).astype(jnp.int32)
    dest_ref[1:2, :] = jnp.sum(val[E:2 * E], axis=0, keepdims=True).astype(jnp.int32)
    cnt_ref[...] = jnp.broadcast_to(tot, (E, LANES)).astype(jnp.int32)


def moe_plan(idx):
    n_tok = idx.shape[1]
    assert n_tok % PLAN_CHUNK == 0 and TOP_K == 2
    return pl.pallas_call(
        functools.partial(_plan_kernel, n_tok=n_tok),
        out_shape=(jax.ShapeDtypeStruct((8, n_tok), jnp.int32),
                   jax.ShapeDtypeStruct((N_EXPERTS, LANES), jnp.int32)),
        compiler_params=_cparams(None),
        name="moe_plan",
    )(idx)


def _scatter_kernel(dest_ref, x_ref, xs_hbm, sem, *, tm, n_tok):
    base = pl.program_id(0) * tm

    def copies(start):
        def make(r, k, d):
            cp = pltpu.make_async_copy(x_ref.at[pl.ds(r, 1)], xs_hbm.at[pl.ds(d, 1)], sem)
            if start:
                cp.start(priority=k)
            else:
                cp.wait()
        _row_copies(dest_ref, n_tok, base, tm, make)

    copies(True)
    copies(False)


def moe_scatter(x, dest, *, tm=256):
    N, D = x.shape
    grid_spec = pltpu.PrefetchScalarGridSpec(
        num_scalar_prefetch=1,
        grid=(N // tm,),
        in_specs=[pl.BlockSpec((tm, D), lambda i, d: (i, 0))],
        out_specs=pl.BlockSpec(memory_space=pl.ANY),
        scratch_shapes=[pltpu.SemaphoreType.DMA(())],
    )
    return pl.pallas_call(
        functools.partial(_scatter_kernel, tm=tm, n_tok=N),
        out_shape=jax.ShapeDtypeStruct((TOP_K * N, D), x.dtype),
        grid_spec=grid_spec,
        compiler_params=_cparams(("arbitrary",)),
        name="moe_scatter",
    )(dest, x)


def _moe_kernel(ti_ref, te_ref, lo_ref, hi_ref, first_ref, head_ref, slot_ref, nxt_ref,
                xs_ref, wg_hbm, wu_hbm, wd_hbm, y_ref,
                wg32_ref, wu32_ref, wd32_ref, wgb_ref, wub_ref, wdb_ref, sem, *, layer):
    i = pl.program_id(0)

    def weight_copies(e, slot):
        return [pltpu.make_async_copy(src.at[layer, e], dst.at[slot], sem.at[slot, n])
                for n, (src, dst) in enumerate(((wg_hbm, wg32_ref), (wu_hbm, wu32_ref), (wd_hbm, wd32_ref)))]

    @pl.when(i == 0)
    def _():
        for cp in weight_copies(te_ref[0], 0):
            cp.start()

    @pl.when(head_ref[i] == 1)
    def _():
        slot = slot_ref[i]
        for cp in weight_copies(te_ref[i], slot):
            cp.wait()
        wgb_ref[...] = wg32_ref[slot].astype(BF16)
        wub_ref[...] = wu32_ref[slot].astype(BF16)
        wdb_ref[...] = wd32_ref[slot].astype(BF16)

        @pl.when(nxt_ref[i] >= 0)
        def _():
            for cp in weight_copies(nxt_ref[i], 1 - slot):
                cp.start()

    lo, hi = lo_ref[i], hi_ref[i]
    tm = xs_ref.shape[0]
    half = tm // 2

    def ffn(r0, n):
        x = xs_ref[r0:r0 + n, :].astype(BF16)
        g = _dot(x, wgb_ref[...])
        u = _dot(x, wub_ref[...])
        rows = r0 + lax.broadcasted_iota(jnp.int32, (n, 1), 0)
        h = jnp.where((rows >= lo) & (rows < hi), (g * _sigmoid(g)) * u, 0.0)
        y = _dot(h.astype(BF16), wdb_ref[...])

        @pl.when(first_ref[i] == 1)
        def _():
            y_ref[r0:r0 + n, :] = y

        @pl.when(first_ref[i] == 0)
        def _():
            y_ref[r0:r0 + n, :] += y

    lower, upper = lo < half, hi > half

    @pl.when(lower & upper)
    def _():
        ffn(0, tm)

    @pl.when(lower & jnp.logical_not(upper) & (hi > lo))
    def _():
        ffn(0, half)

        @pl.when(first_ref[i] == 1)
        def _():
            y_ref[half:tm, :] = jnp.zeros((tm - half, y_ref.shape[1]), y_ref.dtype)

    @pl.when(upper & jnp.logical_not(lower))
    def _():
        ffn(half, tm - half)


def moe_experts(xs, items, w_gate, w_up, w_down, layer, *, tm=MOE_TM):
    P, D = xs.shape
    Fh = w_gate.shape[-1]
    n_items = items[0].shape[0]
    xmap = lambda i, ti, *_: (ti[i], 0)
    hbm = pl.BlockSpec(memory_space=pl.ANY)
    grid_spec = pltpu.PrefetchScalarGridSpec(
        num_scalar_prefetch=len(items),
        grid=(n_items,),
        in_specs=[pl.BlockSpec((tm, D), xmap), hbm, hbm, hbm],
        out_specs=pl.BlockSpec((tm, D), xmap),
        scratch_shapes=[pltpu.VMEM((2, D, Fh), F32), pltpu.VMEM((2, D, Fh), F32), pltpu.VMEM((2, Fh, D), F32),
                        pltpu.VMEM((D, Fh), BF16), pltpu.VMEM((D, Fh), BF16), pltpu.VMEM((Fh, D), BF16),
                        pltpu.SemaphoreType.DMA((2, 3))],
    )
    return pl.pallas_call(
        functools.partial(_moe_kernel, layer=layer),
        out_shape=jax.ShapeDtypeStruct((P, D), F32),
        grid_spec=grid_spec,
        compiler_params=_cparams(("arbitrary",)),
        name="moe_experts",
    )(*items, xs, w_gate, w_up, w_down)


def moe_items(tot, n_rows, tm=MOE_TM):
    E = N_EXPERTS
    n_max = n_rows // tm + E
    ar = jnp.arange(E, dtype=jnp.int32)
    ends = jnp.cumsum(tot)
    offs = ends - tot
    first_tile = offs // tm
    n_e = jnp.where(tot > 0, (ends - 1) // tm - first_tile + 1, 0)
    s_end = jnp.cumsum(n_e)
    s_beg = s_end - n_e
    n_items = s_end[-1]
    i = jnp.arange(n_max, dtype=jnp.int32)
    ic = jnp.minimum(i, n_items - 1)
    e_i = jnp.sum((ic[:, None] >= s_end[None, :]).astype(jnp.int32), axis=1)
    pick = (e_i[:, None] == ar[None, :]).astype(jnp.int32)
    at = lambda v: jnp.sum(pick * v[None, :], axis=1)
    tile = at(first_tile) + ic - at(s_beg)
    live = i < n_items
    lo = jnp.where(live, jnp.maximum(at(offs), tile * tm) - tile * tm, 0)
    hi = jnp.where(live, jnp.minimum(at(ends), tile * tm + tm) - tile * tm, 0)
    prev_tile = jnp.concatenate([jnp.full((1,), -1, jnp.int32), tile[:-1]])
    first = live & (tile != prev_tile)
    head = live & (ic == at(s_beg))
    used = (tot > 0).astype(jnp.int32)
    slot = at(jnp.cumsum(used) - used) & 1
    later = jnp.where((ar[None, :] > ar[:, None]) & (tot[None, :] > 0), ar[None, :], E)
    nxt_e = jnp.min(later, axis=1)
    nxt = at(jnp.where(nxt_e < E, nxt_e, -1))
    return tuple(v.astype(jnp.int32) for v in (tile, e_i, lo, hi, first, head, slot, nxt))


def _rope_tables(pos):
    inv = ROPE_THETA ** (-jnp.arange(0, ROPE_DIM, 2, dtype=F32) / ROPE_DIM)
    ang = pos.astype(F32)[:, None] * inv[None, :]
    cos, sin = jnp.cos(ang), jnp.sin(ang)
    n = pos.shape[0]
    rest = HEAD_DIM - ROPE_DIM
    cos_t = jnp.concatenate([cos, cos, jnp.ones((n, rest), F32)], axis=1)
    sin_lo = jnp.concatenate([-sin, jnp.zeros((n, HEAD_DIM - ROPE_HALF), F32)], axis=1)
    sin_hi = jnp.concatenate([jnp.zeros((n, ROPE_HALF), F32), sin, jnp.zeros((n, rest), F32)], axis=1)
    return cos_t, sin_lo, sin_hi


def _even_mixer(xb, B, T, w_in_all, j, pos, w1, w2, conv_w, conv_b, gn_g, gn_b):
    N, D = xb.shape
    G, HPG, dh = NSA_KV_HEADS, NSA_HPG, HEAD_DIM
    nsa_w = G * HPG * dh
    kv_cols = 3 * 2 * G * dh
    n_gate = 3 * G * HPG
    conv_c = (w_in_all.shape[2] - nsa_w - kv_cols - n_gate) // 2
    qkv = linear(xb, w_in_all, nsa_w + kv_cols, F32, layer=j, tm=512, tn=(nsa_w + kv_cols) // 2, name="even_qkv")
    w_gl = w_in_all[j, :, nsa_w + kv_cols:nsa_w + kv_cols + n_gate].reshape(D, G, 3 * HPG)
    w_gl = jnp.pad(w_gl, ((0, 0), (0, 0), (0, LANES - 3 * HPG))).reshape(D, G * LANES)
    gl = linear(xb, w_gl, G * LANES, F32, tn=G * LANES, name="even_gates")
    glu = linear(xb, w_in_all[j, :, nsa_w + kv_cols + n_gate:], 2 * conv_c, BF16, tn=1024, name="even_glu")

    qkv3 = qkv.reshape(B, T, nsa_w + kv_cols)
    n_cmp = (T - CMP_LEN) // CMP_STRIDE + 1
    nb = T // CMP_STRIDE
    cmp_end = jnp.arange(nb) * CMP_STRIDE + (CMP_LEN - 1)
    cmp_kv = compress(qkv3, pos, w1, w2, _rope_tables(cmp_end), B, T, G * HPG)
    o_nsa = nsa_attention(qkv3, cmp_kv, gl.reshape(B, T, G * LANES), _rope_tables(jnp.arange(T)), B, T)
    u = conv_module(glu.reshape(B, T, 2 * conv_c), conv_w, conv_b, gn_g, gn_b, B, T)
    return o_nsa.reshape(N, nsa_w), u.reshape(N, conv_c)


def _fox_mixer(xb, B, T, w_in_all, j, f_bias, q_gain, k_gain):
    N, D = xb.shape
    H = D // HEAD_DIM
    proj = linear(xb, w_in_all, 4 * D, BF16, layer=j, tn=1024, name="fox_qkvg")
    w_f = jnp.pad(w_in_all[j, :, 4 * D:], ((0, 0), (0, LANES - H)))
    fl = linear(xb, w_f, LANES, F32, tn=LANES, name="fox_forget")
    c = fox_decay(fl, jnp.pad(f_bias, (0, LANES - H)), B, T)[:, :H].reshape(B, T, H)
    return fox_attention(proj.reshape(B, T, 4 * D), c, q_gain, k_gain, B, T, H).reshape(N, D)


def _moe(xf, router_w, router_bias, w_gate, w_up, w_down, layer):
    N, D = xf.shape
    idx, wts = router(xf, router_w, router_bias)
    dest8, cnt = moe_plan(idx)
    dest = dest8[:TOP_K].reshape(-1)
    xs = moe_scatter(xf, dest)
    y = moe_experts(xs, moe_items(cnt[:, 0], TOP_K * N), w_gate, w_up, w_down, layer)
    return y, dest, wts[:TOP_K].T


def kernel(x, even_w_in, even_w_out, nsa_cmp_pos, nsa_cmp_w1, nsa_cmp_w2, conv_w, conv_b, conv_gn_g, conv_gn_b, fox_w_in, fox_f_bias, fox_q_gain, fox_k_gain, fox_w_out, ln_mix_g, ln_mix_b, ln_ffn_g, ln_ffn_b, router_w, router_bias, exp_w_gate, exp_w_up, exp_w_down):
    B, T, D = x.shape
    depth = ln_mix_g.shape[0]
    alpha = (2.0 * depth) ** 0.25
    N = B * T
    xf = x.reshape(N, D)
    xb = xf
    for layer in range(depth):
        j = layer // 2
        if layer % 2 == 0:
            a1, a2 = _even_mixer(xb, B, T, even_w_in, j, nsa_cmp_pos[j], nsa_cmp_w1[j], nsa_cmp_w2[j],
                                 conv_w[j], conv_b[j], conv_gn_g[j], conv_gn_b[j])
            xf, xb = outproj_ln(a1, a2, 0, even_w_out, j, xf, ln_mix_g[layer], ln_mix_b[layer], alpha)
        else:
            a = _fox_mixer(xb, B, T, fox_w_in, j, fox_f_bias[j], fox_q_gain[j], fox_k_gain[j])
            xf, xb = outproj_ln(a, a, 1, fox_w_out, j, xf, ln_mix_g[layer], ln_mix_b[layer], alpha)
        y, dest, wts = _moe(xf, router_w, router_bias, exp_w_gate, exp_w_up, exp_w_down, layer)
        xf, xb = combine_ln(xf, y, dest, wts, ln_ffn_g[layer], ln_ffn_b[layer], alpha)
    return xf.reshape(B, T, D)
```

```python
import functools

import numpy as np
import jax
import jax.numpy as jnp
from jax import lax
from jax.experimental import pallas as pl
from jax.experimental.pallas import tpu as pltpu

F32 = jnp.float32
BF16 = jnp.bfloat16

HEAD_DIM = 128
ROPE_THETA = 500000.0
ROPE_DIM = HEAD_DIM // 4
ROPE_HALF = ROPE_DIM // 2
Q_BLOCK = 128

NSA_KV_HEADS = 2
NSA_HPG = 4
CMP_LEN = 32
CMP_STRIDE = 16
SLC_LEN = 64
N_SLC = 8
WIN = 512
CONV_KERNEL = 31
CONV_GROUPS = 8

N_EXPERTS = 16
N_EXPERT_GROUPS = 4
EXPERTS_PER_GROUP = 4
TOP_K = 2

LN_EPS = 1e-5
NEG = -1e30
BIG = 1e9
TINY = 1e-30

LANES = 128
V7X_VMEM_BYTES = 64 * 1024 * 1024
VMEM_LIMIT = V7X_VMEM_BYTES * 7 // 8

MOE_TM = 256


def _cparams(sem, vmem=VMEM_LIMIT):
    return pltpu.CompilerParams(dimension_semantics=sem, vmem_limit_bytes=vmem)


def _dot(a, b):
    return jnp.dot(a, b, preferred_element_type=F32)


def _dot_nt(a, b):
    return lax.dot_general(a, b, (((1,), (1,)), ((), ())), preferred_element_type=F32)


def _sigmoid(x):
    return 1.0 / (1.0 + jnp.exp(-x))


def _linear_kernel(x_ref, w_ref, o_ref, wb_ref):
    @pl.when(pl.program_id(1) == 0)
    def _():
        wb_ref[...] = w_ref[...].astype(BF16)

    o_ref[...] = _dot(x_ref[...].astype(BF16), wb_ref[...]).astype(o_ref.dtype)


def linear(x, w, n_cols, out_dtype, *, layer=None, tm=1024, tn=512, name="linear"):
    M, K = x.shape
    tm = min(tm, M)
    tn = min(tn, n_cols)
    assert M % tm == 0 and n_cols % tn == 0
    row_block = 0
    if layer is not None:
        w = w.reshape(-1, w.shape[-1])
        row_block = layer
    w_spec = pl.BlockSpec((K, tn), lambda j, i: (row_block, j))
    return pl.pallas_call(
        _linear_kernel,
        out_shape=jax.ShapeDtypeStruct((M, n_cols), out_dtype),
        grid=(n_cols // tn, M // tm),
        in_specs=[
            pl.BlockSpec((tm, K), lambda j, i: (i, 0)),
            w_spec,
        ],
        out_specs=pl.BlockSpec((tm, tn), lambda j, i: (i, j)),
        scratch_shapes=[pltpu.VMEM((K, tn), BF16)],
        compiler_params=_cparams(("arbitrary", "arbitrary")),
        name=name,
    )(x, w)


def _layer_norm_rows(y, g, b):
    mu = jnp.mean(y, axis=-1, keepdims=True)
    d = y - mu
    var = jnp.mean(d * d, axis=-1, keepdims=True)
    return d * lax.rsqrt(var + LN_EPS) * g + b


def _outproj_ln_kernel(a1_ref, a2_ref, w_ref, x_ref, g_ref, b_ref, xo_ref, xb_ref, wb_ref, *, alpha):
    @pl.when(pl.program_id(0) == 0)
    def _():
        wb_ref[...] = w_ref[...].astype(BF16)

    half = a1_ref.shape[1]
    h = _dot(a1_ref[...].astype(BF16), wb_ref[0:half, :])
    h = h + _dot(a2_ref[...].astype(BF16), wb_ref[half:2 * half, :])
    out = _layer_norm_rows(alpha * x_ref[...] + h, g_ref[...], b_ref[...])
    xo_ref[...] = out
    xb_ref[...] = out.astype(BF16)


def outproj_ln(a1, a2, a2_col_block, w, layer, x, g, b, alpha, *, tm=512):
    M, D = x.shape
    half = D // 2
    return pl.pallas_call(
        functools.partial(_outproj_ln_kernel, alpha=alpha),
        out_shape=(jax.ShapeDtypeStruct((M, D), F32), jax.ShapeDtypeStruct((M, D), BF16)),
        grid=(M // tm,),
        in_specs=[
            pl.BlockSpec((tm, half), lambda i: (i, 0)),
            pl.BlockSpec((tm, half), lambda i: (i, a2_col_block)),
            pl.BlockSpec((None, D, D), lambda i: (layer, 0, 0), pipeline_mode=pl.Buffered(1)),
            pl.BlockSpec((tm, D), lambda i: (i, 0)),
            pl.BlockSpec((1, D), lambda i: (0, 0)),
            pl.BlockSpec((1, D), lambda i: (0, 0)),
        ],
        out_specs=(pl.BlockSpec((tm, D), lambda i: (i, 0)), pl.BlockSpec((tm, D), lambda i: (i, 0))),
        scratch_shapes=[pltpu.VMEM((D, D), BF16)],
        compiler_params=_cparams(("arbitrary",)),
        name="outproj_ln",
    )(a1, a2, w, x, g.reshape(1, D), b.reshape(1, D))


ROW_DMA_UNROLL = 8


def _row_copies(dest_ref, n_tok, base, tm, make):
    def body(g, carry):
        r0 = pl.multiple_of(g * ROW_DMA_UNROLL, ROW_DMA_UNROLL)
        for u in range(ROW_DMA_UNROLL):
            for k in range(TOP_K):
                make(r0 + u, k, dest_ref[k * n_tok + base + r0 + u])
        return carry

    lax.fori_loop(0, tm // ROW_DMA_UNROLL, body, 0)


def _combine_ln_kernel(dest_ref, x_ref, w_ref, g_ref, b_ref, y_hbm, xo_ref, xb_ref, buf_ref, sem, *,
                       alpha, tm, n_tok):
    i = pl.program_id(0)
    n_steps = pl.num_programs(0)

    def copies(step, slot, start):
        def make(r, k, d):
            cp = pltpu.make_async_copy(y_hbm.at[pl.ds(d, 1)], buf_ref.at[slot, k, pl.ds(r, 1)], sem.at[slot])
            if start:
                cp.start(priority=k)
            else:
                cp.wait()
        _row_copies(dest_ref, n_tok, step * tm, tm, make)

    @pl.when(i == 0)
    def _():
        copies(0, 0, True)

    @pl.when(i + 1 < n_steps)
    def _():
        copies(i + 1, (i + 1) % 2, True)

    slot = i % 2
    copies(i, slot, False)
    f = w_ref[:, 0:1] * buf_ref[slot, 0] + w_ref[:, 1:2] * buf_ref[slot, 1]
    out = _layer_norm_rows(alpha * x_ref[...] + f, g_ref[...], b_ref[...])
    xo_ref[...] = out
    xb_ref[...] = out.astype(BF16)


def combine_ln(x, y, dest, wts, g, b, alpha, *, tm=256):
    N, D = x.shape
    row = lambda i, d: (i, 0)
    grid_spec = pltpu.PrefetchScalarGridSpec(
        num_scalar_prefetch=1,
        grid=(N // tm,),
        in_specs=[
            pl.BlockSpec((tm, D), row),
            pl.BlockSpec((tm, TOP_K), row),
            pl.BlockSpec((1, D), lambda i, d: (0, 0)),
            pl.BlockSpec((1, D), lambda i, d: (0, 0)),
            pl.BlockSpec(memory_space=pl.ANY),
        ],
        out_specs=(pl.BlockSpec((tm, D), row), pl.BlockSpec((tm, D), row)),
        scratch_shapes=[pltpu.VMEM((2, TOP_K, tm, D), F32), pltpu.SemaphoreType.DMA((2,))],
    )
    return pl.pallas_call(
        functools.partial(_combine_ln_kernel, alpha=alpha, tm=tm, n_tok=N),
        out_shape=(jax.ShapeDtypeStruct((N, D), F32), jax.ShapeDtypeStruct((N, D), BF16)),
        grid_spec=grid_spec,
        compiler_params=_cparams(("arbitrary",)),
        name="moe_combine_ln",
    )(dest, x, wts, g.reshape(1, D), b.reshape(1, D), y)


def _rms(x, gain):
    return x * lax.rsqrt(jnp.mean(x * x, axis=-1, keepdims=True) + LN_EPS) * gain


CUM_CHUNK = 256


def _decay_kernel(fl_ref, bias_ref, c_ref):
    T = fl_ref.shape[0]
    tri = (lax.broadcasted_iota(jnp.int32, (CUM_CHUNK, CUM_CHUNK), 0)
           >= lax.broadcasted_iota(jnp.int32, (CUM_CHUNK, CUM_CHUNK), 1)).astype(F32)
    carry = jnp.zeros((1, LANES), F32)
    for c in range(T // CUM_CHUNK):
        z = fl_ref[c * CUM_CHUNK:(c + 1) * CUM_CHUNK, :] + bias_ref[...]
        log_f = jnp.minimum(z, 0.0) - jnp.log1p(jnp.exp(-jnp.abs(z)))
        cs = jnp.dot(tri, log_f, preferred_element_type=F32, precision=lax.Precision.HIGHEST) + carry
        c_ref[c * CUM_CHUNK:(c + 1) * CUM_CHUNK, :] = cs
        carry = cs[CUM_CHUNK - 1:CUM_CHUNK, :]


def fox_decay(fl, bias, B, T):
    assert T % CUM_CHUNK == 0
    return pl.pallas_call(
        _decay_kernel,
        out_shape=jax.ShapeDtypeStruct(fl.shape, F32),
        grid=(B,),
        in_specs=[pl.BlockSpec((T, LANES), lambda b: (b, 0)), pl.BlockSpec((1, LANES), lambda b: (0, 0))],
        out_specs=pl.BlockSpec((T, LANES), lambda b: (b, 0)),
        compiler_params=_cparams(("arbitrary",)),
        name="fox_decay",
    )(fl, bias.reshape(1, LANES))


LOG2E = 1.4426950408889634
FAST_SOFTMAX_BOUND = 38.0
BIAS_PIECES = 3


def _split3(a):
    hi = a.astype(BF16).astype(F32)
    r = a - hi
    mid = r.astype(BF16).astype(F32)
    return hi, mid, (r - mid).astype(BF16).astype(F32)


def _aux_lanes(pieces, ones_first, n):
    lane = lax.broadcasted_iota(jnp.int32, (n, LANES), 1)
    p0 = BIAS_PIECES if ones_first else 0
    o0 = 0 if ones_first else BIAS_PIECES
    out = jnp.where((lane >= o0) & (lane < o0 + BIAS_PIECES), 1.0, 0.0)
    for i, piece in enumerate(pieces):
        out = out + jnp.where(lane == p0 + i, piece, 0.0)
    return out


def _fox_kernel(fast_ref, q_ref, k_ref, v_ref, og_ref, ckrow_ref, ckcol_ref, cq_ref, bound_ref, qg_ref, kg_ref,
                o_ref, kx_ref, vx_ref, *, tq, tk, hb):
    qi = pl.program_id(2)
    dh = HEAD_DIM
    T = k_ref.shape[1]

    def piece_by_lane(a, lane):
        hi, mid, lo = _split3(a)
        in_group = lambda g: ((lane >= g * hb) & (lane < (g + 1) * hb)) | (
            (lane >= (BIAS_PIECES + g) * hb) & (lane < (BIAS_PIECES + g + 1) * hb))
        return jnp.where(in_group(0), hi, jnp.where(in_group(1), mid, lo))

    @pl.when(qi == 0)
    def _():
        lane_k = lax.broadcasted_iota(jnp.int32, (T, LANES), 1)
        k_aux = jnp.where(lane_k < BIAS_PIECES * hb, piece_by_lane(-LOG2E * ckcol_ref[0, 0], lane_k),
                          jnp.where(lane_k < 2 * BIAS_PIECES * hb, 1.0, 0.0)).astype(BF16)
        for h in range(hb):
            kx_ref[h, :, 0:dh] = _rms(k_ref[0, :, h * dh:(h + 1) * dh].astype(F32), kg_ref[...]).astype(BF16)
            kx_ref[h, :, dh:2 * dh] = k_aux
            vx_ref[h, :, 0:dh] = v_ref[0, :, h * dh:(h + 1) * dh].astype(BF16)
            vx_ref[h, :, dh:2 * dh] = jnp.ones((T, dh), BF16)

    row = qi * tq + lax.broadcasted_iota(jnp.int32, (tq, tk), 0)
    lane = lax.broadcasted_iota(jnp.int32, (tq, tk), 1)
    n_full = (qi * tq) // tk

    def finish(accs):
        for h in range(hb):
            o = accs[h][:, 0:dh] / jnp.maximum(accs[h][:, dh:2 * dh], TINY)
            gate = _sigmoid(og_ref[0, :, h * dh:(h + 1) * dh].astype(F32))
            o_ref[0, :, h * dh:(h + 1) * dh] = (o * gate).astype(o_ref.dtype)

    def q_normed(h):
        return _rms(q_ref[0, :, h * dh:(h + 1) * dh].astype(F32), qg_ref[...]) * (dh ** -0.5)

    @pl.when(fast_ref[0] == 1)
    def _():
        lane_q = lax.broadcasted_iota(jnp.int32, (tq, LANES), 1)
        r = piece_by_lane(LOG2E * (cq_ref[0, 0] - bound_ref[0:1, 0:1]), lane_q)
        qx = []
        for h in range(hb):
            mine = (lane_q & (hb - 1)) == h
            q_aux = jnp.where(mine & (lane_q < BIAS_PIECES * hb), 1.0,
                              jnp.where(mine & (lane_q < 2 * BIAS_PIECES * hb), r, 0.0))
            qx.append(jnp.concatenate([(q_normed(h) * LOG2E).astype(BF16), q_aux.astype(BF16)], axis=1))

        def step(j, accs, masked):
            start = pl.multiple_of(j * tk, tk)
            out = []
            for h in range(hb):
                s = _dot_nt(qx[h], kx_ref[h, pl.ds(start, tk), :])
                if masked:
                    s = jnp.where(j * tk + lane <= row, s, NEG)
                out.append(accs[h] + _dot(jnp.exp2(s).astype(BF16), vx_ref[h, pl.ds(start, tk), :]))
            return tuple(out)

        def diagonal(accs):
            hq = tq // 2
            start = pl.multiple_of(n_full * tk, tk)
            tri = (lax.broadcasted_iota(jnp.int32, (hq, hq), 1) <= lax.broadcasted_iota(jnp.int32, (hq, hq), 0))
            out = []
            for h in range(hb):
                k_lo, k_hi = kx_ref[h, pl.ds(start, hq), :], kx_ref[h, pl.ds(start + hq, hq), :]
                v_lo, v_all = vx_ref[h, pl.ds(start, hq), :], vx_ref[h, pl.ds(start, tk), :]
                p_top = jnp.exp2(jnp.where(tri, _dot_nt(qx[h][0:hq], k_lo), NEG)).astype(BF16)
                p_bot = jnp.concatenate(
                    [jnp.exp2(_dot_nt(qx[h][hq:tq], k_lo)).astype(BF16),
                     jnp.exp2(jnp.where(tri, _dot_nt(qx[h][hq:tq], k_hi), NEG)).astype(BF16)], axis=1)
                out.append(accs[h] + jnp.concatenate([_dot(p_top, v_lo), _dot(p_bot, v_all)], axis=0))
            return out

        accs = tuple(jnp.zeros((tq, 2 * dh), F32) for _ in range(hb))
        accs = lax.fori_loop(0, n_full, lambda j, a: step(j, a, False), accs)
        finish(diagonal(accs) if tq == tk else step(n_full, accs, True))

    @pl.when(fast_ref[0] == 0)
    def _():
        qs = [q_normed(h).astype(BF16) for h in range(hb)]

        def step(j, carry, masked):
            start = pl.multiple_of(j * tk, tk)
            out = []
            for h in range(hb):
                m, acc = carry[h]
                s = _dot_nt(qs[h], kx_ref[h, pl.ds(start, tk), 0:dh]) - ckrow_ref[0, h, pl.ds(j, 1), :]
                if masked:
                    s = jnp.where(j * tk + lane <= row, s, NEG)
                m_new = jnp.maximum(m, jnp.max(s, axis=-1, keepdims=True))
                p = jnp.exp(s - m_new).astype(BF16)
                acc = jnp.exp(m - m_new) * acc + _dot(p, vx_ref[h, pl.ds(start, tk), :])
                out.append((m_new, acc))
            return tuple(out)

        init = tuple((jnp.full((tq, 1), NEG, F32), jnp.zeros((tq, 2 * dh), F32)) for _ in range(hb))
        carry = lax.fori_loop(0, n_full, lambda j, c: step(j, c, False), init)
        finish([c[1] for c in step(n_full, carry, True)])


def fox_attention(proj, c, q_gain, k_gain, B, T, H, *, tq=512, tk=512, hb=4):
    nq = T // tq
    nk = T // tk
    dh = HEAD_DIM
    hg = H // hb
    c_row = c.transpose(0, 2, 1).reshape(B, H, nk, tk)
    assert hb & (hb - 1) == 0 and 2 * BIAS_PIECES * hb <= LANES
    c_col = c.reshape(B, T, hg, hb).transpose(0, 2, 1, 3)
    c_col = jnp.concatenate([jnp.tile(c_col, (1, 1, 1, 2 * BIAS_PIECES)),
                             jnp.zeros((B, hg, T, LANES - 2 * BIAS_PIECES * hb), F32)], axis=-1)
    bound = jnp.max(jnp.abs(q_gain)) * jnp.max(jnp.abs(k_gain)) * (dh ** 0.5) * 1.01
    fast = (bound <= FAST_SOFTMAX_BOUND).astype(jnp.int32).reshape(1)
    full = lambda off: pl.BlockSpec((1, T, hb * dh), lambda b, h, i, f: (b, 0, off + h))
    tile = lambda off: pl.BlockSpec((1, tq, hb * dh), lambda b, h, i, f: (b, i, off + h))
    vec = pl.BlockSpec((1, dh), lambda b, h, i, f: (0, 0))
    grid_spec = pltpu.PrefetchScalarGridSpec(
        num_scalar_prefetch=1,
        grid=(B, hg, nq),
        in_specs=[
            tile(0), full(hg), full(2 * hg), tile(3 * hg),
            pl.BlockSpec((1, hb, nk, tk), lambda b, h, i, f: (b, h, 0, 0)),
            pl.BlockSpec((1, 1, T, LANES), lambda b, h, i, f: (b, h, 0, 0)),
            pl.BlockSpec((1, 1, tq, LANES), lambda b, h, i, f: (b, h, i, 0)),
            vec, vec, vec,
        ],
        out_specs=pl.BlockSpec((1, tq, hb * dh), lambda b, h, i, f: (b, i, h)),
        scratch_shapes=[pltpu.VMEM((hb, T, 2 * dh), BF16), pltpu.VMEM((hb, T, 2 * dh), BF16)],
    )
    return pl.pallas_call(
        functools.partial(_fox_kernel, tq=tq, tk=tk, hb=hb),
        out_shape=jax.ShapeDtypeStruct((B, T, H * dh), BF16),
        grid_spec=grid_spec,
        compiler_params=_cparams(("arbitrary", "arbitrary", "arbitrary")),
        name="fox_attention",
    )(fast, proj, proj, proj, proj, c_row, c_col, c_col, jnp.full((1, dh), bound, F32),
      q_gain.reshape(1, dh), k_gain.reshape(1, dh))


def _rope(x, cos, sin_lo, sin_hi):
    return (x * cos + pltpu.roll(x, LANES - ROPE_HALF, 1) * sin_lo
            + pltpu.roll(x, ROPE_HALF, 1) * sin_hi)


def _gelu_tanh(x):
    return 0.5 * x * (1.0 + jnp.tanh(0.7978845608028654 * (x + 0.044715 * (x * x * x))))


def _compress_kernel(raw_ref, pos_ref, w1_ref, w2_ref, cos_ref, slo_ref, shi_ref, o_ref, *, n_cmp):
    nb = o_ref.shape[-2]
    half = CMP_LEN // 2
    acc_a = jnp.zeros((nb, HEAD_DIM), F32)
    acc_b = jnp.zeros((nb, HEAD_DIM), F32)
    for l in range(half):
        rl = raw_ref[0, pl.ds(l, nb, stride=CMP_STRIDE), :]
        wa = w1_ref[0, l * HEAD_DIM:(l + 1) * HEAD_DIM, :].astype(BF16)
        wb = w1_ref[0, (half + l) * HEAD_DIM:(half + l + 1) * HEAD_DIM, :].astype(BF16)
        acc_a = acc_a + _dot((rl + pos_ref[0, l:l + 1, :]).astype(BF16), wa)
        acc_b = acc_b + _dot((rl + pos_ref[0, half + l:half + l + 1, :]).astype(BF16), wb)
    pre = acc_a + pltpu.roll(acc_b, nb - 1, 0)
    out = _dot(_gelu_tanh(pre).astype(BF16), w2_ref[0].astype(BF16))
    roped = _rope(out, cos_ref[...], slo_ref[...], shi_ref[...])
    out = jnp.where(pl.program_id(1) == 0, roped, out)
    rows = lax.broadcasted_iota(jnp.int32, out.shape, 0)
    o_ref[0, 0, 0] = jnp.where(rows < n_cmp, out, 0.0)


def compress(qkv, pos, w1, w2, tabs, B, T, kv_block0):
    G = NSA_KV_HEADS
    nb = T // CMP_STRIDE
    n_cmp = (T - CMP_LEN) // CMP_STRIDE + 1
    cos, slo, shi = tabs
    tab = pl.BlockSpec((nb, HEAD_DIM), lambda b, kv, g: (0, 0))
    return pl.pallas_call(
        functools.partial(_compress_kernel, n_cmp=n_cmp),
        out_shape=jax.ShapeDtypeStruct((B, 2, G, nb, HEAD_DIM), F32),
        grid=(B, 2, G),
        in_specs=[
            pl.BlockSpec((1, T, HEAD_DIM), lambda b, kv, g: (b, 0, kv_block0 + kv * G + g)),
            pl.BlockSpec((1, CMP_LEN, HEAD_DIM), lambda b, kv, g: (kv, 0, 0)),
            pl.BlockSpec((1, CMP_LEN * HEAD_DIM, HEAD_DIM), lambda b, kv, g: (kv, 0, 0)),
            pl.BlockSpec((1, HEAD_DIM, HEAD_DIM), lambda b, kv, g: (kv, 0, 0)),
            tab, tab, tab,
        ],
        out_specs=pl.BlockSpec((1, 1, 1, nb, HEAD_DIM), lambda b, kv, g: (b, kv, g, 0, 0)),
        compiler_params=_cparams(("arbitrary", "arbitrary", "arbitrary")),
        name="nsa_compress",
    )(qkv, pos, w1, w2, cos, slo, shi)


SLC_CHUNK = 512
WIN_SPAN = WIN + Q_BLOCK
NSA_QB = 4


def _select_blocks(imp, qi, n_slc):
    QB = Q_BLOCK
    n_rows = -(-n_slc // 8) * 8
    imp_t = jnp.transpose(imp)[0:n_rows]
    blk = lax.broadcasted_iota(jnp.int32, (n_rows, QB), 0)
    t = qi * QB + lax.broadcasted_iota(jnp.int32, (n_rows, QB), 1)
    cur = jnp.right_shift(t, 6)
    forced = (blk == 0) | (blk == cur) | (blk == cur - 1)
    score = jnp.where(forced, BIG, jnp.where(blk * SLC_LEN <= t, imp_t, -BIG))
    score = jnp.where(blk < n_slc, score, -2.0 * BIG)
    cnt = jnp.zeros((n_rows, QB), F32)
    for jp in range(n_slc):
        r = score[jp:jp + 1, :]
        cnt = cnt + ((r > score) | ((r == score) & (jp < blk))).astype(F32)
    sel_t = ((cnt < float(min(N_SLC, n_slc))) & (blk < n_slc)).astype(F32)
    if n_rows < LANES:
        sel_t = jnp.concatenate([sel_t, jnp.zeros((LANES - n_rows, QB), F32)], axis=0)
    return jnp.transpose(sel_t).astype(BF16)


def _nsa_kernel(q_ref, ks_ref, vs_ref, kw_ref, vw_ref, kc_ref, vc_ref, gl_ref,
                cq_ref, sloq_ref, shiq_ref, ck_ref, slok_ref, shik_ref, c2s_ref, ex_ref,
                o_ref, ksx_ref, vsx_ref, kwb_ref, vwx_ref, bias_ref, kmax_ref, acc_ref, *, n_slc, n_cmp):
    step = pl.program_id(2)
    QB, HPG, dh = Q_BLOCK, NSA_HPG, HEAD_DIM
    T = ks_ref.shape[1]

    @pl.when(step == 0)
    def _():
        ks = _rope(ks_ref[0], ck_ref[...], slok_ref[...], shik_ref[...])
        ksx_ref[:, 0:dh] = ks.astype(BF16)
        ksx_ref[:, dh:2 * dh] = _aux_lanes((), False, T).astype(BF16)
        kmax_ref[...] = jnp.full(kmax_ref.shape, jnp.max(jnp.sum(ks * ks, axis=-1, keepdims=True)), F32)
        kwb_ref[...] = _rope(kw_ref[0], ck_ref[...], slok_ref[...], shik_ref[...]).astype(BF16)
        ones = jnp.ones((T, dh), BF16)
        vsx_ref[:, 0:dh] = vs_ref[0].astype(BF16)
        vsx_ref[:, dh:2 * dh] = ones
        vwx_ref[:, 0:dh] = vw_ref[0].astype(BF16)
        vwx_ref[:, dh:2 * dh] = ones

    qis = [step * NSA_QB + b for b in range(NSA_QB)]
    rows = lambda b, h: slice((b * HPG + h) * QB, (b * HPG + h + 1) * QB)
    qrows = lambda b: slice(b * QB, (b + 1) * QB)
    pieces, norms2 = [], []
    for b in range(NSA_QB):
        for h in range(HPG):
            x = q_ref[0, qrows(b), h * dh:(h + 1) * dh]
            x = _rope(x, cq_ref[qrows(b), :], sloq_ref[qrows(b), :], shiq_ref[qrows(b), :]) * (dh ** -0.5)
            pieces.append(x)
            norms2.append(_dot((x * x).astype(BF16), jnp.ones((dh, LANES), BF16)))
    q = jnp.concatenate(pieces, axis=0).astype(BF16)

    lane = lax.broadcasted_iota(jnp.int32, (QB, LANES), 1)
    sub = lax.broadcasted_iota(jnp.int32, (QB, LANES), 0)

    kc = kc_ref[0, 0, 0].astype(BF16)
    vc = vc_ref[0, 0, 0].astype(BF16)
    sc = _dot_nt(q, kc)
    o_cmp, sels = {}, []
    for b in range(NSA_QB):
        t = qis[b] * QB + sub
        mask_c = (lane * CMP_STRIDE + (CMP_LEN - 1) <= t) & (lane < n_cmp)
        mask_cf = mask_c.astype(F32)
        imp_c = jnp.zeros((QB, LANES), F32)
        for h in range(HPG):
            s = jnp.where(mask_c, sc[rows(b, h)], NEG)
            e = jnp.exp(s - jnp.max(s, axis=-1, keepdims=True)) * mask_cf
            p = e / jnp.maximum(jnp.sum(e, axis=-1, keepdims=True), TINY)
            imp_c = imp_c + p
            o_cmp[b, h] = _dot(p.astype(BF16), vc)
        imp = jnp.dot(imp_c, c2s_ref[...], preferred_element_type=F32, precision=lax.Precision.HIGHEST)
        sels.append(_select_blocks(imp, qis[b], n_slc))

    sel_all = jnp.concatenate(sels, axis=0)
    for c in range(T // SLC_CHUNK):
        selx = _dot(sel_all, ex_ref[:, c * SLC_CHUNK:(c + 1) * SLC_CHUNK])
        kpos = c * SLC_CHUNK + lax.broadcasted_iota(jnp.int32, (QB, SLC_CHUNK), 1)
        for b in range(NSA_QB):
            tq_ = qis[b] * QB + lax.broadcasted_iota(jnp.int32, (QB, SLC_CHUNK), 0)
            bias_ref[b, c] = jnp.where((selx[qrows(b)] > 0.5) & (kpos <= tq_), 0.0, NEG)

    o_win = {}
    for b in range(NSA_QB):
        wstart = pl.multiple_of(jnp.clip(qis[b] * QB - WIN, 0, T - WIN_SPAN), Q_BLOCK)
        qb = q[b * HPG * QB:(b + 1) * HPG * QB]
        sw = _dot_nt(qb, kwb_ref[pl.ds(wstart, WIN_SPAN), :])
        kpos = wstart + lax.broadcasted_iota(jnp.int32, (QB, WIN_SPAN), 1)
        tw = qis[b] * QB + lax.broadcasted_iota(jnp.int32, (QB, WIN_SPAN), 0)
        bias_w = jnp.where((kpos <= tw) & (kpos > tw - WIN), 0.0, NEG)
        pw = []
        for h in range(HPG):
            sh = sw[h * QB:(h + 1) * QB] + bias_w
            pw.append(jnp.exp(sh - jnp.max(sh, axis=-1, keepdims=True)).astype(BF16))
        ow = _dot(jnp.concatenate(pw, axis=0), vwx_ref[pl.ds(wstart, WIN_SPAN), :])
        for h in range(HPG):
            acc_w = ow[h * QB:(h + 1) * QB]
            o_win[b, h] = acc_w[:, 0:dh] / jnp.maximum(acc_w[:, dh:2 * dh], TINY)

    chains = [(b, h) for b in range(NSA_QB) for h in range(HPG)]

    n_chunks = (qis[-1] * QB + QB + SLC_CHUNK - 1) // SLC_CHUNK
    kmax2 = kmax_ref[0:1, 0:1]
    bounds = [jnp.sqrt(n2 * kmax2) * 1.02 for n2 in norms2]
    worst = functools.reduce(jnp.maximum, [jnp.max(bd) for bd in bounds])

    @pl.when(worst <= FAST_SOFTMAX_BOUND)
    def _():
        qx = jnp.concatenate(
            [jnp.concatenate([(x * LOG2E).astype(BF16),
                              _aux_lanes(_split3(-LOG2E * bd), True, QB).astype(BF16)], axis=1)
             for x, bd in zip(pieces, bounds)], axis=0)

        def body(c, accs):
            start = pl.multiple_of(c * SLC_CHUNK, SLC_CHUNK)
            s = _dot_nt(qx, ksx_ref[pl.ds(start, SLC_CHUNK), :])
            ps = [jnp.exp2(s[rows(b, h)] + bias_ref[b, c]).astype(BF16) for (b, h) in chains]
            pv = _dot(jnp.concatenate(ps, axis=0), vsx_ref[pl.ds(start, SLC_CHUNK), :])
            return tuple(accs[n] + pv[rows(b, h)] for n, (b, h) in enumerate(chains))

        accs = lax.fori_loop(0, n_chunks, body, tuple(jnp.zeros((QB, 2 * dh), F32) for _ in chains))
        for n, (b, h) in enumerate(chains):
            acc_ref[rows(b, h), :] = accs[n]

    @pl.when(worst > FAST_SOFTMAX_BOUND)
    def _():
        def body(c, carry):
            start = pl.multiple_of(c * SLC_CHUNK, SLC_CHUNK)
            s = _dot_nt(q, ksx_ref[pl.ds(start, SLC_CHUNK), 0:dh])
            ms, alphas, ps = [], [], []
            for n, (b, h) in enumerate(chains):
                sh = s[rows(b, h)] + bias_ref[b, c]
                m_new = jnp.maximum(carry[n][0], jnp.max(sh, axis=-1, keepdims=True))
                alphas.append(jnp.exp(carry[n][0] - m_new))
                ps.append(jnp.exp(sh - m_new).astype(BF16))
                ms.append(m_new)
            pv = _dot(jnp.concatenate(ps, axis=0), vsx_ref[pl.ds(start, SLC_CHUNK), :])
            return tuple((ms[n], alphas[n] * carry[n][1] + pv[rows(b, h)]) for n, (b, h) in enumerate(chains))

        init = tuple((jnp.full((QB, 1), NEG, F32), jnp.zeros((QB, 2 * dh), F32)) for _ in chains)
        slc = lax.fori_loop(0, n_chunks, body, init)
        for n, (b, h) in enumerate(chains):
            acc_ref[rows(b, h), :] = slc[n][1]

    for b in range(NSA_QB):
        gates = _sigmoid(gl_ref[0, qrows(b), :])
        for h in range(HPG):
            acc_s = acc_ref[rows(b, h), :]
            o_s = acc_s[:, 0:dh] / jnp.maximum(acc_s[:, dh:2 * dh], TINY)
            out = (gates[:, 3 * h:3 * h + 1] * o_cmp[b, h] + gates[:, 3 * h + 1:3 * h + 2] * o_s
                   + gates[:, 3 * h + 2:3 * h + 3] * o_win[b, h])
            o_ref[0, qrows(b), h * dh:(h + 1) * dh] = out.astype(o_ref.dtype)


def nsa_attention(qkv, cmp_kv, gl, tabs_q, B, T):
    G, HPG, dh = NSA_KV_HEADS, NSA_HPG, HEAD_DIM
    tq = NSA_QB * Q_BLOCK
    nb = T // CMP_STRIDE
    n_cmp = (T - CMP_LEN) // CMP_STRIDE + 1
    n_slc = T // SLC_LEN
    assert nb == LANES and n_slc <= LANES and T % SLC_CHUNK == 0 and T >= WIN_SPAN and T % tq == 0
    kvb = (HPG * G)

    c_start = np.arange(nb) * CMP_STRIDE
    s_start = np.arange(LANES) * SLC_LEN
    c2s = ((c_start[:, None] < s_start[None, :] + SLC_LEN) & (c_start[:, None] + CMP_LEN > s_start[None, :])
           & (np.arange(nb)[:, None] < n_cmp) & (np.arange(LANES)[None, :] < n_slc)).astype(np.float32)
    expand = (np.arange(T)[None, :] // SLC_LEN == np.arange(LANES)[:, None]).astype(np.float32)

    cos, slo, shi = tabs_q
    kv_full = lambda blk: pl.BlockSpec((1, T, dh), lambda b, g, i: (b, 0, kvb + blk + g))
    qtab = pl.BlockSpec((tq, dh), lambda b, g, i: (i, 0))
    ktab = pl.BlockSpec((T, dh), lambda b, g, i: (0, 0))
    return pl.pallas_call(
        functools.partial(_nsa_kernel, n_slc=n_slc, n_cmp=n_cmp),
        out_shape=jax.ShapeDtypeStruct((B, T, G * HPG * dh), BF16),
        grid=(B, G, T // tq),
        in_specs=[
            pl.BlockSpec((1, tq, HPG * dh), lambda b, g, i: (b, i, g)),
            kv_full(2 * G), kv_full(3 * G), kv_full(4 * G), kv_full(5 * G),
            pl.BlockSpec((1, 1, 1, nb, dh), lambda b, g, i: (b, 0, g, 0, 0)),
            pl.BlockSpec((1, 1, 1, nb, dh), lambda b, g, i: (b, 1, g, 0, 0)),
            pl.BlockSpec((1, tq, LANES), lambda b, g, i: (b, i, g)),
            qtab, qtab, qtab, ktab, ktab, ktab,
            pl.BlockSpec((nb, LANES), lambda b, g, i: (0, 0)),
            pl.BlockSpec((LANES, T), lambda b, g, i: (0, 0)),
        ],
        out_specs=pl.BlockSpec((1, tq, HPG * dh), lambda b, g, i: (b, i, g)),
        scratch_shapes=[pltpu.VMEM((T, 2 * dh), BF16), pltpu.VMEM((T, 2 * dh), BF16),
                        pltpu.VMEM((T, dh), BF16), pltpu.VMEM((T, 2 * dh), BF16),
                        pltpu.VMEM((NSA_QB, T // SLC_CHUNK, Q_BLOCK, SLC_CHUNK), F32),
                        pltpu.VMEM((8, LANES), F32),
                        pltpu.VMEM((NSA_QB * HPG * Q_BLOCK, 2 * dh), F32)],
        compiler_params=_cparams(("arbitrary", "arbitrary", "arbitrary")),
        name="nsa_attention",
    )(qkv, qkv, qkv, qkv, qkv, cmp_kv, cmp_kv, gl, cos, slo, shi, cos, slo, shi,
      jnp.asarray(c2s), jnp.asarray(expand, dtype=BF16))


CONV_PAD = 32
CONV_CHUNK = 256


def _conv_kernel(a_ref, b_ref, w_ref, cb_ref, g_ref, gb_ref, o_ref, u_ref):
    T = a_ref.shape[1]
    u_ref[0:CONV_PAD, :] = jnp.zeros((CONV_PAD, LANES), F32)
    u_ref[CONV_PAD:CONV_PAD + T, :] = a_ref[0].astype(F32) * _sigmoid(b_ref[0].astype(F32))
    base = CONV_PAD - (CONV_KERNEL - 1)
    for c in range(T // CONV_CHUNK):
        t0 = c * CONV_CHUNK
        acc = jnp.zeros((CONV_CHUNK, LANES), F32)
        for k in range(CONV_KERNEL):
            acc = acc + u_ref[t0 + base + k:t0 + base + k + CONV_CHUNK, :] * w_ref[k:k + 1, :]
        acc = acc + cb_ref[...]
        mu = jnp.mean(acc, axis=-1, keepdims=True)
        d = acc - mu
        var = jnp.mean(d * d, axis=-1, keepdims=True)
        y = d * lax.rsqrt(var + LN_EPS) * g_ref[...] + gb_ref[...]
        o_ref[0, t0:t0 + CONV_CHUNK, :] = (y * _sigmoid(y)).astype(o_ref.dtype)


def conv_module(glu, conv_w, conv_b, gn_g, gn_b, B, T):
    C = glu.shape[-1] // 2
    ng = C // LANES
    assert C // CONV_GROUPS == LANES
    vec = pl.BlockSpec((1, LANES), lambda b, g: (0, g))
    return pl.pallas_call(
        _conv_kernel,
        out_shape=jax.ShapeDtypeStruct((B, T, C), BF16),
        grid=(B, ng),
        in_specs=[
            pl.BlockSpec((1, T, LANES), lambda b, g: (b, 0, g)),
            pl.BlockSpec((1, T, LANES), lambda b, g: (b, 0, ng + g)),
            pl.BlockSpec((CONV_KERNEL, LANES), lambda b, g: (0, g)),
            vec, vec, vec,
        ],
        out_specs=pl.BlockSpec((1, T, LANES), lambda b, g: (b, 0, g)),
        scratch_shapes=[pltpu.VMEM((CONV_PAD + T, LANES), F32)],
        compiler_params=_cparams(("arbitrary", "arbitrary")),
        name="conformer_conv",
    )(glu, glu, conv_w, conv_b.reshape(1, C), gn_g.reshape(1, C), gn_b.reshape(1, C))


def _split2(a):
    hi = a.astype(BF16)
    return hi, (a - hi.astype(F32)).astype(BF16)


def _router_kernel(x_ref, w_ref, bias_ref, idx_ref, wts_ref):
    xh, xl = _split2(x_ref[...])
    wh, wl = _split2(w_ref[...])
    logits = _dot(xh, wh) + (_dot(xh, wl) + _dot(xl, wh))
    aff = _sigmoid(jnp.concatenate(
        [jnp.transpose(logits[c * LANES:(c + 1) * LANES])[0:N_EXPERTS] for c in range(logits.shape[0] // LANES)],
        axis=1))
    sel = aff + bias_ref[...]
    a = [aff[e:e + 1, :] for e in range(N_EXPERTS)]
    s = [sel[e:e + 1, :] for e in range(N_EXPERTS)]
    P = EXPERTS_PER_GROUP
    grp = []
    for g in range(N_EXPERT_GROUPS):
        v = s[g * P:(g + 1) * P]
        best = None
        for i in range(P):
            for j in range(i + 1, P):
                pair = v[i] + v[j]
                best = pair if best is None else jnp.maximum(best, pair)
        grp.append(best)
    gbest = jnp.zeros_like(grp[0], dtype=jnp.int32)
    gval = grp[0]
    for g in range(1, N_EXPERT_GROUPS):
        better = grp[g] > gval
        gbest = jnp.where(better, g, gbest)
        gval = jnp.where(better, grp[g], gval)
    cs, ca = [], []
    for p in range(P):
        sv, av = s[p], a[p]
        for g in range(1, N_EXPERT_GROUPS):
            sv = jnp.where(gbest == g, s[g * P + p], sv)
            av = jnp.where(gbest == g, a[g * P + p], av)
        cs.append(sv)
        ca.append(av)
    i1 = jnp.zeros_like(gbest)
    v1, a1 = cs[0], ca[0]
    for p in range(1, P):
        better = cs[p] > v1
        i1 = jnp.where(better, p, i1)
        v1 = jnp.where(better, cs[p], v1)
        a1 = jnp.where(better, ca[p], a1)
    i2 = jnp.full_like(gbest, -1)
    v2 = jnp.full_like(v1, -jnp.inf)
    a2 = jnp.zeros_like(a1)
    for p in range(P):
        better = (i1 != p) & (cs[p] > v2)
        i2 = jnp.where(better, p, i2)
        v2 = jnp.where(better, cs[p], v2)
        a2 = jnp.where(better, ca[p], a2)
    den = a1 + a2
    idx_ref[...] = jnp.zeros(idx_ref.shape, jnp.int32)
    wts_ref[...] = jnp.zeros(wts_ref.shape, F32)
    idx_ref[0:1, :] = gbest * P + i1
    idx_ref[1:2, :] = gbest * P + i2
    wts_ref[0:1, :] = a1 / den
    wts_ref[1:2, :] = a2 / den


def router(x, router_w, router_bias, *, tm=1024):
    N, D = x.shape
    E = N_EXPERTS
    return pl.pallas_call(
        _router_kernel,
        out_shape=(jax.ShapeDtypeStruct((8, N), jnp.int32), jax.ShapeDtypeStruct((8, N), F32)),
        grid=(N // tm,),
        in_specs=[
            pl.BlockSpec((tm, D), lambda i: (i, 0)),
            pl.BlockSpec((D, LANES), lambda i: (0, 0)),
            pl.BlockSpec((E, 1), lambda i: (0, 0)),
        ],
        out_specs=(pl.BlockSpec((8, tm), lambda i: (0, i)), pl.BlockSpec((8, tm), lambda i: (0, i))),
        compiler_params=_cparams(("arbitrary",)),
        name="moe_router",
    )(x, jnp.pad(router_w, ((0, 0), (0, LANES - E))), router_bias.reshape(E, 1))


PLAN_CHUNK = 512


def _plan_kernel(idx_ref, dest_ref, cnt_ref, *, n_tok):
    E = N_EXPERTS
    sub = lax.broadcasted_iota(jnp.int32, (TOP_K * E, n_tok), 0)
    tgt = jnp.where(sub < E, idx_ref[0:1, :], idx_ref[1:2, :])
    onehot = ((sub & (E - 1)) == tgt).astype(F32)
    onehot_b = onehot.astype(BF16)
    tri = (lax.broadcasted_iota(jnp.int32, (PLAN_CHUNK, PLAN_CHUNK), 0)
           <= lax.broadcasted_iota(jnp.int32, (PLAN_CHUNK, PLAN_CHUNK), 1)).astype(F32).astype(BF16)
    carry = jnp.zeros((TOP_K * E, 1), F32)
    parts = []
    for c in range(n_tok // PLAN_CHUNK):
        pre = _dot(onehot_b[:, c * PLAN_CHUNK:(c + 1) * PLAN_CHUNK], tri) + carry
        parts.append(pre)
        carry = pre[:, PLAN_CHUNK - 1:PLAN_CHUNK]
    excl = jnp.concatenate(parts, axis=1) - onehot
    cnt0 = carry[0:E]
    tot = cnt0 + carry[E:2 * E]
    lower = (lax.broadcasted_iota(jnp.int32, (E, E), 1)
             < lax.broadcasted_iota(jnp.int32, (E, E), 0)).astype(F32)
    offs = jnp.dot(lower, jnp.broadcast_to(tot, (E, LANES)), preferred_element_type=F32,
                   precision=lax.Precision.HIGHEST)[:, 0:1]
    base = jnp.concatenate([offs, offs + cnt0], axis=0)
    val = onehot * (base + excl)
    dest_ref[...] = jnp.zeros(dest_ref.shape, jnp.int32)
    dest_ref[0:1, :] = jnp.sum(val[0:E], axis=0, keepdims=True).astype(jnp.int32)
    dest_ref[1:2, :] = jnp.sum(val[E:2 * E], axis=0, keepdims=True).astype(jnp.int32)
    cnt_ref[...] = jnp.broadcast_to(tot, (E, LANES)).astype(jnp.int32)


def moe_plan(idx):
    n_tok = idx.shape[1]
    assert n_tok % PLAN_CHUNK == 0 and TOP_K == 2
    return pl.pallas_call(
        functools.partial(_plan_kernel, n_tok=n_tok),
        out_shape=(jax.ShapeDtypeStruct((8, n_tok), jnp.int32),
                   jax.ShapeDtypeStruct((N_EXPERTS, LANES), jnp.int32)),
        compiler_params=_cparams(None),
        name="moe_plan",
    )(idx)


def _scatter_kernel(dest_ref, x_ref, xs_hbm, sem, *, tm, n_tok):
    base = pl.program_id(0) * tm

    def copies(start):
        def make(r, k, d):
            cp = pltpu.make_async_copy(x_ref.at[pl.ds(r, 1)], xs_hbm.at[pl.ds(d, 1)], sem)
            if start:
                cp.start(priority=k)
            else:
                cp.wait()
        _row_copies(dest_ref, n_tok, base, tm, make)

    copies(True)
    copies(False)


def moe_scatter(x, dest, *, tm=256):
    N, D = x.shape
    grid_spec = pltpu.PrefetchScalarGridSpec(
        num_scalar_prefetch=1,
        grid=(N // tm,),
        in_specs=[pl.BlockSpec((tm, D), lambda i, d: (i, 0))],
        out_specs=pl.BlockSpec(memory_space=pl.ANY),
        scratch_shapes=[pltpu.SemaphoreType.DMA(())],
    )
    return pl.pallas_call(
        functools.partial(_scatter_kernel, tm=tm, n_tok=N),
        out_shape=jax.ShapeDtypeStruct((TOP_K * N, D), x.dtype),
        grid_spec=grid_spec,
        compiler_params=_cparams(("arbitrary",)),
        name="moe_scatter",
    )(dest, x)


def _moe_kernel(ti_ref, te_ref, lo_ref, hi_ref, first_ref, head_ref, slot_ref, nxt_ref,
                xs_ref, wg_hbm, wu_hbm, wd_hbm, y_ref,
                wg32_ref, wu32_ref, wd32_ref, wgb_ref, wub_ref, wdb_ref, sem, *, layer):
    i = pl.program_id(0)

    def weight_copies(e, slot):
        return [pltpu.make_async_copy(src.at[layer, e], dst.at[slot], sem.at[slot, n])
                for n, (src, dst) in enumerate(((wg_hbm, wg32_ref), (wu_hbm, wu32_ref), (wd_hbm, wd32_ref)))]

    @pl.when(i == 0)
    def _():
        for cp in weight_copies(te_ref[0], 0):
            cp.start()

    @pl.when(head_ref[i] == 1)
    def _():
        slot = slot_ref[i]
        for cp in weight_copies(te_ref[i], slot):
            cp.wait()
        wgb_ref[...] = wg32_ref[slot].astype(BF16)
        wub_ref[...] = wu32_ref[slot].astype(BF16)
        wdb_ref[...] = wd32_ref[slot].astype(BF16)

        @pl.when(nxt_ref[i] >= 0)
        def _():
            for cp in weight_copies(nxt_ref[i], 1 - slot):
                cp.start()

    lo, hi = lo_ref[i], hi_ref[i]

    @pl.when(hi > lo)
    def _():
        x = xs_ref[...].astype(BF16)
        g = _dot(x, wgb_ref[...])
        u = _dot(x, wub_ref[...])
        rows = lax.broadcasted_iota(jnp.int32, (x.shape[0], 1), 0)
        h = jnp.where((rows >= lo) & (rows < hi), (g * _sigmoid(g)) * u, 0.0)
        y = _dot(h.astype(BF16), wdb_ref[...])

        @pl.when(first_ref[i] == 1)
        def _():
            y_ref[...] = y

        @pl.when(first_ref[i] == 0)
        def _():
            y_ref[...] += y


def moe_experts(xs, items, w_gate, w_up, w_down, layer, *, tm=MOE_TM):
    P, D = xs.shape
    Fh = w_gate.shape[-1]
    n_items = items[0].shape[0]
    xmap = lambda i, ti, *_: (ti[i], 0)
    hbm = pl.BlockSpec(memory_space=pl.ANY)
    grid_spec = pltpu.PrefetchScalarGridSpec(
        num_scalar_prefetch=len(items),
        grid=(n_items,),
        in_specs=[pl.BlockSpec((tm, D), xmap), hbm, hbm, hbm],
        out_specs=pl.BlockSpec((tm, D), xmap),
        scratch_shapes=[pltpu.VMEM((2, D, Fh), F32), pltpu.VMEM((2, D, Fh), F32), pltpu.VMEM((2, Fh, D), F32),
                        pltpu.VMEM((D, Fh), BF16), pltpu.VMEM((D, Fh), BF16), pltpu.VMEM((Fh, D), BF16),
                        pltpu.SemaphoreType.DMA((2, 3))],
    )
    return pl.pallas_call(
        functools.partial(_moe_kernel, layer=layer),
        out_shape=jax.ShapeDtypeStruct((P, D), F32),
        grid_spec=grid_spec,
        compiler_params=_cparams(("arbitrary",)),
        name="moe_experts",
    )(*items, xs, w_gate, w_up, w_down)


def moe_items(tot, n_rows, tm=MOE_TM):
    E = N_EXPERTS
    n_max = n_rows // tm + E
    ar = jnp.arange(E, dtype=jnp.int32)
    ends = jnp.cumsum(tot)
    offs = ends - tot
    first_tile = offs // tm
    n_e = jnp.where(tot > 0, (ends - 1) // tm - first_tile + 1, 0)
    s_end = jnp.cumsum(n_e)
    s_beg = s_end - n_e
    n_items = s_end[-1]
    i = jnp.arange(n_max, dtype=jnp.int32)
    ic = jnp.minimum(i, n_items - 1)
    e_i = jnp.sum((ic[:, None] >= s_end[None, :]).astype(jnp.int32), axis=1)
    pick = (e_i[:, None] == ar[None, :]).astype(jnp.int32)
    at = lambda v: jnp.sum(pick * v[None, :], axis=1)
    tile = at(first_tile) + ic - at(s_beg)
    live = i < n_items
    lo = jnp.where(live, jnp.maximum(at(offs), tile * tm) - tile * tm, 0)
    hi = jnp.where(live, jnp.minimum(at(ends), tile * tm + tm) - tile * tm, 0)
    prev_tile = jnp.concatenate([jnp.full((1,), -1, jnp.int32), tile[:-1]])
    first = live & (tile != prev_tile)
    head = live & (ic == at(s_beg))
    used = (tot > 0).astype(jnp.int32)
    slot = at(jnp.cumsum(used) - used) & 1
    later = jnp.where((ar[None, :] > ar[:, None]) & (tot[None, :] > 0), ar[None, :], E)
    nxt_e = jnp.min(later, axis=1)
    nxt = at(jnp.where(nxt_e < E, nxt_e, -1))
    return tuple(v.astype(jnp.int32) for v in (tile, e_i, lo, hi, first, head, slot, nxt))


def _rope_tables(pos):
    inv = ROPE_THETA ** (-jnp.arange(0, ROPE_DIM, 2, dtype=F32) / ROPE_DIM)
    ang = pos.astype(F32)[:, None] * inv[None, :]
    cos, sin = jnp.cos(ang), jnp.sin(ang)
    n = pos.shape[0]
    rest = HEAD_DIM - ROPE_DIM
    cos_t = jnp.concatenate([cos, cos, jnp.ones((n, rest), F32)], axis=1)
    sin_lo = jnp.concatenate([-sin, jnp.zeros((n, HEAD_DIM - ROPE_HALF), F32)], axis=1)
    sin_hi = jnp.concatenate([jnp.zeros((n, ROPE_HALF), F32), sin, jnp.zeros((n, rest), F32)], axis=1)
    return cos_t, sin_lo, sin_hi


def _even_mixer(xb, B, T, w_in_all, j, pos, w1, w2, conv_w, conv_b, gn_g, gn_b):
    N, D = xb.shape
    G, HPG, dh = NSA_KV_HEADS, NSA_HPG, HEAD_DIM
    nsa_w = G * HPG * dh
    kv_cols = 3 * 2 * G * dh
    n_gate = 3 * G * HPG
    conv_c = (w_in_all.shape[2] - nsa_w - kv_cols - n_gate) // 2
    qkv = linear(xb, w_in_all, nsa_w + kv_cols, F32, layer=j, tm=512, tn=(nsa_w + kv_cols) // 2, name="even_qkv")
    w_gl = w_in_all[j, :, nsa_w + kv_cols:nsa_w + kv_cols + n_gate].reshape(D, G, 3 * HPG)
    w_gl = jnp.pad(w_gl, ((0, 0), (0, 0), (0, LANES - 3 * HPG))).reshape(D, G * LANES)
    gl = linear(xb, w_gl, G * LANES, F32, tn=G * LANES, name="even_gates")
    glu = linear(xb, w_in_all[j, :, nsa_w + kv_cols + n_gate:], 2 * conv_c, BF16, tn=1024, name="even_glu")

    qkv3 = qkv.reshape(B, T, nsa_w + kv_cols)
    n_cmp = (T - CMP_LEN) // CMP_STRIDE + 1
    nb = T // CMP_STRIDE
    cmp_end = jnp.arange(nb) * CMP_STRIDE + (CMP_LEN - 1)
    cmp_kv = compress(qkv3, pos, w1, w2, _rope_tables(cmp_end), B, T, G * HPG)
    o_nsa = nsa_attention(qkv3, cmp_kv, gl.reshape(B, T, G * LANES), _rope_tables(jnp.arange(T)), B, T)
    u = conv_module(glu.reshape(B, T, 2 * conv_c), conv_w, conv_b, gn_g, gn_b, B, T)
    return o_nsa.reshape(N, nsa_w), u.reshape(N, conv_c)


def _fox_mixer(xb, B, T, w_in_all, j, f_bias, q_gain, k_gain):
    N, D = xb.shape
    H = D // HEAD_DIM
    proj = linear(xb, w_in_all, 4 * D, BF16, layer=j, tm=512, tn=2048, name="fox_qkvg")
    w_f = jnp.pad(w_in_all[j, :, 4 * D:], ((0, 0), (0, LANES - H)))
    fl = linear(xb, w_f, LANES, F32, tn=LANES, name="fox_forget")
    c = fox_decay(fl, jnp.pad(f_bias, (0, LANES - H)), B, T)[:, :H].reshape(B, T, H)
    return fox_attention(proj.reshape(B, T, 4 * D), c, q_gain, k_gain, B, T, H).reshape(N, D)


def _moe(xf, router_w, router_bias, w_gate, w_up, w_down, layer):
    N, D = xf.shape
    idx, wts = router(xf, router_w, router_bias)
    dest8, cnt = moe_plan(idx)
    dest = dest8[:TOP_K].reshape(-1)
    xs = moe_scatter(xf, dest)
    y = moe_experts(xs, moe_items(cnt[:, 0], TOP_K * N), w_gate, w_up, w_down, layer)
    return y, dest, wts[:TOP_K].T


def kernel(x, even_w_in, even_w_out, nsa_cmp_pos, nsa_cmp_w1, nsa_cmp_w2, conv_w, conv_b, conv_gn_g, conv_gn_b, fox_w_in, fox_f_bias, fox_q_gain, fox_k_gain, fox_w_out, ln_mix_g, ln_mix_b, ln_ffn_g, ln_ffn_b, router_w, router_bias, exp_w_gate, exp_w_up, exp_w_down):
    B, T, D = x.shape
    depth = ln_mix_g.shape[0]
    alpha = (2.0 * depth) ** 0.25
    N = B * T
    xf = x.reshape(N, D)
    xb = xf
    for layer in range(depth):
        j = layer // 2
        if layer % 2 == 0:
            a1, a2 = _even_mixer(xb, B, T, even_w_in, j, nsa_cmp_pos[j], nsa_cmp_w1[j], nsa_cmp_w2[j],
                                 conv_w[j], conv_b[j], conv_gn_g[j], conv_gn_b[j])
            xf, xb = outproj_ln(a1, a2, 0, even_w_out, j, xf, ln_mix_g[layer], ln_mix_b[layer], alpha)
        else:
            a = _fox_mixer(xb, B, T, fox_w_in, j, fox_f_bias[j], fox_q_gain[j], fox_k_gain[j])
            xf, xb = outproj_ln(a, a, 1, fox_w_out, j, xf, ln_mix_g[layer], ln_mix_b[layer], alpha)
        y, dest, wts = _moe(xf, router_w, router_bias, exp_w_gate, exp_w_up, exp_w_down, layer)
        xf, xb = combine_ln(xf, y, dest, wts, ln_ffn_g[layer], ln_ffn_b[layer], alpha)
    return xf.reshape(B, T, D)
```

```python
import functools

import numpy as np
import jax
import jax.numpy as jnp
from jax import lax
from jax.experimental import pallas as pl
from jax.experimental.pallas import tpu as pltpu

F32 = jnp.float32
BF16 = jnp.bfloat16

HEAD_DIM = 128
ROPE_THETA = 500000.0
ROPE_DIM = HEAD_DIM // 4
ROPE_HALF = ROPE_DIM // 2
Q_BLOCK = 128

NSA_KV_HEADS = 2
NSA_HPG = 4
CMP_LEN = 32
CMP_STRIDE = 16
SLC_LEN = 64
N_SLC = 8
WIN = 512
CONV_KERNEL = 31
CONV_GROUPS = 8

N_EXPERTS = 16
N_EXPERT_GROUPS = 4
EXPERTS_PER_GROUP = 4
TOP_K = 2

LN_EPS = 1e-5
NEG = -1e30
BIG = 1e9
TINY = 1e-30

LANES = 128
V7X_VMEM_BYTES = 64 * 1024 * 1024
VMEM_LIMIT = V7X_VMEM_BYTES * 7 // 8

MOE_TM = 256


def _cparams(sem, vmem=VMEM_LIMIT):
    return pltpu.CompilerParams(dimension_semantics=sem, vmem_limit_bytes=vmem)


def _dot(a, b):
    return jnp.dot(a, b, preferred_element_type=F32)


def _dot_nt(a, b):
    return lax.dot_general(a, b, (((1,), (1,)), ((), ())), preferred_element_type=F32)


def _sigmoid(x):
    return 1.0 / (1.0 + jnp.exp(-x))


def _linear_kernel(x_ref, w_ref, o_ref, wb_ref):
    @pl.when(pl.program_id(1) == 0)
    def _():
        wb_ref[...] = w_ref[...].astype(BF16)

    o_ref[...] = _dot(x_ref[...].astype(BF16), wb_ref[...]).astype(o_ref.dtype)


def linear(x, w, n_cols, out_dtype, *, layer=None, tm=1024, tn=512, name="linear"):
    M, K = x.shape
    tm = min(tm, M)
    tn = min(tn, n_cols)
    assert M % tm == 0 and n_cols % tn == 0
    row_block = 0
    if layer is not None:
        w = w.reshape(-1, w.shape[-1])
        row_block = layer
    w_spec = pl.BlockSpec((K, tn), lambda j, i: (row_block, j))
    return pl.pallas_call(
        _linear_kernel,
        out_shape=jax.ShapeDtypeStruct((M, n_cols), out_dtype),
        grid=(n_cols // tn, M // tm),
        in_specs=[
            pl.BlockSpec((tm, K), lambda j, i: (i, 0)),
            w_spec,
        ],
        out_specs=pl.BlockSpec((tm, tn), lambda j, i: (i, j)),
        scratch_shapes=[pltpu.VMEM((K, tn), BF16)],
        compiler_params=_cparams(("arbitrary", "arbitrary")),
        name=name,
    )(x, w)


def _layer_norm_rows(y, g, b):
    mu = jnp.mean(y, axis=-1, keepdims=True)
    d = y - mu
    var = jnp.mean(d * d, axis=-1, keepdims=True)
    return d * lax.rsqrt(var + LN_EPS) * g + b


def _outproj_ln_kernel(a1_ref, a2_ref, w_ref, x_ref, g_ref, b_ref, xo_ref, xb_ref, wb_ref, *, alpha):
    @pl.when(pl.program_id(0) == 0)
    def _():
        wb_ref[...] = w_ref[...].astype(BF16)

    half = a1_ref.shape[1]
    h = _dot(a1_ref[...].astype(BF16), wb_ref[0:half, :])
    h = h + _dot(a2_ref[...].astype(BF16), wb_ref[half:2 * half, :])
    out = _layer_norm_rows(alpha * x_ref[...] + h, g_ref[...], b_ref[...])
    xo_ref[...] = out
    xb_ref[...] = out.astype(BF16)


def outproj_ln(a1, a2, a2_col_block, w, layer, x, g, b, alpha, *, tm=512):
    M, D = x.shape
    half = D // 2
    return pl.pallas_call(
        functools.partial(_outproj_ln_kernel, alpha=alpha),
        out_shape=(jax.ShapeDtypeStruct((M, D), F32), jax.ShapeDtypeStruct((M, D), BF16)),
        grid=(M // tm,),
        in_specs=[
            pl.BlockSpec((tm, half), lambda i: (i, 0)),
            pl.BlockSpec((tm, half), lambda i: (i, a2_col_block)),
            pl.BlockSpec((None, D, D), lambda i: (layer, 0, 0), pipeline_mode=pl.Buffered(1)),
            pl.BlockSpec((tm, D), lambda i: (i, 0)),
            pl.BlockSpec((1, D), lambda i: (0, 0)),
            pl.BlockSpec((1, D), lambda i: (0, 0)),
        ],
        out_specs=(pl.BlockSpec((tm, D), lambda i: (i, 0)), pl.BlockSpec((tm, D), lambda i: (i, 0))),
        scratch_shapes=[pltpu.VMEM((D, D), BF16)],
        compiler_params=_cparams(("arbitrary",)),
        name="outproj_ln",
    )(a1, a2, w, x, g.reshape(1, D), b.reshape(1, D))


ROW_DMA_UNROLL = 8


def _row_copies(dest_ref, n_tok, base, tm, make):
    def body(g, carry):
        r0 = pl.multiple_of(g * ROW_DMA_UNROLL, ROW_DMA_UNROLL)
        for u in range(ROW_DMA_UNROLL):
            for k in range(TOP_K):
                make(r0 + u, k, dest_ref[k * n_tok + base + r0 + u])
        return carry

    lax.fori_loop(0, tm // ROW_DMA_UNROLL, body, 0)


def _combine_ln_kernel(dest_ref, x_ref, w_ref, g_ref, b_ref, y_hbm, xo_ref, xb_ref, buf_ref, sem, *,
                       alpha, tm, n_tok):
    i = pl.program_id(0)
    n_steps = pl.num_programs(0)

    def copies(step, slot, start):
        def make(r, k, d):
            cp = pltpu.make_async_copy(y_hbm.at[pl.ds(d, 1)], buf_ref.at[slot, k, pl.ds(r, 1)], sem.at[slot])
            if start:
                cp.start(priority=k)
            else:
                cp.wait()
        _row_copies(dest_ref, n_tok, step * tm, tm, make)

    @pl.when(i == 0)
    def _():
        copies(0, 0, True)

    @pl.when(i + 1 < n_steps)
    def _():
        copies(i + 1, (i + 1) % 2, True)

    slot = i % 2
    copies(i, slot, False)
    f = w_ref[:, 0:1] * buf_ref[slot, 0] + w_ref[:, 1:2] * buf_ref[slot, 1]
    out = _layer_norm_rows(alpha * x_ref[...] + f, g_ref[...], b_ref[...])
    xo_ref[...] = out
    xb_ref[...] = out.astype(BF16)


def combine_ln(x, y, dest, wts, g, b, alpha, *, tm=256):
    N, D = x.shape
    row = lambda i, d: (i, 0)
    grid_spec = pltpu.PrefetchScalarGridSpec(
        num_scalar_prefetch=1,
        grid=(N // tm,),
        in_specs=[
            pl.BlockSpec((tm, D), row),
            pl.BlockSpec((tm, TOP_K), row),
            pl.BlockSpec((1, D), lambda i, d: (0, 0)),
            pl.BlockSpec((1, D), lambda i, d: (0, 0)),
            pl.BlockSpec(memory_space=pl.ANY),
        ],
        out_specs=(pl.BlockSpec((tm, D), row), pl.BlockSpec((tm, D), row)),
        scratch_shapes=[pltpu.VMEM((2, TOP_K, tm, D), F32), pltpu.SemaphoreType.DMA((2,))],
    )
    return pl.pallas_call(
        functools.partial(_combine_ln_kernel, alpha=alpha, tm=tm, n_tok=N),
        out_shape=(jax.ShapeDtypeStruct((N, D), F32), jax.ShapeDtypeStruct((N, D), BF16)),
        grid_spec=grid_spec,
        compiler_params=_cparams(("arbitrary",)),
        name="moe_combine_ln",
    )(dest, x, wts, g.reshape(1, D), b.reshape(1, D), y)


def _rms(x, gain):
    return x * lax.rsqrt(jnp.mean(x * x, axis=-1, keepdims=True) + LN_EPS) * gain


CUM_CHUNK = 256


def _decay_kernel(x_ref, w_ref, bias_ref, c_ref):
    T = x_ref.shape[0]
    w = w_ref[...].astype(BF16)
    tri = (lax.broadcasted_iota(jnp.int32, (CUM_CHUNK, CUM_CHUNK), 0)
           >= lax.broadcasted_iota(jnp.int32, (CUM_CHUNK, CUM_CHUNK), 1)).astype(F32)
    carry = jnp.zeros((1, LANES), F32)
    for c in range(T // CUM_CHUNK):
        z = _dot(x_ref[c * CUM_CHUNK:(c + 1) * CUM_CHUNK, :].astype(BF16), w) + bias_ref[...]
        log_f = jnp.minimum(z, 0.0) - jnp.log1p(jnp.exp(-jnp.abs(z)))
        cs = jnp.dot(tri, log_f, preferred_element_type=F32, precision=lax.Precision.HIGHEST) + carry
        c_ref[c * CUM_CHUNK:(c + 1) * CUM_CHUNK, :] = cs
        carry = cs[CUM_CHUNK - 1:CUM_CHUNK, :]


def fox_decay(x, w, bias, B, T):
    N, D = x.shape
    assert T % CUM_CHUNK == 0
    return pl.pallas_call(
        _decay_kernel,
        out_shape=jax.ShapeDtypeStruct((N, LANES), F32),
        grid=(B,),
        in_specs=[pl.BlockSpec((T, D), lambda b: (b, 0)), pl.BlockSpec((D, LANES), lambda b: (0, 0)),
                  pl.BlockSpec((1, LANES), lambda b: (0, 0))],
        out_specs=pl.BlockSpec((T, LANES), lambda b: (b, 0)),
        compiler_params=_cparams(("arbitrary",)),
        name="fox_decay",
    )(x, w, bias.reshape(1, LANES))


LOG2E = 1.4426950408889634
FAST_SOFTMAX_BOUND = 38.0
BIAS_PIECES = 3


def _split3(a):
    hi = a.astype(BF16).astype(F32)
    r = a - hi
    mid = r.astype(BF16).astype(F32)
    return hi, mid, (r - mid).astype(BF16).astype(F32)


def _aux_lanes(pieces, ones_first, n):
    lane = lax.broadcasted_iota(jnp.int32, (n, LANES), 1)
    p0 = BIAS_PIECES if ones_first else 0
    o0 = 0 if ones_first else BIAS_PIECES
    out = jnp.where((lane >= o0) & (lane < o0 + BIAS_PIECES), 1.0, 0.0)
    for i, piece in enumerate(pieces):
        out = out + jnp.where(lane == p0 + i, piece, 0.0)
    return out


def _fox_kernel(fast_ref, q_ref, k_ref, v_ref, og_ref, ckrow_ref, ckcol_ref, cq_ref, bound_ref, qg_ref, kg_ref,
                o_ref, kx_ref, vx_ref, *, tq, tk, hb):
    qi = pl.program_id(2)
    dh = HEAD_DIM
    T = k_ref.shape[1]

    def piece_by_lane(a, lane):
        hi, mid, lo = _split3(a)
        in_group = lambda g: ((lane >= g * hb) & (lane < (g + 1) * hb)) | (
            (lane >= (BIAS_PIECES + g) * hb) & (lane < (BIAS_PIECES + g + 1) * hb))
        return jnp.where(in_group(0), hi, jnp.where(in_group(1), mid, lo))

    @pl.when(qi == 0)
    def _():
        lane_k = lax.broadcasted_iota(jnp.int32, (T, LANES), 1)
        k_aux = jnp.where(lane_k < BIAS_PIECES * hb, piece_by_lane(-LOG2E * ckcol_ref[0, 0], lane_k),
                          jnp.where(lane_k < 2 * BIAS_PIECES * hb, 1.0, 0.0)).astype(BF16)
        for h in range(hb):
            kx_ref[h, :, 0:dh] = _rms(k_ref[0, :, h * dh:(h + 1) * dh].astype(F32), kg_ref[...]).astype(BF16)
            kx_ref[h, :, dh:2 * dh] = k_aux
            vx_ref[h, :, 0:dh] = v_ref[0, :, h * dh:(h + 1) * dh].astype(BF16)
            vx_ref[h, :, dh:2 * dh] = jnp.ones((T, dh), BF16)

    row = qi * tq + lax.broadcasted_iota(jnp.int32, (tq, tk), 0)
    lane = lax.broadcasted_iota(jnp.int32, (tq, tk), 1)
    n_full = (qi * tq) // tk

    def finish(accs):
        for h in range(hb):
            o = accs[h][:, 0:dh] / jnp.maximum(accs[h][:, dh:2 * dh], TINY)
            gate = _sigmoid(og_ref[0, :, h * dh:(h + 1) * dh].astype(F32))
            o_ref[0, :, h * dh:(h + 1) * dh] = (o * gate).astype(o_ref.dtype)

    def q_normed(h):
        return _rms(q_ref[0, :, h * dh:(h + 1) * dh].astype(F32), qg_ref[...]) * (dh ** -0.5)

    @pl.when(fast_ref[0] == 1)
    def _():
        lane_q = lax.broadcasted_iota(jnp.int32, (tq, LANES), 1)
        r = piece_by_lane(LOG2E * (cq_ref[0, 0] - bound_ref[0:1, 0:1]), lane_q)
        qx = []
        for h in range(hb):
            mine = (lane_q & (hb - 1)) == h
            q_aux = jnp.where(mine & (lane_q < BIAS_PIECES * hb), 1.0,
                              jnp.where(mine & (lane_q < 2 * BIAS_PIECES * hb), r, 0.0))
            qx.append(jnp.concatenate([(q_normed(h) * LOG2E).astype(BF16), q_aux.astype(BF16)], axis=1))

        def step(j, accs, masked):
            start = pl.multiple_of(j * tk, tk)
            out = []
            for h in range(hb):
                s = _dot_nt(qx[h], kx_ref[h, pl.ds(start, tk), :])
                if masked:
                    s = jnp.where(j * tk + lane <= row, s, NEG)
                out.append(accs[h] + _dot(jnp.exp2(s).astype(BF16), vx_ref[h, pl.ds(start, tk), :]))
            return tuple(out)

        def diagonal(accs):
            hq = tq // 2
            start = pl.multiple_of(n_full * tk, tk)
            tri = (lax.broadcasted_iota(jnp.int32, (hq, hq), 1) <= lax.broadcasted_iota(jnp.int32, (hq, hq), 0))
            out = []
            for h in range(hb):
                k_lo, k_hi = kx_ref[h, pl.ds(start, hq), :], kx_ref[h, pl.ds(start + hq, hq), :]
                v_lo, v_all = vx_ref[h, pl.ds(start, hq), :], vx_ref[h, pl.ds(start, tk), :]
                p_top = jnp.exp2(jnp.where(tri, _dot_nt(qx[h][0:hq], k_lo), NEG)).astype(BF16)
                p_bot = jnp.concatenate(
                    [jnp.exp2(_dot_nt(qx[h][hq:tq], k_lo)).astype(BF16),
                     jnp.exp2(jnp.where(tri, _dot_nt(qx[h][hq:tq], k_hi), NEG)).astype(BF16)], axis=1)
                out.append(accs[h] + jnp.concatenate([_dot(p_top, v_lo), _dot(p_bot, v_all)], axis=0))
            return out

        accs = tuple(jnp.zeros((tq, 2 * dh), F32) for _ in range(hb))
        accs = lax.fori_loop(0, n_full, lambda j, a: step(j, a, False), accs)
        finish(diagonal(accs) if tq == tk else step(n_full, accs, True))

    @pl.when(fast_ref[0] == 0)
    def _():
        qs = [q_normed(h).astype(BF16) for h in range(hb)]

        def step(j, carry, masked):
            start = pl.multiple_of(j * tk, tk)
            out = []
            for h in range(hb):
                m, acc = carry[h]
                s = _dot_nt(qs[h], kx_ref[h, pl.ds(start, tk), 0:dh]) - ckrow_ref[0, h, pl.ds(j, 1), :]
                if masked:
                    s = jnp.where(j * tk + lane <= row, s, NEG)
                m_new = jnp.maximum(m, jnp.max(s, axis=-1, keepdims=True))
                p = jnp.exp(s - m_new).astype(BF16)
                acc = jnp.exp(m - m_new) * acc + _dot(p, vx_ref[h, pl.ds(start, tk), :])
                out.append((m_new, acc))
            return tuple(out)

        init = tuple((jnp.full((tq, 1), NEG, F32), jnp.zeros((tq, 2 * dh), F32)) for _ in range(hb))
        carry = lax.fori_loop(0, n_full, lambda j, c: step(j, c, False), init)
        finish([c[1] for c in step(n_full, carry, True)])


def fox_attention(proj, c, q_gain, k_gain, B, T, H, *, tq=512, tk=512, hb=4):
    nq = T // tq
    nk = T // tk
    dh = HEAD_DIM
    hg = H // hb
    c_row = c.transpose(0, 2, 1).reshape(B, H, nk, tk)
    assert hb & (hb - 1) == 0 and 2 * BIAS_PIECES * hb <= LANES
    c_col = c.reshape(B, T, hg, hb).transpose(0, 2, 1, 3)
    c_col = jnp.concatenate([jnp.tile(c_col, (1, 1, 1, 2 * BIAS_PIECES)),
                             jnp.zeros((B, hg, T, LANES - 2 * BIAS_PIECES * hb), F32)], axis=-1)
    bound = jnp.max(jnp.abs(q_gain)) * jnp.max(jnp.abs(k_gain)) * (dh ** 0.5) * 1.01
    fast = (bound <= FAST_SOFTMAX_BOUND).astype(jnp.int32).reshape(1)
    full = lambda off: pl.BlockSpec((1, T, hb * dh), lambda b, h, i, f: (b, 0, off + h))
    tile = lambda off: pl.BlockSpec((1, tq, hb * dh), lambda b, h, i, f: (b, i, off + h))
    vec = pl.BlockSpec((1, dh), lambda b, h, i, f: (0, 0))
    grid_spec = pltpu.PrefetchScalarGridSpec(
        num_scalar_prefetch=1,
        grid=(B, hg, nq),
        in_specs=[
            tile(0), full(hg), full(2 * hg), tile(3 * hg),
            pl.BlockSpec((1, hb, nk, tk), lambda b, h, i, f: (b, h, 0, 0)),
            pl.BlockSpec((1, 1, T, LANES), lambda b, h, i, f: (b, h, 0, 0)),
            pl.BlockSpec((1, 1, tq, LANES), lambda b, h, i, f: (b, h, i, 0)),
            vec, vec, vec,
        ],
        out_specs=pl.BlockSpec((1, tq, hb * dh), lambda b, h, i, f: (b, i, h)),
        scratch_shapes=[pltpu.VMEM((hb, T, 2 * dh), BF16), pltpu.VMEM((hb, T, 2 * dh), BF16)],
    )
    return pl.pallas_call(
        functools.partial(_fox_kernel, tq=tq, tk=tk, hb=hb),
        out_shape=jax.ShapeDtypeStruct((B, T, H * dh), BF16),
        grid_spec=grid_spec,
        compiler_params=_cparams(("arbitrary", "arbitrary", "arbitrary")),
        name="fox_attention",
    )(fast, proj, proj, proj, proj, c_row, c_col, c_col, jnp.full((1, dh), bound, F32),
      q_gain.reshape(1, dh), k_gain.reshape(1, dh))


def _rope(x, cos, sin_lo, sin_hi):
    return (x * cos + pltpu.roll(x, LANES - ROPE_HALF, 1) * sin_lo
            + pltpu.roll(x, ROPE_HALF, 1) * sin_hi)


def _gelu_tanh(x):
    return 0.5 * x * (1.0 + jnp.tanh(0.7978845608028654 * (x + 0.044715 * (x * x * x))))


def _compress_kernel(raw_ref, pos_ref, w1_ref, w2_ref, cos_ref, slo_ref, shi_ref, o_ref, *, n_cmp):
    nb = o_ref.shape[-2]
    half = CMP_LEN // 2
    acc_a = jnp.zeros((nb, HEAD_DIM), F32)
    acc_b = jnp.zeros((nb, HEAD_DIM), F32)
    for l in range(half):
        rl = raw_ref[0, pl.ds(l, nb, stride=CMP_STRIDE), :]
        wa = w1_ref[0, l * HEAD_DIM:(l + 1) * HEAD_DIM, :].astype(BF16)
        wb = w1_ref[0, (half + l) * HEAD_DIM:(half + l + 1) * HEAD_DIM, :].astype(BF16)
        acc_a = acc_a + _dot((rl + pos_ref[0, l:l + 1, :]).astype(BF16), wa)
        acc_b = acc_b + _dot((rl + pos_ref[0, half + l:half + l + 1, :]).astype(BF16), wb)
    pre = acc_a + pltpu.roll(acc_b, nb - 1, 0)
    out = _dot(_gelu_tanh(pre).astype(BF16), w2_ref[0].astype(BF16))
    roped = _rope(out, cos_ref[...], slo_ref[...], shi_ref[...])
    out = jnp.where(pl.program_id(1) == 0, roped, out)
    rows = lax.broadcasted_iota(jnp.int32, out.shape, 0)
    o_ref[0, 0, 0] = jnp.where(rows < n_cmp, out, 0.0)


def compress(qkv, pos, w1, w2, tabs, B, T, kv_block0):
    G = NSA_KV_HEADS
    nb = T // CMP_STRIDE
    n_cmp = (T - CMP_LEN) // CMP_STRIDE + 1
    cos, slo, shi = tabs
    tab = pl.BlockSpec((nb, HEAD_DIM), lambda b, kv, g: (0, 0))
    return pl.pallas_call(
        functools.partial(_compress_kernel, n_cmp=n_cmp),
        out_shape=jax.ShapeDtypeStruct((B, 2, G, nb, HEAD_DIM), F32),
        grid=(B, 2, G),
        in_specs=[
            pl.BlockSpec((1, T, HEAD_DIM), lambda b, kv, g: (b, 0, kv_block0 + kv * G + g)),
            pl.BlockSpec((1, CMP_LEN, HEAD_DIM), lambda b, kv, g: (kv, 0, 0)),
            pl.BlockSpec((1, CMP_LEN * HEAD_DIM, HEAD_DIM), lambda b, kv, g: (kv, 0, 0)),
            pl.BlockSpec((1, HEAD_DIM, HEAD_DIM), lambda b, kv, g: (kv, 0, 0)),
            tab, tab, tab,
        ],
        out_specs=pl.BlockSpec((1, 1, 1, nb, HEAD_DIM), lambda b, kv, g: (b, kv, g, 0, 0)),
        compiler_params=_cparams(("arbitrary", "arbitrary", "arbitrary")),
        name="nsa_compress",
    )(qkv, pos, w1, w2, cos, slo, shi)


SLC_CHUNK = 512
WIN_SPAN = WIN + Q_BLOCK
NSA_QB = 4


def _select_blocks(imp, qi, n_slc):
    QB = Q_BLOCK
    n_rows = -(-n_slc // 8) * 8
    imp_t = jnp.transpose(imp)[0:n_rows]
    blk = lax.broadcasted_iota(jnp.int32, (n_rows, QB), 0)
    t = qi * QB + lax.broadcasted_iota(jnp.int32, (n_rows, QB), 1)
    cur = jnp.right_shift(t, 6)
    forced = (blk == 0) | (blk == cur) | (blk == cur - 1)
    score = jnp.where(forced, BIG, jnp.where(blk * SLC_LEN <= t, imp_t, -BIG))
    score = jnp.where(blk < n_slc, score, -2.0 * BIG)
    cnt = jnp.zeros((n_rows, QB), F32)
    for jp in range(n_slc):
        r = score[jp:jp + 1, :]
        cnt = cnt + ((r > score) | ((r == score) & (jp < blk))).astype(F32)
    sel_t = ((cnt < float(min(N_SLC, n_slc))) & (blk < n_slc)).astype(F32)
    if n_rows < LANES:
        sel_t = jnp.concatenate([sel_t, jnp.zeros((LANES - n_rows, QB), F32)], axis=0)
    return jnp.transpose(sel_t).astype(BF16)


def _nsa_kernel(q_ref, ks_ref, vs_ref, kw_ref, vw_ref, kc_ref, vc_ref, gl_ref,
                cq_ref, sloq_ref, shiq_ref, ck_ref, slok_ref, shik_ref, c2s_ref, ex_ref,
                o_ref, ksx_ref, vsx_ref, kwb_ref, vwx_ref, bias_ref, kmax_ref, acc_ref, *, n_slc, n_cmp):
    step = pl.program_id(2)
    QB, HPG, dh = Q_BLOCK, NSA_HPG, HEAD_DIM
    T = ks_ref.shape[1]

    @pl.when(step == 0)
    def _():
        ks = _rope(ks_ref[0], ck_ref[...], slok_ref[...], shik_ref[...])
        ksx_ref[:, 0:dh] = ks.astype(BF16)
        ksx_ref[:, dh:2 * dh] = _aux_lanes((), False, T).astype(BF16)
        kmax_ref[...] = jnp.full(kmax_ref.shape, jnp.max(jnp.sum(ks * ks, axis=-1, keepdims=True)), F32)
        kwb_ref[...] = _rope(kw_ref[0], ck_ref[...], slok_ref[...], shik_ref[...]).astype(BF16)
        ones = jnp.ones((T, dh), BF16)
        vsx_ref[:, 0:dh] = vs_ref[0].astype(BF16)
        vsx_ref[:, dh:2 * dh] = ones
        vwx_ref[:, 0:dh] = vw_ref[0].astype(BF16)
        vwx_ref[:, dh:2 * dh] = ones

    qis = [step * NSA_QB + b for b in range(NSA_QB)]
    rows = lambda b, h: slice((b * HPG + h) * QB, (b * HPG + h + 1) * QB)
    qrows = lambda b: slice(b * QB, (b + 1) * QB)
    pieces, norms2 = [], []
    for b in range(NSA_QB):
        for h in range(HPG):
            x = q_ref[0, qrows(b), h * dh:(h + 1) * dh]
            x = _rope(x, cq_ref[qrows(b), :], sloq_ref[qrows(b), :], shiq_ref[qrows(b), :]) * (dh ** -0.5)
            pieces.append(x)
            norms2.append(_dot((x * x).astype(BF16), jnp.ones((dh, LANES), BF16)))
    q = jnp.concatenate(pieces, axis=0).astype(BF16)

    lane = lax.broadcasted_iota(jnp.int32, (QB, LANES), 1)
    sub = lax.broadcasted_iota(jnp.int32, (QB, LANES), 0)

    kc = kc_ref[0, 0, 0].astype(BF16)
    vc = vc_ref[0, 0, 0].astype(BF16)
    sc = _dot_nt(q, kc)
    o_cmp, sels = {}, []
    for b in range(NSA_QB):
        t = qis[b] * QB + sub
        mask_c = (lane * CMP_STRIDE + (CMP_LEN - 1) <= t) & (lane < n_cmp)
        mask_cf = mask_c.astype(F32)
        imp_c = jnp.zeros((QB, LANES), F32)
        for h in range(HPG):
            s = jnp.where(mask_c, sc[rows(b, h)], NEG)
            e = jnp.exp(s - jnp.max(s, axis=-1, keepdims=True)) * mask_cf
            p = e / jnp.maximum(jnp.sum(e, axis=-1, keepdims=True), TINY)
            imp_c = imp_c + p
            o_cmp[b, h] = _dot(p.astype(BF16), vc)
        imp = jnp.dot(imp_c, c2s_ref[...], preferred_element_type=F32, precision=lax.Precision.HIGHEST)
        sels.append(_select_blocks(imp, qis[b], n_slc))

    sel_all = jnp.concatenate(sels, axis=0)
    for c in range(T // SLC_CHUNK):
        selx = _dot(sel_all, ex_ref[:, c * SLC_CHUNK:(c + 1) * SLC_CHUNK])
        kpos = c * SLC_CHUNK + lax.broadcasted_iota(jnp.int32, (QB, SLC_CHUNK), 1)
        for b in range(NSA_QB):
            tq_ = qis[b] * QB + lax.broadcasted_iota(jnp.int32, (QB, SLC_CHUNK), 0)
            bias_ref[b, c] = jnp.where((selx[qrows(b)] > 0.5) & (kpos <= tq_), 0.0, NEG)

    o_win = {}
    for b in range(NSA_QB):
        wstart = pl.multiple_of(jnp.clip(qis[b] * QB - WIN, 0, T - WIN_SPAN), Q_BLOCK)
        qb = q[b * HPG * QB:(b + 1) * HPG * QB]
        sw = _dot_nt(qb, kwb_ref[pl.ds(wstart, WIN_SPAN), :])
        kpos = wstart + lax.broadcasted_iota(jnp.int32, (QB, WIN_SPAN), 1)
        tw = qis[b] * QB + lax.broadcasted_iota(jnp.int32, (QB, WIN_SPAN), 0)
        bias_w = jnp.where((kpos <= tw) & (kpos > tw - WIN), 0.0, NEG)
        pw = []
        for h in range(HPG):
            sh = sw[h * QB:(h + 1) * QB] + bias_w
            pw.append(jnp.exp(sh - jnp.max(sh, axis=-1, keepdims=True)).astype(BF16))
        ow = _dot(jnp.concatenate(pw, axis=0), vwx_ref[pl.ds(wstart, WIN_SPAN), :])
        for h in range(HPG):
            acc_w = ow[h * QB:(h + 1) * QB]
            o_win[b, h] = acc_w[:, 0:dh] / jnp.maximum(acc_w[:, dh:2 * dh], TINY)

    chains = [(b, h) for b in range(NSA_QB) for h in range(HPG)]

    n_chunks = (qis[-1] * QB + QB + SLC_CHUNK - 1) // SLC_CHUNK
    kmax2 = kmax_ref[0:1, 0:1]
    bounds = [jnp.sqrt(n2 * kmax2) * 1.02 for n2 in norms2]
    worst = functools.reduce(jnp.maximum, [jnp.max(bd) for bd in bounds])

    @pl.when(worst <= FAST_SOFTMAX_BOUND)
    def _():
        qx = jnp.concatenate(
            [jnp.concatenate([(x * LOG2E).astype(BF16),
                              _aux_lanes(_split3(-LOG2E * bd), True, QB).astype(BF16)], axis=1)
             for x, bd in zip(pieces, bounds)], axis=0)

        def body(c, accs):
            start = pl.multiple_of(c * SLC_CHUNK, SLC_CHUNK)
            s = _dot_nt(qx, ksx_ref[pl.ds(start, SLC_CHUNK), :])
            ps = [jnp.exp2(s[rows(b, h)] + bias_ref[b, c]).astype(BF16) for (b, h) in chains]
            pv = _dot(jnp.concatenate(ps, axis=0), vsx_ref[pl.ds(start, SLC_CHUNK), :])
            return tuple(accs[n] + pv[rows(b, h)] for n, (b, h) in enumerate(chains))

        accs = lax.fori_loop(0, n_chunks, body, tuple(jnp.zeros((QB, 2 * dh), F32) for _ in chains))
        for n, (b, h) in enumerate(chains):
            acc_ref[rows(b, h), :] = accs[n]

    @pl.when(worst > FAST_SOFTMAX_BOUND)
    def _():
        def body(c, carry):
            start = pl.multiple_of(c * SLC_CHUNK, SLC_CHUNK)
            s = _dot_nt(q, ksx_ref[pl.ds(start, SLC_CHUNK), 0:dh])
            ms, alphas, ps = [], [], []
            for n, (b, h) in enumerate(chains):
                sh = s[rows(b, h)] + bias_ref[b, c]
                m_new = jnp.maximum(carry[n][0], jnp.max(sh, axis=-1, keepdims=True))
                alphas.append(jnp.exp(carry[n][0] - m_new))
                ps.append(jnp.exp(sh - m_new).astype(BF16))
                ms.append(m_new)
            pv = _dot(jnp.concatenate(ps, axis=0), vsx_ref[pl.ds(start, SLC_CHUNK), :])
            return tuple((ms[n], alphas[n] * carry[n][1] + pv[rows(b, h)]) for n, (b, h) in enumerate(chains))

        init = tuple((jnp.full((QB, 1), NEG, F32), jnp.zeros((QB, 2 * dh), F32)) for _ in chains)
        slc = lax.fori_loop(0, n_chunks, body, init)
        for n, (b, h) in enumerate(chains):
            acc_ref[rows(b, h), :] = slc[n][1]

    for b in range(NSA_QB):
        gates = _sigmoid(gl_ref[0, qrows(b), :])
        for h in range(HPG):
            acc_s = acc_ref[rows(b, h), :]
            o_s = acc_s[:, 0:dh] / jnp.maximum(acc_s[:, dh:2 * dh], TINY)
            out = (gates[:, 3 * h:3 * h + 1] * o_cmp[b, h] + gates[:, 3 * h + 1:3 * h + 2] * o_s
                   + gates[:, 3 * h + 2:3 * h + 3] * o_win[b, h])
            o_ref[0, qrows(b), h * dh:(h + 1) * dh] = out.astype(o_ref.dtype)


def nsa_attention(qkv, cmp_kv, gl, tabs_q, B, T):
    G, HPG, dh = NSA_KV_HEADS, NSA_HPG, HEAD_DIM
    tq = NSA_QB * Q_BLOCK
    nb = T // CMP_STRIDE
    n_cmp = (T - CMP_LEN) // CMP_STRIDE + 1
    n_slc = T // SLC_LEN
    assert nb == LANES and n_slc <= LANES and T % SLC_CHUNK == 0 and T >= WIN_SPAN and T % tq == 0
    kvb = (HPG * G)

    c_start = np.arange(nb) * CMP_STRIDE
    s_start = np.arange(LANES) * SLC_LEN
    c2s = ((c_start[:, None] < s_start[None, :] + SLC_LEN) & (c_start[:, None] + CMP_LEN > s_start[None, :])
           & (np.arange(nb)[:, None] < n_cmp) & (np.arange(LANES)[None, :] < n_slc)).astype(np.float32)
    expand = (np.arange(T)[None, :] // SLC_LEN == np.arange(LANES)[:, None]).astype(np.float32)

    cos, slo, shi = tabs_q
    kv_full = lambda blk: pl.BlockSpec((1, T, dh), lambda b, g, i: (b, 0, kvb + blk + g))
    qtab = pl.BlockSpec((tq, dh), lambda b, g, i: (i, 0))
    ktab = pl.BlockSpec((T, dh), lambda b, g, i: (0, 0))
    return pl.pallas_call(
        functools.partial(_nsa_kernel, n_slc=n_slc, n_cmp=n_cmp),
        out_shape=jax.ShapeDtypeStruct((B, T, G * HPG * dh), BF16),
        grid=(B, G, T // tq),
        in_specs=[
            pl.BlockSpec((1, tq, HPG * dh), lambda b, g, i: (b, i, g)),
            kv_full(2 * G), kv_full(3 * G), kv_full(4 * G), kv_full(5 * G),
            pl.BlockSpec((1, 1, 1, nb, dh), lambda b, g, i: (b, 0, g, 0, 0)),
            pl.BlockSpec((1, 1, 1, nb, dh), lambda b, g, i: (b, 1, g, 0, 0)),
            pl.BlockSpec((1, tq, LANES), lambda b, g, i: (b, i, g)),
            qtab, qtab, qtab, ktab, ktab, ktab,
            pl.BlockSpec((nb, LANES), lambda b, g, i: (0, 0)),
            pl.BlockSpec((LANES, T), lambda b, g, i: (0, 0)),
        ],
        out_specs=pl.BlockSpec((1, tq, HPG * dh), lambda b, g, i: (b, i, g)),
        scratch_shapes=[pltpu.VMEM((T, 2 * dh), BF16), pltpu.VMEM((T, 2 * dh), BF16),
                        pltpu.VMEM((T, dh), BF16), pltpu.VMEM((T, 2 * dh), BF16),
                        pltpu.VMEM((NSA_QB, T // SLC_CHUNK, Q_BLOCK, SLC_CHUNK), F32),
                        pltpu.VMEM((8, LANES), F32),
                        pltpu.VMEM((NSA_QB * HPG * Q_BLOCK, 2 * dh), F32)],
        compiler_params=_cparams(("arbitrary", "arbitrary", "arbitrary")),
        name="nsa_attention",
    )(qkv, qkv, qkv, qkv, qkv, cmp_kv, cmp_kv, gl, cos, slo, shi, cos, slo, shi,
      jnp.asarray(c2s), jnp.asarray(expand, dtype=BF16))


CONV_PAD = 32
CONV_CHUNK = 256


def _conv_kernel(a_ref, b_ref, w_ref, cb_ref, g_ref, gb_ref, o_ref, u_ref):
    T = a_ref.shape[1]
    u_ref[0:CONV_PAD, :] = jnp.zeros((CONV_PAD, LANES), F32)
    u_ref[CONV_PAD:CONV_PAD + T, :] = a_ref[0].astype(F32) * _sigmoid(b_ref[0].astype(F32))
    base = CONV_PAD - (CONV_KERNEL - 1)
    for c in range(T // CONV_CHUNK):
        t0 = c * CONV_CHUNK
        acc = jnp.zeros((CONV_CHUNK, LANES), F32)
        for k in range(CONV_KERNEL):
            acc = acc + u_ref[t0 + base + k:t0 + base + k + CONV_CHUNK, :] * w_ref[k:k + 1, :]
        acc = acc + cb_ref[...]
        mu = jnp.mean(acc, axis=-1, keepdims=True)
        d = acc - mu
        var = jnp.mean(d * d, axis=-1, keepdims=True)
        y = d * lax.rsqrt(var + LN_EPS) * g_ref[...] + gb_ref[...]
        o_ref[0, t0:t0 + CONV_CHUNK, :] = (y * _sigmoid(y)).astype(o_ref.dtype)


def conv_module(glu, conv_w, conv_b, gn_g, gn_b, B, T):
    C = glu.shape[-1] // 2
    ng = C // LANES
    assert C // CONV_GROUPS == LANES
    vec = pl.BlockSpec((1, LANES), lambda b, g: (0, g))
    return pl.pallas_call(
        _conv_kernel,
        out_shape=jax.ShapeDtypeStruct((B, T, C), BF16),
        grid=(B, ng),
        in_specs=[
            pl.BlockSpec((1, T, LANES), lambda b, g: (b, 0, g)),
            pl.BlockSpec((1, T, LANES), lambda b, g: (b, 0, ng + g)),
            pl.BlockSpec((CONV_KERNEL, LANES), lambda b, g: (0, g)),
            vec, vec, vec,
        ],
        out_specs=pl.BlockSpec((1, T, LANES), lambda b, g: (b, 0, g)),
        scratch_shapes=[pltpu.VMEM((CONV_PAD + T, LANES), F32)],
        compiler_params=_cparams(("arbitrary", "arbitrary")),
        name="conformer_conv",
    )(glu, glu, conv_w, conv_b.reshape(1, C), gn_g.reshape(1, C), gn_b.reshape(1, C))


def _split2(a):
    hi = a.astype(BF16)
    return hi, (a - hi.astype(F32)).astype(BF16)


def _router_kernel(x_ref, w_ref, bias_ref, idx_ref, wts_ref):
    xh, xl = _split2(x_ref[...])
    wh, wl = _split2(w_ref[...])
    logits = _dot(xh, wh) + (_dot(xh, wl) + _dot(xl, wh))
    aff = _sigmoid(jnp.concatenate(
        [jnp.transpose(logits[c * LANES:(c + 1) * LANES])[0:N_EXPERTS] for c in range(logits.shape[0] // LANES)],
        axis=1))
    sel = aff + bias_ref[...]
    a = [aff[e:e + 1, :] for e in range(N_EXPERTS)]
    s = [sel[e:e + 1, :] for e in range(N_EXPERTS)]
    P = EXPERTS_PER_GROUP
    grp = []
    for g in range(N_EXPERT_GROUPS):
        v = s[g * P:(g + 1) * P]
        best = None
        for i in range(P):
            for j in range(i + 1, P):
                pair = v[i] + v[j]
                best = pair if best is None else jnp.maximum(best, pair)
        grp.append(best)
    gbest = jnp.zeros_like(grp[0], dtype=jnp.int32)
    gval = grp[0]
    for g in range(1, N_EXPERT_GROUPS):
        better = grp[g] > gval
        gbest = jnp.where(better, g, gbest)
        gval = jnp.where(better, grp[g], gval)
    cs, ca = [], []
    for p in range(P):
        sv, av = s[p], a[p]
        for g in range(1, N_EXPERT_GROUPS):
            sv = jnp.where(gbest == g, s[g * P + p], sv)
            av = jnp.where(gbest == g, a[g * P + p], av)
        cs.append(sv)
        ca.append(av)
    i1 = jnp.zeros_like(gbest)
    v1, a1 = cs[0], ca[0]
    for p in range(1, P):
        better = cs[p] > v1
        i1 = jnp.where(better, p, i1)
        v1 = jnp.where(better, cs[p], v1)
        a1 = jnp.where(better, ca[p], a1)
    i2 = jnp.full_like(gbest, -1)
    v2 = jnp.full_like(v1, -jnp.inf)
    a2 = jnp.zeros_like(a1)
    for p in range(P):
        better = (i1 != p) & (cs[p] > v2)
        i2 = jnp.where(better, p, i2)
        v2 = jnp.where(better, cs[p], v2)
        a2 = jnp.where(better, ca[p], a2)
    den = a1 + a2
    idx_ref[...] = jnp.zeros(idx_ref.shape, jnp.int32)
    wts_ref[...] = jnp.zeros(wts_ref.shape, F32)
    idx_ref[0:1, :] = gbest * P + i1
    idx_ref[1:2, :] = gbest * P + i2
    wts_ref[0:1, :] = a1 / den
    wts_ref[1:2, :] = a2 / den


def router(x, router_w, router_bias, *, tm=1024):
    N, D = x.shape
    E = N_EXPERTS
    return pl.pallas_call(
        _router_kernel,
        out_shape=(jax.ShapeDtypeStruct((8, N), jnp.int32), jax.ShapeDtypeStruct((8, N), F32)),
        grid=(N // tm,),
        in_specs=[
            pl.BlockSpec((tm, D), lambda i: (i, 0)),
            pl.BlockSpec((D, LANES), lambda i: (0, 0)),
            pl.BlockSpec((E, 1), lambda i: (0, 0)),
        ],
        out_specs=(pl.BlockSpec((8, tm), lambda i: (0, i)), pl.BlockSpec((8, tm), lambda i: (0, i))),
        compiler_params=_cparams(("arbitrary",)),
        name="moe_router",
    )(x, jnp.pad(router_w, ((0, 0), (0, LANES - E))), router_bias.reshape(E, 1))


PLAN_CHUNK = 512


def _plan_kernel(idx_ref, dest_ref, cnt_ref, *, n_tok):
    E = N_EXPERTS
    sub = lax.broadcasted_iota(jnp.int32, (TOP_K * E, n_tok), 0)
    tgt = jnp.where(sub < E, idx_ref[0:1, :], idx_ref[1:2, :])
    onehot = ((sub & (E - 1)) == tgt).astype(F32)
    onehot_b = onehot.astype(BF16)
    tri = (lax.broadcasted_iota(jnp.int32, (PLAN_CHUNK, PLAN_CHUNK), 0)
           <= lax.broadcasted_iota(jnp.int32, (PLAN_CHUNK, PLAN_CHUNK), 1)).astype(F32).astype(BF16)
    carry = jnp.zeros((TOP_K * E, 1), F32)
    parts = []
    for c in range(n_tok // PLAN_CHUNK):
        pre = _dot(onehot_b[:, c * PLAN_CHUNK:(c + 1) * PLAN_CHUNK], tri) + carry
        parts.append(pre)
        carry = pre[:, PLAN_CHUNK - 1:PLAN_CHUNK]
    excl = jnp.concatenate(parts, axis=1) - onehot
    cnt0 = carry[0:E]
    tot = cnt0 + carry[E:2 * E]
    lower = (lax.broadcasted_iota(jnp.int32, (E, E), 1)
             < lax.broadcasted_iota(jnp.int32, (E, E), 0)).astype(F32)
    offs = jnp.dot(lower, jnp.broadcast_to(tot, (E, LANES)), preferred_element_type=F32,
                   precision=lax.Precision.HIGHEST)[:, 0:1]
    base = jnp.concatenate([offs, offs + cnt0], axis=0)
    val = onehot * (base + excl)
    dest_ref[...] = jnp.zeros(dest_ref.shape, jnp.int32)
    dest_ref[0:1, :] = jnp.sum(val[0:E], axis=0, keepdims=True).astype(jnp.int32)
    dest_ref[1:2, :] = jnp.sum(val[E:2 * E], axis=0, keepdims=True).astype(jnp.int32)
    cnt_ref[...] = jnp.broadcast_to(tot, (E, LANES)).astype(jnp.int32)


def moe_plan(idx):
    n_tok = idx.shape[1]
    assert n_tok % PLAN_CHUNK == 0 and TOP_K == 2
    return pl.pallas_call(
        functools.partial(_plan_kernel, n_tok=n_tok),
        out_shape=(jax.ShapeDtypeStruct((8, n_tok), jnp.int32),
                   jax.ShapeDtypeStruct((N_EXPERTS, LANES), jnp.int32)),
        compiler_params=_cparams(None),
        name="moe_plan",
    )(idx)


def _scatter_kernel(dest_ref, x_ref, xs_hbm, sem, *, tm, n_tok):
    base = pl.program_id(0) * tm

    def copies(start):
        def make(r, k, d):
            cp = pltpu.make_async_copy(x_ref.at[pl.ds(r, 1)], xs_hbm.at[pl.ds(d, 1)], sem)
            if start:
                cp.start(priority=k)
            else:
                cp.wait()
        _row_copies(dest_ref, n_tok, base, tm, make)

    copies(True)
    copies(False)


def moe_scatter(x, dest, *, tm=256):
    N, D = x.shape
    grid_spec = pltpu.PrefetchScalarGridSpec(
        num_scalar_prefetch=1,
        grid=(N // tm,),
        in_specs=[pl.BlockSpec((tm, D), lambda i, d: (i, 0))],
        out_specs=pl.BlockSpec(memory_space=pl.ANY),
        scratch_shapes=[pltpu.SemaphoreType.DMA(())],
    )
    return pl.pallas_call(
        functools.partial(_scatter_kernel, tm=tm, n_tok=N),
        out_shape=jax.ShapeDtypeStruct((TOP_K * N, D), x.dtype),
        grid_spec=grid_spec,
        compiler_params=_cparams(("arbitrary",)),
        name="moe_scatter",
    )(dest, x)


def _moe_kernel(ti_ref, te_ref, lo_ref, hi_ref, first_ref, head_ref, slot_ref, nxt_ref,
                xs_ref, wg_hbm, wu_hbm, wd_hbm, y_ref,
                wg32_ref, wu32_ref, wd32_ref, sem, *, layer):
    i = pl.program_id(0)

    def weight_copies(e, slot):
        return [pltpu.make_async_copy(src.at[layer, e], dst.at[slot], sem.at[slot, n])
                for n, (src, dst) in enumerate(((wg_hbm, wg32_ref), (wu_hbm, wu32_ref), (wd_hbm, wd32_ref)))]

    @pl.when(i == 0)
    def _():
        for cp in weight_copies(te_ref[0], 0):
            cp.start()

    @pl.when(head_ref[i] == 1)
    def _():
        slot = slot_ref[i]
        for cp in weight_copies(te_ref[i], slot):
            cp.wait()

        @pl.when(nxt_ref[i] >= 0)
        def _():
            for cp in weight_copies(nxt_ref[i], 1 - slot):
                cp.start()

    lo, hi = lo_ref[i], hi_ref[i]

    @pl.when(hi > lo)
    def _():
        x = xs_ref[...].astype(BF16)
        ws = slot_ref[i]
        g = _dot(x, wg32_ref[ws].astype(BF16))
        u = _dot(x, wu32_ref[ws].astype(BF16))
        rows = lax.broadcasted_iota(jnp.int32, (x.shape[0], 1), 0)
        h = jnp.where((rows >= lo) & (rows < hi), (g * _sigmoid(g)) * u, 0.0)
        y = _dot(h.astype(BF16), wd32_ref[ws].astype(BF16))

        @pl.when(first_ref[i] == 1)
        def _():
            y_ref[...] = y

        @pl.when(first_ref[i] == 0)
        def _():
            y_ref[...] += y


def moe_experts(xs, items, w_gate, w_up, w_down, layer, *, tm=MOE_TM):
    P, D = xs.shape
    Fh = w_gate.shape[-1]
    n_items = items[0].shape[0]
    xmap = lambda i, ti, *_: (ti[i], 0)
    hbm = pl.BlockSpec(memory_space=pl.ANY)
    grid_spec = pltpu.PrefetchScalarGridSpec(
        num_scalar_prefetch=len(items),
        grid=(n_items,),
        in_specs=[pl.BlockSpec((tm, D), xmap), hbm, hbm, hbm],
        out_specs=pl.BlockSpec((tm, D), xmap),
        scratch_shapes=[pltpu.VMEM((2, D, Fh), F32), pltpu.VMEM((2, D, Fh), F32), pltpu.VMEM((2, Fh, D), F32),
                        pltpu.SemaphoreType.DMA((2, 3))],
    )
    return pl.pallas_call(
        functools.partial(_moe_kernel, layer=layer),
        out_shape=jax.ShapeDtypeStruct((P, D), F32),
        grid_spec=grid_spec,
        compiler_params=_cparams(("arbitrary",)),
        name="moe_experts",
    )(*items, xs, w_gate, w_up, w_down)


def moe_items(tot, n_rows, tm=MOE_TM):
    E = N_EXPERTS
    n_max = n_rows // tm + E
    ar = jnp.arange(E, dtype=jnp.int32)
    ends = jnp.cumsum(tot)
    offs = ends - tot
    first_tile = offs // tm
    n_e = jnp.where(tot > 0, (ends - 1) // tm - first_tile + 1, 0)
    s_end = jnp.cumsum(n_e)
    s_beg = s_end - n_e
    n_items = s_end[-1]
    i = jnp.arange(n_max, dtype=jnp.int32)
    ic = jnp.minimum(i, n_items - 1)
    e_i = jnp.sum((ic[:, None] >= s_end[None, :]).astype(jnp.int32), axis=1)
    pick = (e_i[:, None] == ar[None, :]).astype(jnp.int32)
    at = lambda v: jnp.sum(pick * v[None, :], axis=1)
    tile = at(first_tile) + ic - at(s_beg)
    live = i < n_items
    lo = jnp.where(live, jnp.maximum(at(offs), tile * tm) - tile * tm, 0)
    hi = jnp.where(live, jnp.minimum(at(ends), tile * tm + tm) - tile * tm, 0)
    prev_tile = jnp.concatenate([jnp.full((1,), -1, jnp.int32), tile[:-1]])
    first = live & (tile != prev_tile)
    head = live & (ic == at(s_beg))
    used = (tot > 0).astype(jnp.int32)
    slot = at(jnp.cumsum(used) - used) & 1
    later = jnp.where((ar[None, :] > ar[:, None]) & (tot[None, :] > 0), ar[None, :], E)
    nxt_e = jnp.min(later, axis=1)
    nxt = at(jnp.where(nxt_e < E, nxt_e, -1))
    return tuple(v.astype(jnp.int32) for v in (tile, e_i, lo, hi, first, head, slot, nxt))


def _rope_tables(pos):
    inv = ROPE_THETA ** (-jnp.arange(0, ROPE_DIM, 2, dtype=F32) / ROPE_DIM)
    ang = pos.astype(F32)[:, None] * inv[None, :]
    cos, sin = jnp.cos(ang), jnp.sin(ang)
    n = pos.shape[0]
    rest = HEAD_DIM - ROPE_DIM
    cos_t = jnp.concatenate([cos, cos, jnp.ones((n, rest), F32)], axis=1)
    sin_lo = jnp.concatenate([-sin, jnp.zeros((n, HEAD_DIM - ROPE_HALF), F32)], axis=1)
    sin_hi = jnp.concatenate([jnp.zeros((n, ROPE_HALF), F32), sin, jnp.zeros((n, rest), F32)], axis=1)
    return cos_t, sin_lo, sin_hi


def _even_mixer(xb, B, T, w_in_all, j, pos, w1, w2, conv_w, conv_b, gn_g, gn_b):
    N, D = xb.shape
    G, HPG, dh = NSA_KV_HEADS, NSA_HPG, HEAD_DIM
    nsa_w = G * HPG * dh
    kv_cols = 3 * 2 * G * dh
    n_gate = 3 * G * HPG
    conv_c = (w_in_all.shape[2] - nsa_w - kv_cols - n_gate) // 2
    qkv = linear(xb, w_in_all, nsa_w + kv_cols, F32, layer=j, tm=512, tn=(nsa_w + kv_cols) // 2, name="even_qkv")
    w_gl = w_in_all[j, :, nsa_w + kv_cols:nsa_w + kv_cols + n_gate].reshape(D, G, 3 * HPG)
    w_gl = jnp.pad(w_gl, ((0, 0), (0, 0), (0, LANES - 3 * HPG))).reshape(D, G * LANES)
    gl = linear(xb, w_gl, G * LANES, F32, tn=G * LANES, name="even_gates")
    glu = linear(xb, w_in_all[j, :, nsa_w + kv_cols + n_gate:], 2 * conv_c, BF16, tn=1024, name="even_glu")

    qkv3 = qkv.reshape(B, T, nsa_w + kv_cols)
    n_cmp = (T - CMP_LEN) // CMP_STRIDE + 1
    nb = T // CMP_STRIDE
    cmp_end = jnp.arange(nb) * CMP_STRIDE + (CMP_LEN - 1)
    cmp_kv = compress(qkv3, pos, w1, w2, _rope_tables(cmp_end), B, T, G * HPG)
    o_nsa = nsa_attention(qkv3, cmp_kv, gl.reshape(B, T, G * LANES), _rope_tables(jnp.arange(T)), B, T)
    u = conv_module(glu.reshape(B, T, 2 * conv_c), conv_w, conv_b, gn_g, gn_b, B, T)
    return o_nsa.reshape(N, nsa_w), u.reshape(N, conv_c)


def _fox_mixer(xb, B, T, w_in_all, j, f_bias, q_gain, k_gain):
    N, D = xb.shape
    H = D // HEAD_DIM
    proj = linear(xb, w_in_all, 4 * D, BF16, layer=j, tm=512, tn=2048, name="fox_qkvg")
    w_f = jnp.pad(w_in_all[j, :, 4 * D:], ((0, 0), (0, LANES - H)))
    c = fox_decay(xb, w_f, jnp.pad(f_bias, (0, LANES - H)), B, T)[:, :H].reshape(B, T, H)
    return fox_attention(proj.reshape(B, T, 4 * D), c, q_gain, k_gain, B, T, H).reshape(N, D)


def _moe(xf, router_w, router_bias, w_gate, w_up, w_down, layer):
    N, D = xf.shape
    idx, wts = router(xf, router_w, router_bias)
    dest8, cnt = moe_plan(idx)
    dest = dest8[:TOP_K].reshape(-1)
    xs = moe_scatter(xf, dest)
    y = moe_experts(xs, moe_items(cnt[:, 0], TOP_K * N), w_gate, w_up, w_down, layer)
    return y, dest, wts[:TOP_K].T


def kernel(x, even_w_in, even_w_out, nsa_cmp_pos, nsa_cmp_w1, nsa_cmp_w2, conv_w, conv_b, conv_gn_g, conv_gn_b, fox_w_in, fox_f_bias, fox_q_gain, fox_k_gain, fox_w_out, ln_mix_g, ln_mix_b, ln_ffn_g, ln_ffn_b, router_w, router_bias, exp_w_gate, exp_w_up, exp_w_down):
    B, T, D = x.shape
    depth = ln_mix_g.shape[0]
    alpha = (2.0 * depth) ** 0.25
    N = B * T
    xf = x.reshape(N, D)
    xb = xf
    for layer in range(depth):
        j = layer // 2
        if layer % 2 == 0:
            a1, a2 = _even_mixer(xb, B, T, even_w_in, j, nsa_cmp_pos[j], nsa_cmp_w1[j], nsa_cmp_w2[j],
                                 conv_w[j], conv_b[j], conv_gn_g[j], conv_gn_b[j])
            xf, xb = outproj_ln(a1, a2, 0, even_w_out, j, xf, ln_mix_g[layer], ln_mix_b[layer], alpha)
        else:
            a = _fox_mixer(xb, B, T, fox_w_in, j, fox_f_bias[j], fox_q_gain[j], fox_k_gain[j])
            xf, xb = outproj_ln(a, a, 1, fox_w_out, j, xf, ln_mix_g[layer], ln_mix_b[layer], alpha)
        y, dest, wts = _moe(xf, router_w, router_bias, exp_w_gate, exp_w_up, exp_w_down, layer)
        xf, xb = combine_ln(xf, y, dest, wts, ln_ffn_g[layer], ln_ffn_b[layer], alpha)
    return xf.reshape(B, T, D)
```

```python
import functools

import numpy as np
import jax
import jax.numpy as jnp
from jax import lax
from jax.experimental import pallas as pl
from jax.experimental.pallas import tpu as pltpu

F32 = jnp.float32
BF16 = jnp.bfloat16

HEAD_DIM = 128
ROPE_THETA = 500000.0
ROPE_DIM = HEAD_DIM // 4
ROPE_HALF = ROPE_DIM // 2
Q_BLOCK = 128

NSA_KV_HEADS = 2
NSA_HPG = 4
CMP_LEN = 32
CMP_STRIDE = 16
SLC_LEN = 64
N_SLC = 8
WIN = 512
CONV_KERNEL = 31
CONV_GROUPS = 8

N_EXPERTS = 16
N_EXPERT_GROUPS = 4
EXPERTS_PER_GROUP = 4
TOP_K = 2

LN_EPS = 1e-5
NEG = -1e30
BIG = 1e9
TINY = 1e-30

LANES = 128
V7X_VMEM_BYTES = 64 * 1024 * 1024
VMEM_LIMIT = V7X_VMEM_BYTES * 7 // 8

MOE_TM = 256


def _cparams(sem, vmem=VMEM_LIMIT):
    return pltpu.CompilerParams(dimension_semantics=sem, vmem_limit_bytes=vmem)


def _dot(a, b):
    return jnp.dot(a, b, preferred_element_type=F32)


def _dot_nt(a, b):
    return lax.dot_general(a, b, (((1,), (1,)), ((), ())), preferred_element_type=F32)


def _sigmoid(x):
    return 1.0 / (1.0 + jnp.exp(-x))


def _linear_kernel(x_ref, w_ref, o_ref, wb_ref):
    @pl.when(pl.program_id(1) == 0)
    def _():
        wb_ref[...] = w_ref[...].astype(BF16)

    o_ref[...] = _dot(x_ref[...].astype(BF16), wb_ref[...]).astype(o_ref.dtype)


def linear(x, w, n_cols, out_dtype, *, layer=None, tm=1024, tn=512, name="linear"):
    M, K = x.shape
    tm = min(tm, M)
    tn = min(tn, n_cols)
    assert M % tm == 0 and n_cols % tn == 0
    row_block = 0
    if layer is not None:
        w = w.reshape(-1, w.shape[-1])
        row_block = layer
    w_spec = pl.BlockSpec((K, tn), lambda j, i: (row_block, j))
    return pl.pallas_call(
        _linear_kernel,
        out_shape=jax.ShapeDtypeStruct((M, n_cols), out_dtype),
        grid=(n_cols // tn, M // tm),
        in_specs=[
            pl.BlockSpec((tm, K), lambda j, i: (i, 0)),
            w_spec,
        ],
        out_specs=pl.BlockSpec((tm, tn), lambda j, i: (i, j)),
        scratch_shapes=[pltpu.VMEM((K, tn), BF16)],
        compiler_params=_cparams(("arbitrary", "arbitrary")),
        name=name,
    )(x, w)


def _layer_norm_rows(y, g, b):
    mu = jnp.mean(y, axis=-1, keepdims=True)
    d = y - mu
    var = jnp.mean(d * d, axis=-1, keepdims=True)
    return d * lax.rsqrt(var + LN_EPS) * g + b


def _outproj_ln_kernel(a1_ref, a2_ref, w_ref, x_ref, g_ref, b_ref, xo_ref, xb_ref, wb_ref, *, alpha):
    @pl.when(pl.program_id(0) == 0)
    def _():
        wb_ref[...] = w_ref[...].astype(BF16)

    half = a1_ref.shape[1]
    h = _dot(a1_ref[...].astype(BF16), wb_ref[0:half, :])
    h = h + _dot(a2_ref[...].astype(BF16), wb_ref[half:2 * half, :])
    out = _layer_norm_rows(alpha * x_ref[...] + h, g_ref[...], b_ref[...])
    xo_ref[...] = out
    xb_ref[...] = out.astype(BF16)


def outproj_ln(a1, a2, a2_col_block, w, layer, x, g, b, alpha, *, tm=512):
    M, D = x.shape
    half = D // 2
    return pl.pallas_call(
        functools.partial(_outproj_ln_kernel, alpha=alpha),
        out_shape=(jax.ShapeDtypeStruct((M, D), F32), jax.ShapeDtypeStruct((M, D), BF16)),
        grid=(M // tm,),
        in_specs=[
            pl.BlockSpec((tm, half), lambda i: (i, 0)),
            pl.BlockSpec((tm, half), lambda i: (i, a2_col_block)),
            pl.BlockSpec((None, D, D), lambda i: (layer, 0, 0), pipeline_mode=pl.Buffered(1)),
            pl.BlockSpec((tm, D), lambda i: (i, 0)),
            pl.BlockSpec((1, D), lambda i: (0, 0)),
            pl.BlockSpec((1, D), lambda i: (0, 0)),
        ],
        out_specs=(pl.BlockSpec((tm, D), lambda i: (i, 0)), pl.BlockSpec((tm, D), lambda i: (i, 0))),
        scratch_shapes=[pltpu.VMEM((D, D), BF16)],
        compiler_params=_cparams(("arbitrary",)),
        name="outproj_ln",
    )(a1, a2, w, x, g.reshape(1, D), b.reshape(1, D))


ROW_DMA_UNROLL = 8


def _row_copies(dest_ref, n_tok, base, tm, make):
    def body(g, carry):
        r0 = pl.multiple_of(g * ROW_DMA_UNROLL, ROW_DMA_UNROLL)
        for u in range(ROW_DMA_UNROLL):
            for k in range(TOP_K):
                make(r0 + u, k, dest_ref[k * n_tok + base + r0 + u])
        return carry

    lax.fori_loop(0, tm // ROW_DMA_UNROLL, body, 0)


def _combine_ln_kernel(dest_ref, x_ref, w_ref, g_ref, b_ref, y_hbm, xo_ref, xb_ref, buf_ref, sem, *,
                       alpha, tm, n_tok):
    i = pl.program_id(0)
    n_steps = pl.num_programs(0)

    def copies(step, slot, start):
        def make(r, k, d):
            cp = pltpu.make_async_copy(y_hbm.at[pl.ds(d, 1)], buf_ref.at[slot, k, pl.ds(r, 1)], sem.at[slot])
            if start:
                cp.start(priority=k)
            else:
                cp.wait()
        _row_copies(dest_ref, n_tok, step * tm, tm, make)

    @pl.when(i == 0)
    def _():
        copies(0, 0, True)

    @pl.when(i + 1 < n_steps)
    def _():
        copies(i + 1, (i + 1) % 2, True)

    slot = i % 2
    copies(i, slot, False)
    f = w_ref[:, 0:1] * buf_ref[slot, 0] + w_ref[:, 1:2] * buf_ref[slot, 1]
    out = _layer_norm_rows(alpha * x_ref[...] + f, g_ref[...], b_ref[...])
    xo_ref[...] = out
    xb_ref[...] = out.astype(BF16)


def combine_ln(x, y, dest, wts, g, b, alpha, *, tm=256):
    N, D = x.shape
    row = lambda i, d: (i, 0)
    grid_spec = pltpu.PrefetchScalarGridSpec(
        num_scalar_prefetch=1,
        grid=(N // tm,),
        in_specs=[
            pl.BlockSpec((tm, D), row),
            pl.BlockSpec((tm, TOP_K), row),
            pl.BlockSpec((1, D), lambda i, d: (0, 0)),
            pl.BlockSpec((1, D), lambda i, d: (0, 0)),
            pl.BlockSpec(memory_space=pl.ANY),
        ],
        out_specs=(pl.BlockSpec((tm, D), row), pl.BlockSpec((tm, D), row)),
        scratch_shapes=[pltpu.VMEM((2, TOP_K, tm, D), F32), pltpu.SemaphoreType.DMA((2,))],
    )
    return pl.pallas_call(
        functools.partial(_combine_ln_kernel, alpha=alpha, tm=tm, n_tok=N),
        out_shape=(jax.ShapeDtypeStruct((N, D), F32), jax.ShapeDtypeStruct((N, D), BF16)),
        grid_spec=grid_spec,
        compiler_params=_cparams(("arbitrary",)),
        name="moe_combine_ln",
    )(dest, x, wts, g.reshape(1, D), b.reshape(1, D), y)


def _rms(x, gain):
    return x * lax.rsqrt(jnp.mean(x * x, axis=-1, keepdims=True) + LN_EPS) * gain


CUM_CHUNK = 256


def _decay_kernel(fl_ref, bias_ref, c_ref):
    T = fl_ref.shape[0]
    tri = (lax.broadcasted_iota(jnp.int32, (CUM_CHUNK, CUM_CHUNK), 0)
           >= lax.broadcasted_iota(jnp.int32, (CUM_CHUNK, CUM_CHUNK), 1)).astype(F32)
    carry = jnp.zeros((1, LANES), F32)
    for c in range(T // CUM_CHUNK):
        z = fl_ref[c * CUM_CHUNK:(c + 1) * CUM_CHUNK, :] + bias_ref[...]
        log_f = jnp.minimum(z, 0.0) - jnp.log1p(jnp.exp(-jnp.abs(z)))
        cs = jnp.dot(tri, log_f, preferred_element_type=F32, precision=lax.Precision.HIGHEST) + carry
        c_ref[c * CUM_CHUNK:(c + 1) * CUM_CHUNK, :] = cs
        carry = cs[CUM_CHUNK - 1:CUM_CHUNK, :]


def fox_decay(fl, bias, B, T):
    assert T % CUM_CHUNK == 0
    return pl.pallas_call(
        _decay_kernel,
        out_shape=jax.ShapeDtypeStruct(fl.shape, F32),
        grid=(B,),
        in_specs=[pl.BlockSpec((T, LANES), lambda b: (b, 0)), pl.BlockSpec((1, LANES), lambda b: (0, 0))],
        out_specs=pl.BlockSpec((T, LANES), lambda b: (b, 0)),
        compiler_params=_cparams(("arbitrary",)),
        name="fox_decay",
    )(fl, bias.reshape(1, LANES))


LOG2E = 1.4426950408889634
FAST_SOFTMAX_BOUND = 38.0
BIAS_PIECES = 3


def _split3(a):
    hi = a.astype(BF16).astype(F32)
    r = a - hi
    mid = r.astype(BF16).astype(F32)
    return hi, mid, (r - mid).astype(BF16).astype(F32)


def _aux_lanes(pieces, ones_first, n):
    lane = lax.broadcasted_iota(jnp.int32, (n, LANES), 1)
    p0 = BIAS_PIECES if ones_first else 0
    o0 = 0 if ones_first else BIAS_PIECES
    out = jnp.where((lane >= o0) & (lane < o0 + BIAS_PIECES), 1.0, 0.0)
    for i, piece in enumerate(pieces):
        out = out + jnp.where(lane == p0 + i, piece, 0.0)
    return out


def _fox_kernel(fast_ref, q_ref, k_ref, v_ref, og_ref, ckrow_ref, ckcol_ref, cq_ref, bound_ref, qg_ref, kg_ref,
                o_ref, kx_ref, vx_ref, *, tq, tk, hb):
    qi = pl.program_id(2)
    dh = HEAD_DIM
    T = k_ref.shape[1]

    def piece_by_lane(a, lane):
        hi, mid, lo = _split3(a)
        in_group = lambda g: ((lane >= g * hb) & (lane < (g + 1) * hb)) | (
            (lane >= (BIAS_PIECES + g) * hb) & (lane < (BIAS_PIECES + g + 1) * hb))
        return jnp.where(in_group(0), hi, jnp.where(in_group(1), mid, lo))

    @pl.when(qi == 0)
    def _():
        lane_k = lax.broadcasted_iota(jnp.int32, (T, LANES), 1)
        k_aux = jnp.where(lane_k < BIAS_PIECES * hb, piece_by_lane(-LOG2E * ckcol_ref[0, 0], lane_k),
                          jnp.where(lane_k < 2 * BIAS_PIECES * hb, 1.0, 0.0)).astype(BF16)
        for h in range(hb):
            kx_ref[h, :, 0:dh] = _rms(k_ref[0, :, h * dh:(h + 1) * dh].astype(F32), kg_ref[...]).astype(BF16)
            kx_ref[h, :, dh:2 * dh] = k_aux
            vx_ref[h, :, 0:dh] = v_ref[0, :, h * dh:(h + 1) * dh].astype(BF16)
            vx_ref[h, :, dh:2 * dh] = jnp.ones((T, dh), BF16)

    row = qi * tq + lax.broadcasted_iota(jnp.int32, (tq, tk), 0)
    lane = lax.broadcasted_iota(jnp.int32, (tq, tk), 1)
    n_full = (qi * tq) // tk

    def finish(accs):
        for h in range(hb):
            o = accs[h][:, 0:dh] / jnp.maximum(accs[h][:, dh:2 * dh], TINY)
            gate = _sigmoid(og_ref[0, :, h * dh:(h + 1) * dh].astype(F32))
            o_ref[0, :, h * dh:(h + 1) * dh] = (o * gate).astype(o_ref.dtype)

    def q_normed(h):
        return _rms(q_ref[0, :, h * dh:(h + 1) * dh].astype(F32), qg_ref[...]) * (dh ** -0.5)

    @pl.when(fast_ref[0] == 1)
    def _():
        lane_q = lax.broadcasted_iota(jnp.int32, (tq, LANES), 1)
        r = piece_by_lane(LOG2E * (cq_ref[0, 0] - bound_ref[0:1, 0:1]), lane_q)
        qx = []
        for h in range(hb):
            mine = (lane_q & (hb - 1)) == h
            q_aux = jnp.where(mine & (lane_q < BIAS_PIECES * hb), 1.0,
                              jnp.where(mine & (lane_q < 2 * BIAS_PIECES * hb), r, 0.0))
            qx.append(jnp.concatenate([(q_normed(h) * LOG2E).astype(BF16), q_aux.astype(BF16)], axis=1))

        def step(j, accs, masked):
            start = pl.multiple_of(j * tk, tk)
            out = []
            for h in range(hb):
                s = _dot_nt(qx[h], kx_ref[h, pl.ds(start, tk), :])
                if masked:
                    s = jnp.where(j * tk + lane <= row, s, NEG)
                out.append(accs[h] + _dot(jnp.exp2(s).astype(BF16), vx_ref[h, pl.ds(start, tk), :]))
            return tuple(out)

        def diagonal(accs):
            hq = tq // 2
            start = pl.multiple_of(n_full * tk, tk)
            tri = (lax.broadcasted_iota(jnp.int32, (hq, hq), 1) <= lax.broadcasted_iota(jnp.int32, (hq, hq), 0))
            out = []
            for h in range(hb):
                k_lo, k_hi = kx_ref[h, pl.ds(start, hq), :], kx_ref[h, pl.ds(start + hq, hq), :]
                v_lo, v_all = vx_ref[h, pl.ds(start, hq), :], vx_ref[h, pl.ds(start, tk), :]
                p_top = jnp.exp2(jnp.where(tri, _dot_nt(qx[h][0:hq], k_lo), NEG)).astype(BF16)
                p_bot = jnp.concatenate(
                    [jnp.exp2(_dot_nt(qx[h][hq:tq], k_lo)).astype(BF16),
                     jnp.exp2(jnp.where(tri, _dot_nt(qx[h][hq:tq], k_hi), NEG)).astype(BF16)], axis=1)
                out.append(accs[h] + jnp.concatenate([_dot(p_top, v_lo), _dot(p_bot, v_all)], axis=0))
            return out

        accs = tuple(jnp.zeros((tq, 2 * dh), F32) for _ in range(hb))
        accs = lax.fori_loop(0, n_full, lambda j, a: step(j, a, False), accs)
        finish(diagonal(accs) if tq == tk else step(n_full, accs, True))

    @pl.when(fast_ref[0] == 0)
    def _():
        qs = [q_normed(h).astype(BF16) for h in range(hb)]

        def step(j, carry, masked):
            start = pl.multiple_of(j * tk, tk)
            out = []
            for h in range(hb):
                m, acc = carry[h]
                s = _dot_nt(qs[h], kx_ref[h, pl.ds(start, tk), 0:dh]) - ckrow_ref[0, h, pl.ds(j, 1), :]
                if masked:
                    s = jnp.where(j * tk + lane <= row, s, NEG)
                m_new = jnp.maximum(m, jnp.max(s, axis=-1, keepdims=True))
                p = jnp.exp(s - m_new).astype(BF16)
                acc = jnp.exp(m - m_new) * acc + _dot(p, vx_ref[h, pl.ds(start, tk), :])
                out.append((m_new, acc))
            return tuple(out)

        init = tuple((jnp.full((tq, 1), NEG, F32), jnp.zeros((tq, 2 * dh), F32)) for _ in range(hb))
        carry = lax.fori_loop(0, n_full, lambda j, c: step(j, c, False), init)
        finish([c[1] for c in step(n_full, carry, True)])


def fox_attention(proj, c, q_gain, k_gain, B, T, H, *, tq=512, tk=512, hb=4):
    nq = T // tq
    nk = T // tk
    dh = HEAD_DIM
    hg = H // hb
    c_row = c.transpose(0, 2, 1).reshape(B, H, nk, tk)
    assert hb & (hb - 1) == 0 and 2 * BIAS_PIECES * hb <= LANES
    c_col = c.reshape(B, T, hg, hb).transpose(0, 2, 1, 3)
    c_col = jnp.concatenate([jnp.tile(c_col, (1, 1, 1, 2 * BIAS_PIECES)),
                             jnp.zeros((B, hg, T, LANES - 2 * BIAS_PIECES * hb), F32)], axis=-1)
    bound = jnp.max(jnp.abs(q_gain)) * jnp.max(jnp.abs(k_gain)) * (dh ** 0.5) * 1.01
    fast = (bound <= FAST_SOFTMAX_BOUND).astype(jnp.int32).reshape(1)
    full = lambda off: pl.BlockSpec((1, T, hb * dh), lambda b, h, i, f: (b, 0, off + h))
    tile = lambda off: pl.BlockSpec((1, tq, hb * dh), lambda b, h, i, f: (b, i, off + h))
    vec = pl.BlockSpec((1, dh), lambda b, h, i, f: (0, 0))
    grid_spec = pltpu.PrefetchScalarGridSpec(
        num_scalar_prefetch=1,
        grid=(B, hg, nq),
        in_specs=[
            tile(0), full(hg), full(2 * hg), tile(3 * hg),
            pl.BlockSpec((1, hb, nk, tk), lambda b, h, i, f: (b, h, 0, 0)),
            pl.BlockSpec((1, 1, T, LANES), lambda b, h, i, f: (b, h, 0, 0)),
            pl.BlockSpec((1, 1, tq, LANES), lambda b, h, i, f: (b, h, i, 0)),
            vec, vec, vec,
        ],
        out_specs=pl.BlockSpec((1, tq, hb * dh), lambda b, h, i, f: (b, i, h)),
        scratch_shapes=[pltpu.VMEM((hb, T, 2 * dh), BF16), pltpu.VMEM((hb, T, 2 * dh), BF16)],
    )
    return pl.pallas_call(
        functools.partial(_fox_kernel, tq=tq, tk=tk, hb=hb),
        out_shape=jax.ShapeDtypeStruct((B, T, H * dh), BF16),
        grid_spec=grid_spec,
        compiler_params=_cparams(("arbitrary", "arbitrary", "arbitrary")),
        name="fox_attention",
    )(fast, proj, proj, proj, proj, c_row, c_col, c_col, jnp.full((1, dh), bound, F32),
      q_gain.reshape(1, dh), k_gain.reshape(1, dh))


def _rope(x, cos, sin_lo, sin_hi):
    return (x * cos + pltpu.roll(x, LANES - ROPE_HALF, 1) * sin_lo
            + pltpu.roll(x, ROPE_HALF, 1) * sin_hi)


def _gelu_tanh(x):
    return 0.5 * x * (1.0 + jnp.tanh(0.7978845608028654 * (x + 0.044715 * (x * x * x))))


def _compress_kernel(raw_ref, pos_ref, w1_ref, w2_ref, cos_ref, slo_ref, shi_ref, o_ref, *, n_cmp):
    nb = o_ref.shape[-2]
    half = CMP_LEN // 2
    acc_a = jnp.zeros((nb, HEAD_DIM), F32)
    acc_b = jnp.zeros((nb, HEAD_DIM), F32)
    for l in range(half):
        rl = raw_ref[0, pl.ds(l, nb, stride=CMP_STRIDE), :]
        wa = w1_ref[0, l * HEAD_DIM:(l + 1) * HEAD_DIM, :].astype(BF16)
        wb = w1_ref[0, (half + l) * HEAD_DIM:(half + l + 1) * HEAD_DIM, :].astype(BF16)
        acc_a = acc_a + _dot((rl + pos_ref[0, l:l + 1, :]).astype(BF16), wa)
        acc_b = acc_b + _dot((rl + pos_ref[0, half + l:half + l + 1, :]).astype(BF16), wb)
    pre = acc_a + pltpu.roll(acc_b, nb - 1, 0)
    out = _dot(_gelu_tanh(pre).astype(BF16), w2_ref[0].astype(BF16))
    roped = _rope(out, cos_ref[...], slo_ref[...], shi_ref[...])
    out = jnp.where(pl.program_id(1) == 0, roped, out)
    rows = lax.broadcasted_iota(jnp.int32, out.shape, 0)
    o_ref[0, 0, 0] = jnp.where(rows < n_cmp, out, 0.0)


def compress(qkv, pos, w1, w2, tabs, B, T, kv_block0):
    G = NSA_KV_HEADS
    nb = T // CMP_STRIDE
    n_cmp = (T - CMP_LEN) // CMP_STRIDE + 1
    cos, slo, shi = tabs
    tab = pl.BlockSpec((nb, HEAD_DIM), lambda b, kv, g: (0, 0))
    return pl.pallas_call(
        functools.partial(_compress_kernel, n_cmp=n_cmp),
        out_shape=jax.ShapeDtypeStruct((B, 2, G, nb, HEAD_DIM), F32),
        grid=(B, 2, G),
        in_specs=[
            pl.BlockSpec((1, T, HEAD_DIM), lambda b, kv, g: (b, 0, kv_block0 + kv * G + g)),
            pl.BlockSpec((1, CMP_LEN, HEAD_DIM), lambda b, kv, g: (kv, 0, 0)),
            pl.BlockSpec((1, CMP_LEN * HEAD_DIM, HEAD_DIM), lambda b, kv, g: (kv, 0, 0)),
            pl.BlockSpec((1, HEAD_DIM, HEAD_DIM), lambda b, kv, g: (kv, 0, 0)),
            tab, tab, tab,
        ],
        out_specs=pl.BlockSpec((1, 1, 1, nb, HEAD_DIM), lambda b, kv, g: (b, kv, g, 0, 0)),
        compiler_params=_cparams(("arbitrary", "arbitrary", "arbitrary")),
        name="nsa_compress",
    )(qkv, pos, w1, w2, cos, slo, shi)


SLC_CHUNK = 512
WIN_SPAN = WIN + Q_BLOCK
NSA_QB = 4


def _select_blocks(imp, qi, n_slc):
    QB = Q_BLOCK
    n_rows = -(-n_slc // 8) * 8
    imp_t = jnp.transpose(imp)[0:n_rows]
    blk = lax.broadcasted_iota(jnp.int32, (n_rows, QB), 0)
    t = qi * QB + lax.broadcasted_iota(jnp.int32, (n_rows, QB), 1)
    cur = jnp.right_shift(t, 6)
    forced = (blk == 0) | (blk == cur) | (blk == cur - 1)
    score = jnp.where(forced, BIG, jnp.where(blk * SLC_LEN <= t, imp_t, -BIG))
    score = jnp.where(blk < n_slc, score, -2.0 * BIG)
    cnt = jnp.zeros((n_rows, QB), F32)
    for jp in range(n_slc):
        r = score[jp:jp + 1, :]
        cnt = cnt + ((r > score) | ((r == score) & (jp < blk))).astype(F32)
    sel_t = ((cnt < float(min(N_SLC, n_slc))) & (blk < n_slc)).astype(F32)
    if n_rows < LANES:
        sel_t = jnp.concatenate([sel_t, jnp.zeros((LANES - n_rows, QB), F32)], axis=0)
    return jnp.transpose(sel_t).astype(BF16)


def _nsa_kernel(q_ref, ks_ref, vs_ref, kw_ref, vw_ref, kc_ref, vc_ref, gl_ref,
                cq_ref, sloq_ref, shiq_ref, ck_ref, slok_ref, shik_ref, c2s_ref, ex_ref,
                o_ref, ksx_ref, vsx_ref, kwb_ref, vwx_ref, bias_ref, kmax_ref, acc_ref, *, n_slc, n_cmp):
    step = pl.program_id(2)
    QB, HPG, dh = Q_BLOCK, NSA_HPG, HEAD_DIM
    T = ks_ref.shape[1]

    @pl.when(step == 0)
    def _():
        ks = _rope(ks_ref[0], ck_ref[...], slok_ref[...], shik_ref[...])
        ksx_ref[:, 0:dh] = ks.astype(BF16)
        ksx_ref[:, dh:2 * dh] = _aux_lanes((), False, T).astype(BF16)
        kmax_ref[...] = jnp.full(kmax_ref.shape, jnp.max(jnp.sum(ks * ks, axis=-1, keepdims=True)), F32)
        kwb_ref[...] = _rope(kw_ref[0], ck_ref[...], slok_ref[...], shik_ref[...]).astype(BF16)
        ones = jnp.ones((T, dh), BF16)
        vsx_ref[:, 0:dh] = vs_ref[0].astype(BF16)
        vsx_ref[:, dh:2 * dh] = ones
        vwx_ref[:, 0:dh] = vw_ref[0].astype(BF16)
        vwx_ref[:, dh:2 * dh] = ones

    qis = [step * NSA_QB + b for b in range(NSA_QB)]
    rows = lambda b, h: slice((b * HPG + h) * QB, (b * HPG + h + 1) * QB)
    qrows = lambda b: slice(b * QB, (b + 1) * QB)
    pieces, norms2 = [], []
    for b in range(NSA_QB):
        for h in range(HPG):
            x = q_ref[0, qrows(b), h * dh:(h + 1) * dh]
            x = _rope(x, cq_ref[qrows(b), :], sloq_ref[qrows(b), :], shiq_ref[qrows(b), :]) * (dh ** -0.5)
            pieces.append(x)
            norms2.append(_dot((x * x).astype(BF16), jnp.ones((dh, LANES), BF16)))
    q = jnp.concatenate(pieces, axis=0).astype(BF16)

    lane = lax.broadcasted_iota(jnp.int32, (QB, LANES), 1)
    sub = lax.broadcasted_iota(jnp.int32, (QB, LANES), 0)

    kc = kc_ref[0, 0, 0].astype(BF16)
    vc = vc_ref[0, 0, 0].astype(BF16)
    sc = _dot_nt(q, kc)
    o_cmp, sels = {}, []
    for b in range(NSA_QB):
        t = qis[b] * QB + sub
        mask_c = (lane * CMP_STRIDE + (CMP_LEN - 1) <= t) & (lane < n_cmp)
        mask_cf = mask_c.astype(F32)
        imp_c = jnp.zeros((QB, LANES), F32)
        for h in range(HPG):
            s = jnp.where(mask_c, sc[rows(b, h)], NEG)
            e = jnp.exp(s - jnp.max(s, axis=-1, keepdims=True)) * mask_cf
            p = e / jnp.maximum(jnp.sum(e, axis=-1, keepdims=True), TINY)
            imp_c = imp_c + p
            o_cmp[b, h] = _dot(p.astype(BF16), vc)
        imp = jnp.dot(imp_c, c2s_ref[...], preferred_element_type=F32, precision=lax.Precision.HIGHEST)
        sels.append(_select_blocks(imp, qis[b], n_slc))

    sel_all = jnp.concatenate(sels, axis=0)
    for c in range(T // SLC_CHUNK):
        selx = _dot(sel_all, ex_ref[:, c * SLC_CHUNK:(c + 1) * SLC_CHUNK])
        kpos = c * SLC_CHUNK + lax.broadcasted_iota(jnp.int32, (QB, SLC_CHUNK), 1)
        for b in range(NSA_QB):
            tq_ = qis[b] * QB + lax.broadcasted_iota(jnp.int32, (QB, SLC_CHUNK), 0)
            bias_ref[b, c] = jnp.where((selx[qrows(b)] > 0.5) & (kpos <= tq_), 0.0, NEG)

    o_win = {}
    for b in range(NSA_QB):
        wstart = pl.multiple_of(jnp.clip(qis[b] * QB - WIN, 0, T - WIN_SPAN), Q_BLOCK)
        qb = q[b * HPG * QB:(b + 1) * HPG * QB]
        sw = _dot_nt(qb, kwb_ref[pl.ds(wstart, WIN_SPAN), :])
        kpos = wstart + lax.broadcasted_iota(jnp.int32, (QB, WIN_SPAN), 1)
        tw = qis[b] * QB + lax.broadcasted_iota(jnp.int32, (QB, WIN_SPAN), 0)
        bias_w = jnp.where((kpos <= tw) & (kpos > tw - WIN), 0.0, NEG)
        pw = []
        for h in range(HPG):
            sh = sw[h * QB:(h + 1) * QB] + bias_w
            pw.append(jnp.exp(sh - jnp.max(sh, axis=-1, keepdims=True)).astype(BF16))
        ow = _dot(jnp.concatenate(pw, axis=0), vwx_ref[pl.ds(wstart, WIN_SPAN), :])
        for h in range(HPG):
            acc_w = ow[h * QB:(h + 1) * QB]
            o_win[b, h] = acc_w[:, 0:dh] / jnp.maximum(acc_w[:, dh:2 * dh], TINY)

    chains = [(b, h) for b in range(NSA_QB) for h in range(HPG)]

    n_chunks = (qis[-1] * QB + QB + SLC_CHUNK - 1) // SLC_CHUNK
    kmax2 = kmax_ref[0:1, 0:1]
    bounds = [jnp.sqrt(n2 * kmax2) * 1.02 for n2 in norms2]
    worst = functools.reduce(jnp.maximum, [jnp.max(bd) for bd in bounds])

    @pl.when(worst <= FAST_SOFTMAX_BOUND)
    def _():
        qx = jnp.concatenate(
            [jnp.concatenate([(x * LOG2E).astype(BF16),
                              _aux_lanes(_split3(-LOG2E * bd), True, QB).astype(BF16)], axis=1)
             for x, bd in zip(pieces, bounds)], axis=0)

        def body(c, accs):
            start = pl.multiple_of(c * SLC_CHUNK, SLC_CHUNK)
            s = _dot_nt(qx, ksx_ref[pl.ds(start, SLC_CHUNK), :])
            ps = [jnp.exp2(s[rows(b, h)] + bias_ref[b, c]).astype(BF16) for (b, h) in chains]
            pv = _dot(jnp.concatenate(ps, axis=0), vsx_ref[pl.ds(start, SLC_CHUNK), :])
            return tuple(accs[n] + pv[rows(b, h)] for n, (b, h) in enumerate(chains))

        accs = lax.fori_loop(0, n_chunks, body, tuple(jnp.zeros((QB, 2 * dh), F32) for _ in chains))
        for n, (b, h) in enumerate(chains):
            acc_ref[rows(b, h), :] = accs[n]

    @pl.when(worst > FAST_SOFTMAX_BOUND)
    def _():
        def body(c, carry):
            start = pl.multiple_of(c * SLC_CHUNK, SLC_CHUNK)
            s = _dot_nt(q, ksx_ref[pl.ds(start, SLC_CHUNK), 0:dh])
            ms, alphas, ps = [], [], []
            for n, (b, h) in enumerate(chains):
                sh = s[rows(b, h)] + bias_ref[b, c]
                m_new = jnp.maximum(carry[n][0], jnp.max(sh, axis=-1, keepdims=True))
                alphas.append(jnp.exp(carry[n][0] - m_new))
                ps.append(jnp.exp(sh - m_new).astype(BF16))
                ms.append(m_new)
            pv = _dot(jnp.concatenate(ps, axis=0), vsx_ref[pl.ds(start, SLC_CHUNK), :])
            return tuple((ms[n], alphas[n] * carry[n][1] + pv[rows(b, h)]) for n, (b, h) in enumerate(chains))

        init = tuple((jnp.full((QB, 1), NEG, F32), jnp.zeros((QB, 2 * dh), F32)) for _ in chains)
        slc = lax.fori_loop(0, n_chunks, body, init)
        for n, (b, h) in enumerate(chains):
            acc_ref[rows(b, h), :] = slc[n][1]

    for b in range(NSA_QB):
        gates = _sigmoid(gl_ref[0, qrows(b), :])
        for h in range(HPG):
            acc_s = acc_ref[rows(b, h), :]
            o_s = acc_s[:, 0:dh] / jnp.maximum(acc_s[:, dh:2 * dh], TINY)
            out = (gates[:, 3 * h:3 * h + 1] * o_cmp[b, h] + gates[:, 3 * h + 1:3 * h + 2] * o_s
                   + gates[:, 3 * h + 2:3 * h + 3] * o_win[b, h])
            o_ref[0, qrows(b), h * dh:(h + 1) * dh] = out.astype(o_ref.dtype)


def nsa_attention(qkv, cmp_kv, gl, tabs_q, B, T):
    G, HPG, dh = NSA_KV_HEADS, NSA_HPG, HEAD_DIM
    tq = NSA_QB * Q_BLOCK
    nb = T // CMP_STRIDE
    n_cmp = (T - CMP_LEN) // CMP_STRIDE + 1
    n_slc = T // SLC_LEN
    assert nb == LANES and n_slc <= LANES and T % SLC_CHUNK == 0 and T >= WIN_SPAN and T % tq == 0
    kvb = (HPG * G)

    c_start = np.arange(nb) * CMP_STRIDE
    s_start = np.arange(LANES) * SLC_LEN
    c2s = ((c_start[:, None] < s_start[None, :] + SLC_LEN) & (c_start[:, None] + CMP_LEN > s_start[None, :])
           & (np.arange(nb)[:, None] < n_cmp) & (np.arange(LANES)[None, :] < n_slc)).astype(np.float32)
    expand = (np.arange(T)[None, :] // SLC_LEN == np.arange(LANES)[:, None]).astype(np.float32)

    cos, slo, shi = tabs_q
    kv_full = lambda blk: pl.BlockSpec((1, T, dh), lambda b, g, i: (b, 0, kvb + blk + g))
    qtab = pl.BlockSpec((tq, dh), lambda b, g, i: (i, 0))
    ktab = pl.BlockSpec((T, dh), lambda b, g, i: (0, 0))
    return pl.pallas_call(
        functools.partial(_nsa_kernel, n_slc=n_slc, n_cmp=n_cmp),
        out_shape=jax.ShapeDtypeStruct((B, T, G * HPG * dh), BF16),
        grid=(B, G, T // tq),
        in_specs=[
            pl.BlockSpec((1, tq, HPG * dh), lambda b, g, i: (b, i, g)),
            kv_full(2 * G), kv_full(3 * G), kv_full(4 * G), kv_full(5 * G),
            pl.BlockSpec((1, 1, 1, nb, dh), lambda b, g, i: (b, 0, g, 0, 0)),
            pl.BlockSpec((1, 1, 1, nb, dh), lambda b, g, i: (b, 1, g, 0, 0)),
            pl.BlockSpec((1, tq, LANES), lambda b, g, i: (b, i, g)),
            qtab, qtab, qtab, ktab, ktab, ktab,
            pl.BlockSpec((nb, LANES), lambda b, g, i: (0, 0)),
            pl.BlockSpec((LANES, T), lambda b, g, i: (0, 0)),
        ],
        out_specs=pl.BlockSpec((1, tq, HPG * dh), lambda b, g, i: (b, i, g)),
        scratch_shapes=[pltpu.VMEM((T, 2 * dh), BF16), pltpu.VMEM((T, 2 * dh), BF16),
                        pltpu.VMEM((T, dh), BF16), pltpu.VMEM((T, 2 * dh), BF16),
                        pltpu.VMEM((NSA_QB, T // SLC_CHUNK, Q_BLOCK, SLC_CHUNK), F32),
                        pltpu.VMEM((8, LANES), F32),
                        pltpu.VMEM((NSA_QB * HPG * Q_BLOCK, 2 * dh), F32)],
        compiler_params=_cparams(("arbitrary", "arbitrary", "arbitrary")),
        name="nsa_attention",
    )(qkv, qkv, qkv, qkv, qkv, cmp_kv, cmp_kv, gl, cos, slo, shi, cos, slo, shi,
      jnp.asarray(c2s), jnp.asarray(expand, dtype=BF16))


CONV_PAD = 32
CONV_CHUNK = 256


def _conv_kernel(a_ref, b_ref, w_ref, cb_ref, g_ref, gb_ref, o_ref, u_ref):
    T = a_ref.shape[1]
    u_ref[0:CONV_PAD, :] = jnp.zeros((CONV_PAD, LANES), F32)
    u_ref[CONV_PAD:CONV_PAD + T, :] = a_ref[0].astype(F32) * _sigmoid(b_ref[0].astype(F32))
    base = CONV_PAD - (CONV_KERNEL - 1)
    for c in range(T // CONV_CHUNK):
        t0 = c * CONV_CHUNK
        acc = jnp.zeros((CONV_CHUNK, LANES), F32)
        for k in range(CONV_KERNEL):
            acc = acc + u_ref[t0 + base + k:t0 + base + k + CONV_CHUNK, :] * w_ref[k:k + 1, :]
        acc = acc + cb_ref[...]
        mu = jnp.mean(acc, axis=-1, keepdims=True)
        d = acc - mu
        var = jnp.mean(d * d, axis=-1, keepdims=True)
        y = d * lax.rsqrt(var + LN_EPS) * g_ref[...] + gb_ref[...]
        o_ref[0, t0:t0 + CONV_CHUNK, :] = (y * _sigmoid(y)).astype(o_ref.dtype)


def conv_module(glu, conv_w, conv_b, gn_g, gn_b, B, T):
    C = glu.shape[-1] // 2
    ng = C // LANES
    assert C // CONV_GROUPS == LANES
    vec = pl.BlockSpec((1, LANES), lambda b, g: (0, g))
    return pl.pallas_call(
        _conv_kernel,
        out_shape=jax.ShapeDtypeStruct((B, T, C), BF16),
        grid=(B, ng),
        in_specs=[
            pl.BlockSpec((1, T, LANES), lambda b, g: (b, 0, g)),
            pl.BlockSpec((1, T, LANES), lambda b, g: (b, 0, ng + g)),
            pl.BlockSpec((CONV_KERNEL, LANES), lambda b, g: (0, g)),
            vec, vec, vec,
        ],
        out_specs=pl.BlockSpec((1, T, LANES), lambda b, g: (b, 0, g)),
        scratch_shapes=[pltpu.VMEM((CONV_PAD + T, LANES), F32)],
        compiler_params=_cparams(("arbitrary", "arbitrary")),
        name="conformer_conv",
    )(glu, glu, conv_w, conv_b.reshape(1, C), gn_g.reshape(1, C), gn_b.reshape(1, C))


def _split2(a):
    hi = a.astype(BF16)
    return hi, (a - hi.astype(F32)).astype(BF16)


def _router_kernel(x_ref, w_ref, bias_ref, idx_ref, wts_ref):
    xh, xl = _split2(x_ref[...])
    wh, wl = _split2(w_ref[...])
    logits = _dot(xh, wh) + (_dot(xh, wl) + _dot(xl, wh))
    aff = _sigmoid(jnp.concatenate(
        [jnp.transpose(logits[c * LANES:(c + 1) * LANES])[0:N_EXPERTS] for c in range(logits.shape[0] // LANES)],
        axis=1))
    sel = aff + bias_ref[...]
    a = [aff[e:e + 1, :] for e in range(N_EXPERTS)]
    s = [sel[e:e + 1, :] for e in range(N_EXPERTS)]
    P = EXPERTS_PER_GROUP
    grp = []
    for g in range(N_EXPERT_GROUPS):
        v = s[g * P:(g + 1) * P]
        best = None
        for i in range(P):
            for j in range(i + 1, P):
                pair = v[i] + v[j]
                best = pair if best is None else jnp.maximum(best, pair)
        grp.append(best)
    gbest = jnp.zeros_like(grp[0], dtype=jnp.int32)
    gval = grp[0]
    for g in range(1, N_EXPERT_GROUPS):
        better = grp[g] > gval
        gbest = jnp.where(better, g, gbest)
        gval = jnp.where(better, grp[g], gval)
    cs, ca = [], []
    for p in range(P):
        sv, av = s[p], a[p]
        for g in range(1, N_EXPERT_GROUPS):
            sv = jnp.where(gbest == g, s[g * P + p], sv)
            av = jnp.where(gbest == g, a[g * P + p], av)
        cs.append(sv)
        ca.append(av)
    i1 = jnp.zeros_like(gbest)
    v1, a1 = cs[0], ca[0]
    for p in range(1, P):
        better = cs[p] > v1
        i1 = jnp.where(better, p, i1)
        v1 = jnp.where(better, cs[p], v1)
        a1 = jnp.where(better, ca[p], a1)
    i2 = jnp.full_like(gbest, -1)
    v2 = jnp.full_like(v1, -jnp.inf)
    a2 = jnp.zeros_like(a1)
    for p in range(P):
        better = (i1 != p) & (cs[p] > v2)
        i2 = jnp.where(better, p, i2)
        v2 = jnp.where(better, cs[p], v2)
        a2 = jnp.where(better, ca[p], a2)
    den = a1 + a2
    idx_ref[...] = jnp.zeros(idx_ref.shape, jnp.int32)
    wts_ref[...] = jnp.zeros(wts_ref.shape, F32)
    idx_ref[0:1, :] = gbest * P + i1
    idx_ref[1:2, :] = gbest * P + i2
    wts_ref[0:1, :] = a1 / den
    wts_ref[1:2, :] = a2 / den


def router(x, router_w, router_bias, *, tm=1024):
    N, D = x.shape
    E = N_EXPERTS
    return pl.pallas_call(
        _router_kernel,
        out_shape=(jax.ShapeDtypeStruct((8, N), jnp.int32), jax.ShapeDtypeStruct((8, N), F32)),
        grid=(N // tm,),
        in_specs=[
            pl.BlockSpec((tm, D), lambda i: (i, 0)),
            pl.BlockSpec((D, LANES), lambda i: (0, 0)),
            pl.BlockSpec((E, 1), lambda i: (0, 0)),
        ],
        out_specs=(pl.BlockSpec((8, tm), lambda i: (0, i)), pl.BlockSpec((8, tm), lambda i: (0, i))),
        compiler_params=_cparams(("arbitrary",)),
        name="moe_router",
    )(x, jnp.pad(router_w, ((0, 0), (0, LANES - E))), router_bias.reshape(E, 1))


PLAN_CHUNK = 512


def _plan_kernel(idx_ref, dest_ref, cnt_ref, *, n_tok):
    E = N_EXPERTS
    sub = lax.broadcasted_iota(jnp.int32, (TOP_K * E, n_tok), 0)
    tgt = jnp.where(sub < E, idx_ref[0:1, :], idx_ref[1:2, :])
    onehot = ((sub & (E - 1)) == tgt).astype(F32)
    onehot_b = onehot.astype(BF16)
    tri = (lax.broadcasted_iota(jnp.int32, (PLAN_CHUNK, PLAN_CHUNK), 0)
           <= lax.broadcasted_iota(jnp.int32, (PLAN_CHUNK, PLAN_CHUNK), 1)).astype(F32).astype(BF16)
    carry = jnp.zeros((TOP_K * E, 1), F32)
    parts = []
    for c in range(n_tok // PLAN_CHUNK):
        pre = _dot(onehot_b[:, c * PLAN_CHUNK:(c + 1) * PLAN_CHUNK], tri) + carry
        parts.append(pre)
        carry = pre[:, PLAN_CHUNK - 1:PLAN_CHUNK]
    excl = jnp.concatenate(parts, axis=1) - onehot
    cnt0 = carry[0:E]
    tot = cnt0 + carry[E:2 * E]
    lower = (lax.broadcasted_iota(jnp.int32, (E, E), 1)
             < lax.broadcasted_iota(jnp.int32, (E, E), 0)).astype(F32)
    offs = jnp.dot(lower, jnp.broadcast_to(tot, (E, LANES)), preferred_element_type=F32,
                   precision=lax.Precision.HIGHEST)[:, 0:1]
    base = jnp.concatenate([offs, offs + cnt0], axis=0)
    val = onehot * (base + excl)
    dest_ref[...] = jnp.zeros(dest_ref.shape, jnp.int32)
    dest_ref[0:1, :] = jnp.sum(val[0:E], axis=0, keepdims=True).astype(jnp.int32)
    dest_ref[1:2, :] = jnp.sum(val[E:2 * E], axis=0, keepdims=True).astype(jnp.int32)
    cnt_ref[...] = jnp.broadcast_to(tot, (E, LANES)).astype(jnp.int32)


def moe_plan(idx):
    n_tok = idx.shape[1]
    assert n_tok % PLAN_CHUNK == 0 and TOP_K == 2
    return pl.pallas_call(
        functools.partial(_plan_kernel, n_tok=n_tok),
        out_shape=(jax.ShapeDtypeStruct((8, n_tok), jnp.int32),
                   jax.ShapeDtypeStruct((N_EXPERTS, LANES), jnp.int32)),
        compiler_params=_cparams(None),
        name="moe_plan",
    )(idx)


def _scatter_kernel(dest_ref, x_ref, xs_hbm, sem, *, tm, n_tok):
    base = pl.program_id(0) * tm

    def copies(start):
        def make(r, k, d):
            cp = pltpu.make_async_copy(x_ref.at[pl.ds(r, 1)], xs_hbm.at[pl.ds(d, 1)], sem)
            if start:
                cp.start(priority=k)
            else:
                cp.wait()
        _row_copies(dest_ref, n_tok, base, tm, make)

    copies(True)
    copies(False)


def moe_scatter(x, dest, *, tm=1024):
    N, D = x.shape
    grid_spec = pltpu.PrefetchScalarGridSpec(
        num_scalar_prefetch=1,
        grid=(N // tm,),
        in_specs=[pl.BlockSpec((tm, D), lambda i, d: (i, 0))],
        out_specs=pl.BlockSpec(memory_space=pl.ANY),
        scratch_shapes=[pltpu.SemaphoreType.DMA(())],
    )
    return pl.pallas_call(
        functools.partial(_scatter_kernel, tm=tm, n_tok=N),
        out_shape=jax.ShapeDtypeStruct((TOP_K * N, D), x.dtype),
        grid_spec=grid_spec,
        compiler_params=_cparams(("arbitrary",)),
        name="moe_scatter",
    )(dest, x)


def _moe_kernel(ti_ref, te_ref, lo_ref, hi_ref, first_ref, head_ref, slot_ref, nxt_ref,
                xs_ref, wg_hbm, wu_hbm, wd_hbm, y_ref,
                wg32_ref, wu32_ref, wd32_ref, wgb_ref, wub_ref, wdb_ref, sem, *, layer):
    i = pl.program_id(0)

    def weight_copies(e, slot):
        return [pltpu.make_async_copy(src.at[layer, e], dst.at[slot], sem.at[slot, n])
                for n, (src, dst) in enumerate(((wg_hbm, wg32_ref), (wu_hbm, wu32_ref), (wd_hbm, wd32_ref)))]

    @pl.when(i == 0)
    def _():
        for cp in weight_copies(te_ref[0], 0):
            cp.start()

    @pl.when(head_ref[i] == 1)
    def _():
        slot = slot_ref[i]
        for cp in weight_copies(te_ref[i], slot):
            cp.wait()
        wgb_ref[...] = wg32_ref[slot].astype(BF16)
        wub_ref[...] = wu32_ref[slot].astype(BF16)
        wdb_ref[...] = wd32_ref[slot].astype(BF16)

        @pl.when(nxt_ref[i] >= 0)
        def _():
            for cp in weight_copies(nxt_ref[i], 1 - slot):
                cp.start()

    lo, hi = lo_ref[i], hi_ref[i]

    @pl.when(hi > lo)
    def _():
        x = xs_ref[...].astype(BF16)
        g = _dot(x, wgb_ref[...])
        u = _dot(x, wub_ref[...])
        rows = lax.broadcasted_iota(jnp.int32, (x.shape[0], 1), 0)
        h = jnp.where((rows >= lo) & (rows < hi), (g * _sigmoid(g)) * u, 0.0)
        y = _dot(h.astype(BF16), wdb_ref[...])

        @pl.when(first_ref[i] == 1)
        def _():
            y_ref[...] = y

        @pl.when(first_ref[i] == 0)
        def _():
            y_ref[...] += y


def moe_experts(xs, items, w_gate, w_up, w_down, layer, *, tm=MOE_TM):
    P, D = xs.shape
    Fh = w_gate.shape[-1]
    n_items = items[0].shape[0]
    xmap = lambda i, ti, *_: (ti[i], 0)
    hbm = pl.BlockSpec(memory_space=pl.ANY)
    grid_spec = pltpu.PrefetchScalarGridSpec(
        num_scalar_prefetch=len(items),
        grid=(n_items,),
        in_specs=[pl.BlockSpec((tm, D), xmap), hbm, hbm, hbm],
        out_specs=pl.BlockSpec((tm, D), xmap),
        scratch_shapes=[pltpu.VMEM((2, D, Fh), F32), pltpu.VMEM((2, D, Fh), F32), pltpu.VMEM((2, Fh, D), F32),
                        pltpu.VMEM((D, Fh), BF16), pltpu.VMEM((D, Fh), BF16), pltpu.VMEM((Fh, D), BF16),
                        pltpu.SemaphoreType.DMA((2, 3))],
    )
    return pl.pallas_call(
        functools.partial(_moe_kernel, layer=layer),
        out_shape=jax.ShapeDtypeStruct((P, D), F32),
        grid_spec=grid_spec,
        compiler_params=_cparams(("arbitrary",)),
        name="moe_experts",
    )(*items, xs, w_gate, w_up, w_down)


def moe_items(tot, n_rows, tm=MOE_TM):
    E = N_EXPERTS
    n_max = n_rows // tm + E
    ar = jnp.arange(E, dtype=jnp.int32)
    ends = jnp.cumsum(tot)
    offs = ends - tot
    first_tile = offs // tm
    n_e = jnp.where(tot > 0, (ends - 1) // tm - first_tile + 1, 0)
    s_end = jnp.cumsum(n_e)
    s_beg = s_end - n_e
    n_items = s_end[-1]
    i = jnp.arange(n_max, dtype=jnp.int32)
    ic = jnp.minimum(i, n_items - 1)
    e_i = jnp.sum((ic[:, None] >= s_end[None, :]).astype(jnp.int32), axis=1)
    pick = (e_i[:, None] == ar[None, :]).astype(jnp.int32)
    at = lambda v: jnp.sum(pick * v[None, :], axis=1)
    tile = at(first_tile) + ic - at(s_beg)
    live = i < n_items
    lo = jnp.where(live, jnp.maximum(at(offs), tile * tm) - tile * tm, 0)
    hi = jnp.where(live, jnp.minimum(at(ends), tile * tm + tm) - tile * tm, 0)
    prev_tile = jnp.concatenate([jnp.full((1,), -1, jnp.int32), tile[:-1]])
    first = live & (tile != prev_tile)
    head = live & (ic == at(s_beg))
    used = (tot > 0).astype(jnp.int32)
    slot = at(jnp.cumsum(used) - used) & 1
    later = jnp.where((ar[None, :] > ar[:, None]) & (tot[None, :] > 0), ar[None, :], E)
    nxt_e = jnp.min(later, axis=1)
    nxt = at(jnp.where(nxt_e < E, nxt_e, -1))
    return tuple(v.astype(jnp.int32) for v in (tile, e_i, lo, hi, first, head, slot, nxt))


def _rope_tables(pos):
    inv = ROPE_THETA ** (-jnp.arange(0, ROPE_DIM, 2, dtype=F32) / ROPE_DIM)
    ang = pos.astype(F32)[:, None] * inv[None, :]
    cos, sin = jnp.cos(ang), jnp.sin(ang)
    n = pos.shape[0]
    rest = HEAD_DIM - ROPE_DIM
    cos_t = jnp.concatenate([cos, cos, jnp.ones((n, rest), F32)], axis=1)
    sin_lo = jnp.concatenate([-sin, jnp.zeros((n, HEAD_DIM - ROPE_HALF), F32)], axis=1)
    sin_hi = jnp.concatenate([jnp.zeros((n, ROPE_HALF), F32), sin, jnp.zeros((n, rest), F32)], axis=1)
    return cos_t, sin_lo, sin_hi


def _even_mixer(xb, B, T, w_in_all, j, pos, w1, w2, conv_w, conv_b, gn_g, gn_b):
    N, D = xb.shape
    G, HPG, dh = NSA_KV_HEADS, NSA_HPG, HEAD_DIM
    nsa_w = G * HPG * dh
    kv_cols = 3 * 2 * G * dh
    n_gate = 3 * G * HPG
    conv_c = (w_in_all.shape[2] - nsa_w - kv_cols - n_gate) // 2
    qkv = linear(xb, w_in_all, nsa_w + kv_cols, F32, layer=j, tm=512, tn=(nsa_w + kv_cols) // 2, name="even_qkv")
    w_gl = w_in_all[j, :, nsa_w + kv_cols:nsa_w + kv_cols + n_gate].reshape(D, G, 3 * HPG)
    w_gl = jnp.pad(w_gl, ((0, 0), (0, 0), (0, LANES - 3 * HPG))).reshape(D, G * LANES)
    gl = linear(xb, w_gl, G * LANES, F32, tn=G * LANES, name="even_gates")
    glu = linear(xb, w_in_all[j, :, nsa_w + kv_cols + n_gate:], 2 * conv_c, BF16, tn=1024, name="even_glu")

    qkv3 = qkv.reshape(B, T, nsa_w + kv_cols)
    n_cmp = (T - CMP_LEN) // CMP_STRIDE + 1
    nb = T // CMP_STRIDE
    cmp_end = jnp.arange(nb) * CMP_STRIDE + (CMP_LEN - 1)
    cmp_kv = compress(qkv3, pos, w1, w2, _rope_tables(cmp_end), B, T, G * HPG)
    o_nsa = nsa_attention(qkv3, cmp_kv, gl.reshape(B, T, G * LANES), _rope_tables(jnp.arange(T)), B, T)
    u = conv_module(glu.reshape(B, T, 2 * conv_c), conv_w, conv_b, gn_g, gn_b, B, T)
    return o_nsa.reshape(N, nsa_w), u.reshape(N, conv_c)


def _fox_mixer(xb, B, T, w_in_all, j, f_bias, q_gain, k_gain):
    N, D = xb.shape
    H = D // HEAD_DIM
    proj = linear(xb, w_in_all, 4 * D, BF16, layer=j, tm=512, tn=2048, name="fox_qkvg")
    w_f = jnp.pad(w_in_all[j, :, 4 * D:], ((0, 0), (0, LANES - H)))
    fl = linear(xb, w_f, LANES, F32, tn=LANES, name="fox_forget")
    c = fox_decay(fl, jnp.pad(f_bias, (0, LANES - H)), B, T)[:, :H].reshape(B, T, H)
    return fox_attention(proj.reshape(B, T, 4 * D), c, q_gain, k_gain, B, T, H).reshape(N, D)


def _moe(xf, router_w, router_bias, w_gate, w_up, w_down, layer):
    N, D = xf.shape
    idx, wts = router(xf, router_w, router_bias)
    dest8, cnt = moe_plan(idx)
    dest = dest8[:TOP_K].reshape(-1)
    xs = moe_scatter(xf, dest)
    y = moe_experts(xs, moe_items(cnt[:, 0], TOP_K * N), w_gate, w_up, w_down, layer)
    return y, dest, wts[:TOP_K].T


def kernel(x, even_w_in, even_w_out, nsa_cmp_pos, nsa_cmp_w1, nsa_cmp_w2, conv_w, conv_b, conv_gn_g, conv_gn_b, fox_w_in, fox_f_bias, fox_q_gain, fox_k_gain, fox_w_out, ln_mix_g, ln_mix_b, ln_ffn_g, ln_ffn_b, router_w, router_bias, exp_w_gate, exp_w_up, exp_w_down):
    B, T, D = x.shape
    depth = ln_mix_g.shape[0]
    alpha = (2.0 * depth) ** 0.25
    N = B * T
    xf = x.reshape(N, D)
    xb = xf
    for layer in range(depth):
        j = layer // 2
        if layer % 2 == 0:
            a1, a2 = _even_mixer(xb, B, T, even_w_in, j, nsa_cmp_pos[j], nsa_cmp_w1[j], nsa_cmp_w2[j],
                                 conv_w[j], conv_b[j], conv_gn_g[j], conv_gn_b[j])
            xf, xb = outproj_ln(a1, a2, 0, even_w_out, j, xf, ln_mix_g[layer], ln_mix_b[layer], alpha)
        else:
            a = _fox_mixer(xb, B, T, fox_w_in, j, fox_f_bias[j], fox_q_gain[j], fox_k_gain[j])
            xf, xb = outproj_ln(a, a, 1, fox_w_out, j, xf, ln_mix_g[layer], ln_mix_b[layer], alpha)
        y, dest, wts = _moe(xf, router_w, router_bias, exp_w_gate, exp_w_up, exp_w_down, layer)
        xf, xb = combine_ln(xf, y, dest, wts, ln_ffn_g[layer], ln_ffn_b[layer], alpha)
    return xf.reshape(B, T, D)
```

```python
import functools

import numpy as np
import jax
import jax.numpy as jnp
from jax import lax
from jax.experimental import pallas as pl
from jax.experimental.pallas import tpu as pltpu

F32 = jnp.float32
BF16 = jnp.bfloat16

HEAD_DIM = 128
ROPE_THETA = 500000.0
ROPE_DIM = HEAD_DIM // 4
ROPE_HALF = ROPE_DIM // 2
Q_BLOCK = 128

NSA_KV_HEADS = 2
NSA_HPG = 4
CMP_LEN = 32
CMP_STRIDE = 16
SLC_LEN = 64
N_SLC = 8
WIN = 512
CONV_KERNEL = 31
CONV_GROUPS = 8

N_EXPERTS = 16
N_EXPERT_GROUPS = 4
EXPERTS_PER_GROUP = 4
TOP_K = 2

LN_EPS = 1e-5
NEG = -1e30
BIG = 1e9
TINY = 1e-30

LANES = 128
V7X_VMEM_BYTES = 64 * 1024 * 1024
VMEM_LIMIT = V7X_VMEM_BYTES * 7 // 8

MOE_TM = 256


def _cparams(sem, vmem=VMEM_LIMIT):
    return pltpu.CompilerParams(dimension_semantics=sem, vmem_limit_bytes=vmem)


def _dot(a, b):
    return jnp.dot(a, b, preferred_element_type=F32)


def _dot_nt(a, b):
    return lax.dot_general(a, b, (((1,), (1,)), ((), ())), preferred_element_type=F32)


def _sigmoid(x):
    return 1.0 / (1.0 + jnp.exp(-x))


def _linear_kernel(x_ref, w_ref, o_ref, wb_ref):
    @pl.when(pl.program_id(1) == 0)
    def _():
        wb_ref[...] = w_ref[...].astype(BF16)

    o_ref[...] = _dot(x_ref[...].astype(BF16), wb_ref[...]).astype(o_ref.dtype)


def linear(x, w, n_cols, out_dtype, *, layer=None, tm=1024, tn=512, name="linear"):
    M, K = x.shape
    tm = min(tm, M)
    tn = min(tn, n_cols)
    assert M % tm == 0 and n_cols % tn == 0
    row_block = 0
    if layer is not None:
        w = w.reshape(-1, w.shape[-1])
        row_block = layer
    w_spec = pl.BlockSpec((K, tn), lambda j, i: (row_block, j))
    return pl.pallas_call(
        _linear_kernel,
        out_shape=jax.ShapeDtypeStruct((M, n_cols), out_dtype),
        grid=(n_cols // tn, M // tm),
        in_specs=[
            pl.BlockSpec((tm, K), lambda j, i: (i, 0)),
            w_spec,
        ],
        out_specs=pl.BlockSpec((tm, tn), lambda j, i: (i, j)),
        scratch_shapes=[pltpu.VMEM((K, tn), BF16)],
        compiler_params=_cparams(("arbitrary", "arbitrary")),
        name=name,
    )(x, w)


def _layer_norm_rows(y, g, b):
    mu = jnp.mean(y, axis=-1, keepdims=True)
    d = y - mu
    var = jnp.mean(d * d, axis=-1, keepdims=True)
    return d * lax.rsqrt(var + LN_EPS) * g + b


def _outproj_ln_kernel(a1_ref, a2_ref, w_ref, x_ref, g_ref, b_ref, xo_ref, xb_ref, wb_ref, *, alpha):
    @pl.when(pl.program_id(0) == 0)
    def _():
        wb_ref[...] = w_ref[...].astype(BF16)

    half = a1_ref.shape[1]
    h = _dot(a1_ref[...].astype(BF16), wb_ref[0:half, :])
    h = h + _dot(a2_ref[...].astype(BF16), wb_ref[half:2 * half, :])
    out = _layer_norm_rows(alpha * x_ref[...] + h, g_ref[...], b_ref[...])
    xo_ref[...] = out
    xb_ref[...] = out.astype(BF16)


def outproj_ln(a1, a2, a2_col_block, w, layer, x, g, b, alpha, *, tm=512):
    M, D = x.shape
    half = D // 2
    return pl.pallas_call(
        functools.partial(_outproj_ln_kernel, alpha=alpha),
        out_shape=(jax.ShapeDtypeStruct((M, D), F32), jax.ShapeDtypeStruct((M, D), BF16)),
        grid=(M // tm,),
        in_specs=[
            pl.BlockSpec((tm, half), lambda i: (i, 0)),
            pl.BlockSpec((tm, half), lambda i: (i, a2_col_block)),
            pl.BlockSpec((None, D, D), lambda i: (layer, 0, 0), pipeline_mode=pl.Buffered(1)),
            pl.BlockSpec((tm, D), lambda i: (i, 0)),
            pl.BlockSpec((1, D), lambda i: (0, 0)),
            pl.BlockSpec((1, D), lambda i: (0, 0)),
        ],
        out_specs=(pl.BlockSpec((tm, D), lambda i: (i, 0)), pl.BlockSpec((tm, D), lambda i: (i, 0))),
        scratch_shapes=[pltpu.VMEM((D, D), BF16)],
        compiler_params=_cparams(("arbitrary",)),
        name="outproj_ln",
    )(a1, a2, w, x, g.reshape(1, D), b.reshape(1, D))


ROW_DMA_UNROLL = 8


def _row_copies(dest_ref, n_tok, base, tm, make):
    def body(g, carry):
        r0 = pl.multiple_of(g * ROW_DMA_UNROLL, ROW_DMA_UNROLL)
        for u in range(ROW_DMA_UNROLL):
            for k in range(TOP_K):
                make(r0 + u, k, dest_ref[k * n_tok + base + r0 + u])
        return carry

    lax.fori_loop(0, tm // ROW_DMA_UNROLL, body, 0)


def _combine_ln_kernel(dest_ref, x_ref, w_ref, g_ref, b_ref, y_hbm, xo_ref, xb_ref, buf_ref, sem, *,
                       alpha, tm, n_tok):
    i = pl.program_id(0)
    n_steps = pl.num_programs(0)

    def copies(step, slot, start):
        def make(r, k, d):
            cp = pltpu.make_async_copy(y_hbm.at[pl.ds(d, 1)], buf_ref.at[slot, k, pl.ds(r, 1)], sem.at[slot])
            if start:
                cp.start(priority=k)
            else:
                cp.wait()
        _row_copies(dest_ref, n_tok, step * tm, tm, make)

    @pl.when(i == 0)
    def _():
        copies(0, 0, True)

    @pl.when(i + 1 < n_steps)
    def _():
        copies(i + 1, (i + 1) % 2, True)

    slot = i % 2
    copies(i, slot, False)
    f = w_ref[:, 0:1] * buf_ref[slot, 0] + w_ref[:, 1:2] * buf_ref[slot, 1]
    out = _layer_norm_rows(alpha * x_ref[...] + f, g_ref[...], b_ref[...])
    xo_ref[...] = out
    xb_ref[...] = out.astype(BF16)


def combine_ln(x, y, dest, wts, g, b, alpha, *, tm=512):
    N, D = x.shape
    row = lambda i, d: (i, 0)
    grid_spec = pltpu.PrefetchScalarGridSpec(
        num_scalar_prefetch=1,
        grid=(N // tm,),
        in_specs=[
            pl.BlockSpec((tm, D), row),
            pl.BlockSpec((tm, TOP_K), row),
            pl.BlockSpec((1, D), lambda i, d: (0, 0)),
            pl.BlockSpec((1, D), lambda i, d: (0, 0)),
            pl.BlockSpec(memory_space=pl.ANY),
        ],
        out_specs=(pl.BlockSpec((tm, D), row), pl.BlockSpec((tm, D), row)),
        scratch_shapes=[pltpu.VMEM((2, TOP_K, tm, D), F32), pltpu.SemaphoreType.DMA((2,))],
    )
    return pl.pallas_call(
        functools.partial(_combine_ln_kernel, alpha=alpha, tm=tm, n_tok=N),
        out_shape=(jax.ShapeDtypeStruct((N, D), F32), jax.ShapeDtypeStruct((N, D), BF16)),
        grid_spec=grid_spec,
        compiler_params=_cparams(("arbitrary",)),
        name="moe_combine_ln",
    )(dest, x, wts, g.reshape(1, D), b.reshape(1, D), y)


def _rms(x, gain):
    return x * lax.rsqrt(jnp.mean(x * x, axis=-1, keepdims=True) + LN_EPS) * gain


CUM_CHUNK = 256


def _decay_kernel(fl_ref, bias_ref, c_ref):
    T = fl_ref.shape[0]
    tri = (lax.broadcasted_iota(jnp.int32, (CUM_CHUNK, CUM_CHUNK), 0)
           >= lax.broadcasted_iota(jnp.int32, (CUM_CHUNK, CUM_CHUNK), 1)).astype(F32)
    carry = jnp.zeros((1, LANES), F32)
    for c in range(T // CUM_CHUNK):
        z = fl_ref[c * CUM_CHUNK:(c + 1) * CUM_CHUNK, :] + bias_ref[...]
        log_f = jnp.minimum(z, 0.0) - jnp.log1p(jnp.exp(-jnp.abs(z)))
        cs = jnp.dot(tri, log_f, preferred_element_type=F32, precision=lax.Precision.HIGHEST) + carry
        c_ref[c * CUM_CHUNK:(c + 1) * CUM_CHUNK, :] = cs
        carry = cs[CUM_CHUNK - 1:CUM_CHUNK, :]


def fox_decay(fl, bias, B, T):
    assert T % CUM_CHUNK == 0
    return pl.pallas_call(
        _decay_kernel,
        out_shape=jax.ShapeDtypeStruct(fl.shape, F32),
        grid=(B,),
        in_specs=[pl.BlockSpec((T, LANES), lambda b: (b, 0)), pl.BlockSpec((1, LANES), lambda b: (0, 0))],
        out_specs=pl.BlockSpec((T, LANES), lambda b: (b, 0)),
        compiler_params=_cparams(("arbitrary",)),
        name="fox_decay",
    )(fl, bias.reshape(1, LANES))


LOG2E = 1.4426950408889634
FAST_SOFTMAX_BOUND = 38.0
BIAS_PIECES = 3


def _split3(a):
    hi = a.astype(BF16).astype(F32)
    r = a - hi
    mid = r.astype(BF16).astype(F32)
    return hi, mid, (r - mid).astype(BF16).astype(F32)


def _aux_lanes(pieces, ones_first, n):
    lane = lax.broadcasted_iota(jnp.int32, (n, LANES), 1)
    p0 = BIAS_PIECES if ones_first else 0
    o0 = 0 if ones_first else BIAS_PIECES
    out = jnp.where((lane >= o0) & (lane < o0 + BIAS_PIECES), 1.0, 0.0)
    for i, piece in enumerate(pieces):
        out = out + jnp.where(lane == p0 + i, piece, 0.0)
    return out


def _fox_kernel(fast_ref, q_ref, k_ref, v_ref, og_ref, ckrow_ref, ckcol_ref, cq_ref, bound_ref, qg_ref, kg_ref,
                o_ref, kx_ref, vx_ref, *, tq, tk, hb):
    qi = pl.program_id(2)
    dh = HEAD_DIM
    T = k_ref.shape[1]

    def piece_by_lane(a, lane):
        hi, mid, lo = _split3(a)
        in_group = lambda g: ((lane >= g * hb) & (lane < (g + 1) * hb)) | (
            (lane >= (BIAS_PIECES + g) * hb) & (lane < (BIAS_PIECES + g + 1) * hb))
        return jnp.where(in_group(0), hi, jnp.where(in_group(1), mid, lo))

    @pl.when(qi == 0)
    def _():
        lane_k = lax.broadcasted_iota(jnp.int32, (T, LANES), 1)
        k_aux = jnp.where(lane_k < BIAS_PIECES * hb, piece_by_lane(-LOG2E * ckcol_ref[0, 0], lane_k),
                          jnp.where(lane_k < 2 * BIAS_PIECES * hb, 1.0, 0.0)).astype(BF16)
        for h in range(hb):
            kx_ref[h, :, 0:dh] = _rms(k_ref[0, :, h * dh:(h + 1) * dh].astype(F32), kg_ref[...]).astype(BF16)
            kx_ref[h, :, dh:2 * dh] = k_aux
            vx_ref[h, :, 0:dh] = v_ref[0, :, h * dh:(h + 1) * dh].astype(BF16)
            vx_ref[h, :, dh:2 * dh] = jnp.ones((T, dh), BF16)

    row = qi * tq + lax.broadcasted_iota(jnp.int32, (tq, tk), 0)
    lane = lax.broadcasted_iota(jnp.int32, (tq, tk), 1)
    n_full = (qi * tq) // tk

    def finish(accs):
        for h in range(hb):
            o = accs[h][:, 0:dh] / jnp.maximum(accs[h][:, dh:2 * dh], TINY)
            gate = _sigmoid(og_ref[0, :, h * dh:(h + 1) * dh].astype(F32))
            o_ref[0, :, h * dh:(h + 1) * dh] = (o * gate).astype(o_ref.dtype)

    def q_normed(h):
        return _rms(q_ref[0, :, h * dh:(h + 1) * dh].astype(F32), qg_ref[...]) * (dh ** -0.5)

    @pl.when(fast_ref[0] == 1)
    def _():
        lane_q = lax.broadcasted_iota(jnp.int32, (tq, LANES), 1)
        r = piece_by_lane(LOG2E * (cq_ref[0, 0] - bound_ref[0:1, 0:1]), lane_q)
        qx = []
        for h in range(hb):
            mine = (lane_q & (hb - 1)) == h
            q_aux = jnp.where(mine & (lane_q < BIAS_PIECES * hb), 1.0,
                              jnp.where(mine & (lane_q < 2 * BIAS_PIECES * hb), r, 0.0))
            qx.append(jnp.concatenate([(q_normed(h) * LOG2E).astype(BF16), q_aux.astype(BF16)], axis=1))

        def step(j, accs, masked):
            start = pl.multiple_of(j * tk, tk)
            out = []
            for h in range(hb):
                s = _dot_nt(qx[h], kx_ref[h, pl.ds(start, tk), :])
                if masked:
                    s = jnp.where(j * tk + lane <= row, s, NEG)
                out.append(accs[h] + _dot(jnp.exp2(s).astype(BF16), vx_ref[h, pl.ds(start, tk), :]))
            return tuple(out)

        def diagonal(accs):
            hq = tq // 2
            start = pl.multiple_of(n_full * tk, tk)
            tri = (lax.broadcasted_iota(jnp.int32, (hq, hq), 1) <= lax.broadcasted_iota(jnp.int32, (hq, hq), 0))
            out = []
            for h in range(hb):
                k_lo, k_hi = kx_ref[h, pl.ds(start, hq), :], kx_ref[h, pl.ds(start + hq, hq), :]
                v_lo, v_all = vx_ref[h, pl.ds(start, hq), :], vx_ref[h, pl.ds(start, tk), :]
                p_top = jnp.exp2(jnp.where(tri, _dot_nt(qx[h][0:hq], k_lo), NEG)).astype(BF16)
                p_bot = jnp.concatenate(
                    [jnp.exp2(_dot_nt(qx[h][hq:tq], k_lo)).astype(BF16),
                     jnp.exp2(jnp.where(tri, _dot_nt(qx[h][hq:tq], k_hi), NEG)).astype(BF16)], axis=1)
                out.append(accs[h] + jnp.concatenate([_dot(p_top, v_lo), _dot(p_bot, v_all)], axis=0))
            return out

        accs = tuple(jnp.zeros((tq, 2 * dh), F32) for _ in range(hb))
        accs = lax.fori_loop(0, n_full, lambda j, a: step(j, a, False), accs)
        finish(diagonal(accs) if tq == tk else step(n_full, accs, True))

    @pl.when(fast_ref[0] == 0)
    def _():
        qs = [q_normed(h).astype(BF16) for h in range(hb)]

        def step(j, carry, masked):
            start = pl.multiple_of(j * tk, tk)
            out = []
            for h in range(hb):
                m, acc = carry[h]
                s = _dot_nt(qs[h], kx_ref[h, pl.ds(start, tk), 0:dh]) - ckrow_ref[0, h, pl.ds(j, 1), :]
                if masked:
                    s = jnp.where(j * tk + lane <= row, s, NEG)
                m_new = jnp.maximum(m, jnp.max(s, axis=-1, keepdims=True))
                p = jnp.exp(s - m_new).astype(BF16)
                acc = jnp.exp(m - m_new) * acc + _dot(p, vx_ref[h, pl.ds(start, tk), :])
                out.append((m_new, acc))
            return tuple(out)

        init = tuple((jnp.full((tq, 1), NEG, F32), jnp.zeros((tq, 2 * dh), F32)) for _ in range(hb))
        carry = lax.fori_loop(0, n_full, lambda j, c: step(j, c, False), init)
        finish([c[1] for c in step(n_full, carry, True)])


def fox_attention(proj, c, q_gain, k_gain, B, T, H, *, tq=512, tk=512, hb=4):
    nq = T // tq
    nk = T // tk
    dh = HEAD_DIM
    hg = H // hb
    c_row = c.transpose(0, 2, 1).reshape(B, H, nk, tk)
    assert hb & (hb - 1) == 0 and 2 * BIAS_PIECES * hb <= LANES
    c_col = c.reshape(B, T, hg, hb).transpose(0, 2, 1, 3)
    c_col = jnp.concatenate([jnp.tile(c_col, (1, 1, 1, 2 * BIAS_PIECES)),
                             jnp.zeros((B, hg, T, LANES - 2 * BIAS_PIECES * hb), F32)], axis=-1)
    bound = jnp.max(jnp.abs(q_gain)) * jnp.max(jnp.abs(k_gain)) * (dh ** 0.5) * 1.01
    fast = (bound <= FAST_SOFTMAX_BOUND).astype(jnp.int32).reshape(1)
    full = lambda off: pl.BlockSpec((1, T, hb * dh), lambda b, h, i, f: (b, 0, off + h))
    tile = lambda off: pl.BlockSpec((1, tq, hb * dh), lambda b, h, i, f: (b, i, off + h))
    vec = pl.BlockSpec((1, dh), lambda b, h, i, f: (0, 0))
    grid_spec = pltpu.PrefetchScalarGridSpec(
        num_scalar_prefetch=1,
        grid=(B, hg, nq),
        in_specs=[
            tile(0), full(hg), full(2 * hg), tile(3 * hg),
            pl.BlockSpec((1, hb, nk, tk), lambda b, h, i, f: (b, h, 0, 0)),
            pl.BlockSpec((1, 1, T, LANES), lambda b, h, i, f: (b, h, 0, 0)),
            pl.BlockSpec((1, 1, tq, LANES), lambda b, h, i, f: (b, h, i, 0)),
            vec, vec, vec,
        ],
        out_specs=pl.BlockSpec((1, tq, hb * dh), lambda b, h, i, f: (b, i, h)),
        scratch_shapes=[pltpu.VMEM((hb, T, 2 * dh), BF16), pltpu.VMEM((hb, T, 2 * dh), BF16)],
    )
    return pl.pallas_call(
        functools.partial(_fox_kernel, tq=tq, tk=tk, hb=hb),
        out_shape=jax.ShapeDtypeStruct((B, T, H * dh), BF16),
        grid_spec=grid_spec,
        compiler_params=_cparams(("arbitrary", "arbitrary", "arbitrary")),
        name="fox_attention",
    )(fast, proj, proj, proj, proj, c_row, c_col, c_col, jnp.full((1, dh), bound, F32),
      q_gain.reshape(1, dh), k_gain.reshape(1, dh))


def _rope(x, cos, sin_lo, sin_hi):
    return (x * cos + pltpu.roll(x, LANES - ROPE_HALF, 1) * sin_lo
            + pltpu.roll(x, ROPE_HALF, 1) * sin_hi)


def _gelu_tanh(x):
    return 0.5 * x * (1.0 + jnp.tanh(0.7978845608028654 * (x + 0.044715 * (x * x * x))))


def _compress_kernel(raw_ref, pos_ref, w1_ref, w2_ref, cos_ref, slo_ref, shi_ref, o_ref, *, n_cmp):
    nb = o_ref.shape[-2]
    half = CMP_LEN // 2
    acc_a = jnp.zeros((nb, HEAD_DIM), F32)
    acc_b = jnp.zeros((nb, HEAD_DIM), F32)
    for l in range(half):
        rl = raw_ref[0, pl.ds(l, nb, stride=CMP_STRIDE), :]
        wa = w1_ref[0, l * HEAD_DIM:(l + 1) * HEAD_DIM, :].astype(BF16)
        wb = w1_ref[0, (half + l) * HEAD_DIM:(half + l + 1) * HEAD_DIM, :].astype(BF16)
        acc_a = acc_a + _dot((rl + pos_ref[0, l:l + 1, :]).astype(BF16), wa)
        acc_b = acc_b + _dot((rl + pos_ref[0, half + l:half + l + 1, :]).astype(BF16), wb)
    pre = acc_a + pltpu.roll(acc_b, nb - 1, 0)
    out = _dot(_gelu_tanh(pre).astype(BF16), w2_ref[0].astype(BF16))
    roped = _rope(out, cos_ref[...], slo_ref[...], shi_ref[...])
    out = jnp.where(pl.program_id(1) == 0, roped, out)
    rows = lax.broadcasted_iota(jnp.int32, out.shape, 0)
    o_ref[0, 0, 0] = jnp.where(rows < n_cmp, out, 0.0)


def compress(qkv, pos, w1, w2, tabs, B, T, kv_block0):
    G = NSA_KV_HEADS
    nb = T // CMP_STRIDE
    n_cmp = (T - CMP_LEN) // CMP_STRIDE + 1
    cos, slo, shi = tabs
    tab = pl.BlockSpec((nb, HEAD_DIM), lambda b, kv, g: (0, 0))
    return pl.pallas_call(
        functools.partial(_compress_kernel, n_cmp=n_cmp),
        out_shape=jax.ShapeDtypeStruct((B, 2, G, nb, HEAD_DIM), F32),
        grid=(B, 2, G),
        in_specs=[
            pl.BlockSpec((1, T, HEAD_DIM), lambda b, kv, g: (b, 0, kv_block0 + kv * G + g)),
            pl.BlockSpec((1, CMP_LEN, HEAD_DIM), lambda b, kv, g: (kv, 0, 0)),
            pl.BlockSpec((1, CMP_LEN * HEAD_DIM, HEAD_DIM), lambda b, kv, g: (kv, 0, 0)),
            pl.BlockSpec((1, HEAD_DIM, HEAD_DIM), lambda b, kv, g: (kv, 0, 0)),
            tab, tab, tab,
        ],
        out_specs=pl.BlockSpec((1, 1, 1, nb, HEAD_DIM), lambda b, kv, g: (b, kv, g, 0, 0)),
        compiler_params=_cparams(("arbitrary", "arbitrary", "arbitrary")),
        name="nsa_compress",
    )(qkv, pos, w1, w2, cos, slo, shi)


SLC_CHUNK = 512
WIN_SPAN = WIN + Q_BLOCK
NSA_QB = 4


def _select_blocks(imp, qi, n_slc):
    QB = Q_BLOCK
    n_rows = -(-n_slc // 8) * 8
    imp_t = jnp.transpose(imp)[0:n_rows]
    blk = lax.broadcasted_iota(jnp.int32, (n_rows, QB), 0)
    t = qi * QB + lax.broadcasted_iota(jnp.int32, (n_rows, QB), 1)
    cur = jnp.right_shift(t, 6)
    forced = (blk == 0) | (blk == cur) | (blk == cur - 1)
    score = jnp.where(forced, BIG, jnp.where(blk * SLC_LEN <= t, imp_t, -BIG))
    score = jnp.where(blk < n_slc, score, -2.0 * BIG)
    cnt = jnp.zeros((n_rows, QB), F32)
    for jp in range(n_slc):
        r = score[jp:jp + 1, :]
        cnt = cnt + ((r > score) | ((r == score) & (jp < blk))).astype(F32)
    sel_t = ((cnt < float(min(N_SLC, n_slc))) & (blk < n_slc)).astype(F32)
    if n_rows < LANES:
        sel_t = jnp.concatenate([sel_t, jnp.zeros((LANES - n_rows, QB), F32)], axis=0)
    return jnp.transpose(sel_t).astype(BF16)


def _nsa_kernel(q_ref, ks_ref, vs_ref, kw_ref, vw_ref, kc_ref, vc_ref, gl_ref,
                cq_ref, sloq_ref, shiq_ref, ck_ref, slok_ref, shik_ref, c2s_ref, ex_ref,
                o_ref, ksx_ref, vsx_ref, kwb_ref, vwx_ref, bias_ref, kmax_ref, acc_ref, *, n_slc, n_cmp):
    step = pl.program_id(2)
    QB, HPG, dh = Q_BLOCK, NSA_HPG, HEAD_DIM
    T = ks_ref.shape[1]

    @pl.when(step == 0)
    def _():
        ks = _rope(ks_ref[0], ck_ref[...], slok_ref[...], shik_ref[...])
        ksx_ref[:, 0:dh] = ks.astype(BF16)
        ksx_ref[:, dh:2 * dh] = _aux_lanes((), False, T).astype(BF16)
        kmax_ref[...] = jnp.full(kmax_ref.shape, jnp.max(jnp.sum(ks * ks, axis=-1, keepdims=True)), F32)
        kwb_ref[...] = _rope(kw_ref[0], ck_ref[...], slok_ref[...], shik_ref[...]).astype(BF16)
        ones = jnp.ones((T, dh), BF16)
        vsx_ref[:, 0:dh] = vs_ref[0].astype(BF16)
        vsx_ref[:, dh:2 * dh] = ones
        vwx_ref[:, 0:dh] = vw_ref[0].astype(BF16)
        vwx_ref[:, dh:2 * dh] = ones

    qis = [step * NSA_QB + b for b in range(NSA_QB)]
    rows = lambda b, h: slice((b * HPG + h) * QB, (b * HPG + h + 1) * QB)
    qrows = lambda b: slice(b * QB, (b + 1) * QB)
    pieces, norms2 = [], []
    for b in range(NSA_QB):
        for h in range(HPG):
            x = q_ref[0, qrows(b), h * dh:(h + 1) * dh]
            x = _rope(x, cq_ref[qrows(b), :], sloq_ref[qrows(b), :], shiq_ref[qrows(b), :]) * (dh ** -0.5)
            pieces.append(x)
            norms2.append(_dot((x * x).astype(BF16), jnp.ones((dh, LANES), BF16)))
    q = jnp.concatenate(pieces, axis=0).astype(BF16)

    lane = lax.broadcasted_iota(jnp.int32, (QB, LANES), 1)
    sub = lax.broadcasted_iota(jnp.int32, (QB, LANES), 0)

    kc = kc_ref[0, 0, 0].astype(BF16)
    vc = vc_ref[0, 0, 0].astype(BF16)
    sc = _dot_nt(q, kc)
    o_cmp, sels = {}, []
    for b in range(NSA_QB):
        t = qis[b] * QB + sub
        mask_c = (lane * CMP_STRIDE + (CMP_LEN - 1) <= t) & (lane < n_cmp)
        mask_cf = mask_c.astype(F32)
        imp_c = jnp.zeros((QB, LANES), F32)
        for h in range(HPG):
            s = jnp.where(mask_c, sc[rows(b, h)], NEG)
            e = jnp.exp(s - jnp.max(s, axis=-1, keepdims=True)) * mask_cf
            p = e / jnp.maximum(jnp.sum(e, axis=-1, keepdims=True), TINY)
            imp_c = imp_c + p
            o_cmp[b, h] = _dot(p.astype(BF16), vc)
        imp = jnp.dot(imp_c, c2s_ref[...], preferred_element_type=F32, precision=lax.Precision.HIGHEST)
        sels.append(_select_blocks(imp, qis[b], n_slc))

    sel_all = jnp.concatenate(sels, axis=0)
    for c in range(T // SLC_CHUNK):
        selx = _dot(sel_all, ex_ref[:, c * SLC_CHUNK:(c + 1) * SLC_CHUNK])
        kpos = c * SLC_CHUNK + lax.broadcasted_iota(jnp.int32, (QB, SLC_CHUNK), 1)
        for b in range(NSA_QB):
            tq_ = qis[b] * QB + lax.broadcasted_iota(jnp.int32, (QB, SLC_CHUNK), 0)
            bias_ref[b, c] = jnp.where((selx[qrows(b)] > 0.5) & (kpos <= tq_), 0.0, NEG)

    o_win = {}
    for b in range(NSA_QB):
        wstart = pl.multiple_of(jnp.clip(qis[b] * QB - WIN, 0, T - WIN_SPAN), Q_BLOCK)
        qb = q[b * HPG * QB:(b + 1) * HPG * QB]
        sw = _dot_nt(qb, kwb_ref[pl.ds(wstart, WIN_SPAN), :])
        kpos = wstart + lax.broadcasted_iota(jnp.int32, (QB, WIN_SPAN), 1)
        tw = qis[b] * QB + lax.broadcasted_iota(jnp.int32, (QB, WIN_SPAN), 0)
        bias_w = jnp.where((kpos <= tw) & (kpos > tw - WIN), 0.0, NEG)
        pw = []
        for h in range(HPG):
            sh = sw[h * QB:(h + 1) * QB] + bias_w
            pw.append(jnp.exp(sh - jnp.max(sh, axis=-1, keepdims=True)).astype(BF16))
        ow = _dot(jnp.concatenate(pw, axis=0), vwx_ref[pl.ds(wstart, WIN_SPAN), :])
        for h in range(HPG):
            acc_w = ow[h * QB:(h + 1) * QB]
            o_win[b, h] = acc_w[:, 0:dh] / jnp.maximum(acc_w[:, dh:2 * dh], TINY)

    chains = [(b, h) for b in range(NSA_QB) for h in range(HPG)]

    n_chunks = (qis[-1] * QB + QB + SLC_CHUNK - 1) // SLC_CHUNK
    kmax2 = kmax_ref[0:1, 0:1]
    bounds = [jnp.sqrt(n2 * kmax2) * 1.02 for n2 in norms2]
    worst = functools.reduce(jnp.maximum, [jnp.max(bd) for bd in bounds])

    @pl.when(worst <= FAST_SOFTMAX_BOUND)
    def _():
        qx = jnp.concatenate(
            [jnp.concatenate([(x * LOG2E).astype(BF16),
                              _aux_lanes(_split3(-LOG2E * bd), True, QB).astype(BF16)], axis=1)
             for x, bd in zip(pieces, bounds)], axis=0)

        def body(c, accs):
            start = pl.multiple_of(c * SLC_CHUNK, SLC_CHUNK)
            s = _dot_nt(qx, ksx_ref[pl.ds(start, SLC_CHUNK), :])
            ps = [jnp.exp2(s[rows(b, h)] + bias_ref[b, c]).astype(BF16) for (b, h) in chains]
            pv = _dot(jnp.concatenate(ps, axis=0), vsx_ref[pl.ds(start, SLC_CHUNK), :])
            return tuple(accs[n] + pv[rows(b, h)] for n, (b, h) in enumerate(chains))

        accs = lax.fori_loop(0, n_chunks, body, tuple(jnp.zeros((QB, 2 * dh), F32) for _ in chains))
        for n, (b, h) in enumerate(chains):
            acc_ref[rows(b, h), :] = accs[n]

    @pl.when(worst > FAST_SOFTMAX_BOUND)
    def _():
        def body(c, carry):
            start = pl.multiple_of(c * SLC_CHUNK, SLC_CHUNK)
            s = _dot_nt(q, ksx_ref[pl.ds(start, SLC_CHUNK), 0:dh])
            ms, alphas, ps = [], [], []
            for n, (b, h) in enumerate(chains):
                sh = s[rows(b, h)] + bias_ref[b, c]
                m_new = jnp.maximum(carry[n][0], jnp.max(sh, axis=-1, keepdims=True))
                alphas.append(jnp.exp(carry[n][0] - m_new))
                ps.append(jnp.exp(sh - m_new).astype(BF16))
                ms.append(m_new)
            pv = _dot(jnp.concatenate(ps, axis=0), vsx_ref[pl.ds(start, SLC_CHUNK), :])
            return tuple((ms[n], alphas[n] * carry[n][1] + pv[rows(b, h)]) for n, (b, h) in enumerate(chains))

        init = tuple((jnp.full((QB, 1), NEG, F32), jnp.zeros((QB, 2 * dh), F32)) for _ in chains)
        slc = lax.fori_loop(0, n_chunks, body, init)
        for n, (b, h) in enumerate(chains):
            acc_ref[rows(b, h), :] = slc[n][1]

    for b in range(NSA_QB):
        gates = _sigmoid(gl_ref[0, qrows(b), :])
        for h in range(HPG):
            acc_s = acc_ref[rows(b, h), :]
            o_s = acc_s[:, 0:dh] / jnp.maximum(acc_s[:, dh:2 * dh], TINY)
            out = (gates[:, 3 * h:3 * h + 1] * o_cmp[b, h] + gates[:, 3 * h + 1:3 * h + 2] * o_s
                   + gates[:, 3 * h + 2:3 * h + 3] * o_win[b, h])
            o_ref[0, qrows(b), h * dh:(h + 1) * dh] = out.astype(o_ref.dtype)


def nsa_attention(qkv, cmp_kv, gl, tabs_q, B, T):
    G, HPG, dh = NSA_KV_HEADS, NSA_HPG, HEAD_DIM
    tq = NSA_QB * Q_BLOCK
    nb = T // CMP_STRIDE
    n_cmp = (T - CMP_LEN) // CMP_STRIDE + 1
    n_slc = T // SLC_LEN
    assert nb == LANES and n_slc <= LANES and T % SLC_CHUNK == 0 and T >= WIN_SPAN and T % tq == 0
    kvb = (HPG * G)

    c_start = np.arange(nb) * CMP_STRIDE
    s_start = np.arange(LANES) * SLC_LEN
    c2s = ((c_start[:, None] < s_start[None, :] + SLC_LEN) & (c_start[:, None] + CMP_LEN > s_start[None, :])
           & (np.arange(nb)[:, None] < n_cmp) & (np.arange(LANES)[None, :] < n_slc)).astype(np.float32)
    expand = (np.arange(T)[None, :] // SLC_LEN == np.arange(LANES)[:, None]).astype(np.float32)

    cos, slo, shi = tabs_q
    kv_full = lambda blk: pl.BlockSpec((1, T, dh), lambda b, g, i: (b, 0, kvb + blk + g))
    qtab = pl.BlockSpec((tq, dh), lambda b, g, i: (i, 0))
    ktab = pl.BlockSpec((T, dh), lambda b, g, i: (0, 0))
    return pl.pallas_call(
        functools.partial(_nsa_kernel, n_slc=n_slc, n_cmp=n_cmp),
        out_shape=jax.ShapeDtypeStruct((B, T, G * HPG * dh), BF16),
        grid=(B, G, T // tq),
        in_specs=[
            pl.BlockSpec((1, tq, HPG * dh), lambda b, g, i: (b, i, g)),
            kv_full(2 * G), kv_full(3 * G), kv_full(4 * G), kv_full(5 * G),
            pl.BlockSpec((1, 1, 1, nb, dh), lambda b, g, i: (b, 0, g, 0, 0)),
            pl.BlockSpec((1, 1, 1, nb, dh), lambda b, g, i: (b, 1, g, 0, 0)),
            pl.BlockSpec((1, tq, LANES), lambda b, g, i: (b, i, g)),
            qtab, qtab, qtab, ktab, ktab, ktab,
            pl.BlockSpec((nb, LANES), lambda b, g, i: (0, 0)),
            pl.BlockSpec((LANES, T), lambda b, g, i: (0, 0)),
        ],
        out_specs=pl.BlockSpec((1, tq, HPG * dh), lambda b, g, i: (b, i, g)),
        scratch_shapes=[pltpu.VMEM((T, 2 * dh), BF16), pltpu.VMEM((T, 2 * dh), BF16),
                        pltpu.VMEM((T, dh), BF16), pltpu.VMEM((T, 2 * dh), BF16),
                        pltpu.VMEM((NSA_QB, T // SLC_CHUNK, Q_BLOCK, SLC_CHUNK), F32),
                        pltpu.VMEM((8, LANES), F32),
                        pltpu.VMEM((NSA_QB * HPG * Q_BLOCK, 2 * dh), F32)],
        compiler_params=_cparams(("arbitrary", "arbitrary", "arbitrary")),
        name="nsa_attention",
    )(qkv, qkv, qkv, qkv, qkv, cmp_kv, cmp_kv, gl, cos, slo, shi, cos, slo, shi,
      jnp.asarray(c2s), jnp.asarray(expand, dtype=BF16))


CONV_PAD = 32
CONV_CHUNK = 256


def _conv_kernel(a_ref, b_ref, w_ref, cb_ref, g_ref, gb_ref, o_ref, u_ref):
    T = a_ref.shape[1]
    u_ref[0:CONV_PAD, :] = jnp.zeros((CONV_PAD, LANES), F32)
    u_ref[CONV_PAD:CONV_PAD + T, :] = a_ref[0].astype(F32) * _sigmoid(b_ref[0].astype(F32))
    base = CONV_PAD - (CONV_KERNEL - 1)
    for c in range(T // CONV_CHUNK):
        t0 = c * CONV_CHUNK
        acc = jnp.zeros((CONV_CHUNK, LANES), F32)
        for k in range(CONV_KERNEL):
            acc = acc + u_ref[t0 + base + k:t0 + base + k + CONV_CHUNK, :] * w_ref[k:k + 1, :]
        acc = acc + cb_ref[...]
        mu = jnp.mean(acc, axis=-1, keepdims=True)
        d = acc - mu
        var = jnp.mean(d * d, axis=-1, keepdims=True)
        y = d * lax.rsqrt(var + LN_EPS) * g_ref[...] + gb_ref[...]
        o_ref[0, t0:t0 + CONV_CHUNK, :] = (y * _sigmoid(y)).astype(o_ref.dtype)


def conv_module(glu, conv_w, conv_b, gn_g, gn_b, B, T):
    C = glu.shape[-1] // 2
    ng = C // LANES
    assert C // CONV_GROUPS == LANES
    vec = pl.BlockSpec((1, LANES), lambda b, g: (0, g))
    return pl.pallas_call(
        _conv_kernel,
        out_shape=jax.ShapeDtypeStruct((B, T, C), BF16),
        grid=(B, ng),
        in_specs=[
            pl.BlockSpec((1, T, LANES), lambda b, g: (b, 0, g)),
            pl.BlockSpec((1, T, LANES), lambda b, g: (b, 0, ng + g)),
            pl.BlockSpec((CONV_KERNEL, LANES), lambda b, g: (0, g)),
            vec, vec, vec,
        ],
        out_specs=pl.BlockSpec((1, T, LANES), lambda b, g: (b, 0, g)),
        scratch_shapes=[pltpu.VMEM((CONV_PAD + T, LANES), F32)],
        compiler_params=_cparams(("arbitrary", "arbitrary")),
        name="conformer_conv",
    )(glu, glu, conv_w, conv_b.reshape(1, C), gn_g.reshape(1, C), gn_b.reshape(1, C))


def _split2(a):
    hi = a.astype(BF16)
    return hi, (a - hi.astype(F32)).astype(BF16)


def _router_kernel(x_ref, w_ref, bias_ref, idx_ref, wts_ref):
    xh, xl = _split2(x_ref[...])
    wh, wl = _split2(w_ref[...])
    logits = _dot(xh, wh) + (_dot(xh, wl) + _dot(xl, wh))
    aff = _sigmoid(jnp.concatenate(
        [jnp.transpose(logits[c * LANES:(c + 1) * LANES])[0:N_EXPERTS] for c in range(logits.shape[0] // LANES)],
        axis=1))
    sel = aff + bias_ref[...]
    a = [aff[e:e + 1, :] for e in range(N_EXPERTS)]
    s = [sel[e:e + 1, :] for e in range(N_EXPERTS)]
    P = EXPERTS_PER_GROUP
    grp = []
    for g in range(N_EXPERT_GROUPS):
        v = s[g * P:(g + 1) * P]
        best = None
        for i in range(P):
            for j in range(i + 1, P):
                pair = v[i] + v[j]
                best = pair if best is None else jnp.maximum(best, pair)
        grp.append(best)
    gbest = jnp.zeros_like(grp[0], dtype=jnp.int32)
    gval = grp[0]
    for g in range(1, N_EXPERT_GROUPS):
        better = grp[g] > gval
        gbest = jnp.where(better, g, gbest)
        gval = jnp.where(better, grp[g], gval)
    cs, ca = [], []
    for p in range(P):
        sv, av = s[p], a[p]
        for g in range(1, N_EXPERT_GROUPS):
            sv = jnp.where(gbest == g, s[g * P + p], sv)
            av = jnp.where(gbest == g, a[g * P + p], av)
        cs.append(sv)
        ca.append(av)
    i1 = jnp.zeros_like(gbest)
    v1, a1 = cs[0], ca[0]
    for p in range(1, P):
        better = cs[p] > v1
        i1 = jnp.where(better, p, i1)
        v1 = jnp.where(better, cs[p], v1)
        a1 = jnp.where(better, ca[p], a1)
    i2 = jnp.full_like(gbest, -1)
    v2 = jnp.full_like(v1, -jnp.inf)
    a2 = jnp.zeros_like(a1)
    for p in range(P):
        better = (i1 != p) & (cs[p] > v2)
        i2 = jnp.where(better, p, i2)
        v2 = jnp.where(better, cs[p], v2)
        a2 = jnp.where(better, ca[p], a2)
    den = a1 + a2
    idx_ref[...] = jnp.zeros(idx_ref.shape, jnp.int32)
    wts_ref[...] = jnp.zeros(wts_ref.shape, F32)
    idx_ref[0:1, :] = gbest * P + i1
    idx_ref[1:2, :] = gbest * P + i2
    wts_ref[0:1, :] = a1 / den
    wts_ref[1:2, :] = a2 / den


def router(x, router_w, router_bias, *, tm=1024):
    N, D = x.shape
    E = N_EXPERTS
    return pl.pallas_call(
        _router_kernel,
        out_shape=(jax.ShapeDtypeStruct((8, N), jnp.int32), jax.ShapeDtypeStruct((8, N), F32)),
        grid=(N // tm,),
        in_specs=[
            pl.BlockSpec((tm, D), lambda i: (i, 0)),
            pl.BlockSpec((D, LANES), lambda i: (0, 0)),
            pl.BlockSpec((E, 1), lambda i: (0, 0)),
        ],
        out_specs=(pl.BlockSpec((8, tm), lambda i: (0, i)), pl.BlockSpec((8, tm), lambda i: (0, i))),
        compiler_params=_cparams(("arbitrary",)),
        name="moe_router",
    )(x, jnp.pad(router_w, ((0, 0), (0, LANES - E))), router_bias.reshape(E, 1))


PLAN_CHUNK = 512


def _plan_kernel(idx_ref, dest_ref, cnt_ref, *, n_tok):
    E = N_EXPERTS
    sub = lax.broadcasted_iota(jnp.int32, (TOP_K * E, n_tok), 0)
    tgt = jnp.where(sub < E, idx_ref[0:1, :], idx_ref[1:2, :])
    onehot = ((sub & (E - 1)) == tgt).astype(F32)
    onehot_b = onehot.astype(BF16)
    tri = (lax.broadcasted_iota(jnp.int32, (PLAN_CHUNK, PLAN_CHUNK), 0)
           <= lax.broadcasted_iota(jnp.int32, (PLAN_CHUNK, PLAN_CHUNK), 1)).astype(F32).astype(BF16)
    carry = jnp.zeros((TOP_K * E, 1), F32)
    parts = []
    for c in range(n_tok // PLAN_CHUNK):
        pre = _dot(onehot_b[:, c * PLAN_CHUNK:(c + 1) * PLAN_CHUNK], tri) + carry
        parts.append(pre)
        carry = pre[:, PLAN_CHUNK - 1:PLAN_CHUNK]
    excl = jnp.concatenate(parts, axis=1) - onehot
    cnt0 = carry[0:E]
    tot = cnt0 + carry[E:2 * E]
    lower = (lax.broadcasted_iota(jnp.int32, (E, E), 1)
             < lax.broadcasted_iota(jnp.int32, (E, E), 0)).astype(F32)
    offs = jnp.dot(lower, jnp.broadcast_to(tot, (E, LANES)), preferred_element_type=F32,
                   precision=lax.Precision.HIGHEST)[:, 0:1]
    base = jnp.concatenate([offs, offs + cnt0], axis=0)
    val = onehot * (base + excl)
    dest_ref[...] = jnp.zeros(dest_ref.shape, jnp.int32)
    dest_ref[0:1, :] = jnp.sum(val[0:E], axis=0, keepdims=True).astype(jnp.int32)
    dest_ref[1:2, :] = jnp.sum(val[E:2 * E], axis=0, keepdims=True).astype(jnp.int32)
    cnt_ref[...] = jnp.broadcast_to(tot, (E, LANES)).astype(jnp.int32)


def moe_plan(idx):
    n_tok = idx.shape[1]
    assert n_tok % PLAN_CHUNK == 0 and TOP_K == 2
    return pl.pallas_call(
        functools.partial(_plan_kernel, n_tok=n_tok),
        out_shape=(jax.ShapeDtypeStruct((8, n_tok), jnp.int32),
                   jax.ShapeDtypeStruct((N_EXPERTS, LANES), jnp.int32)),
        compiler_params=_cparams(None),
        name="moe_plan",
    )(idx)


def _scatter_kernel(dest_ref, x_ref, xs_hbm, sem, *, tm, n_tok):
    base = pl.program_id(0) * tm

    def copies(start):
        def make(r, k, d):
            cp = pltpu.make_async_copy(x_ref.at[pl.ds(r, 1)], xs_hbm.at[pl.ds(d, 1)], sem)
            if start:
                cp.start(priority=k)
            else:
                cp.wait()
        _row_copies(dest_ref, n_tok, base, tm, make)

    copies(True)
    copies(False)


def moe_scatter(x, dest, *, tm=2048):
    N, D = x.shape
    grid_spec = pltpu.PrefetchScalarGridSpec(
        num_scalar_prefetch=1,
        grid=(N // tm,),
        in_specs=[pl.BlockSpec((tm, D), lambda i, d: (i, 0))],
        out_specs=pl.BlockSpec(memory_space=pl.ANY),
        scratch_shapes=[pltpu.SemaphoreType.DMA(())],
    )
    return pl.pallas_call(
        functools.partial(_scatter_kernel, tm=tm, n_tok=N),
        out_shape=jax.ShapeDtypeStruct((TOP_K * N, D), x.dtype),
        grid_spec=grid_spec,
        compiler_params=_cparams(("arbitrary",)),
        name="moe_scatter",
    )(dest, x)


def _moe_kernel(ti_ref, te_ref, lo_ref, hi_ref, first_ref, head_ref, slot_ref, nxt_ref,
                xs_ref, wg_hbm, wu_hbm, wd_hbm, y_ref,
                wg32_ref, wu32_ref, wd32_ref, wgb_ref, wub_ref, wdb_ref, sem, *, layer):
    i = pl.program_id(0)

    def weight_copies(e, slot):
        return [pltpu.make_async_copy(src.at[layer, e], dst.at[slot], sem.at[slot, n])
                for n, (src, dst) in enumerate(((wg_hbm, wg32_ref), (wu_hbm, wu32_ref), (wd_hbm, wd32_ref)))]

    @pl.when(i == 0)
    def _():
        for cp in weight_copies(te_ref[0], 0):
            cp.start()

    @pl.when(head_ref[i] == 1)
    def _():
        slot = slot_ref[i]
        for cp in weight_copies(te_ref[i], slot):
            cp.wait()
        wgb_ref[...] = wg32_ref[slot].astype(BF16)
        wub_ref[...] = wu32_ref[slot].astype(BF16)
        wdb_ref[...] = wd32_ref[slot].astype(BF16)

        @pl.when(nxt_ref[i] >= 0)
        def _():
            for cp in weight_copies(nxt_ref[i], 1 - slot):
                cp.start()

    lo, hi = lo_ref[i], hi_ref[i]

    @pl.when(hi > lo)
    def _():
        x = xs_ref[...].astype(BF16)
        g = _dot(x, wgb_ref[...])
        u = _dot(x, wub_ref[...])
        rows = lax.broadcasted_iota(jnp.int32, (x.shape[0], 1), 0)
        h = jnp.where((rows >= lo) & (rows < hi), (g * _sigmoid(g)) * u, 0.0)
        y = _dot(h.astype(BF16), wdb_ref[...])

        @pl.when(first_ref[i] == 1)
        def _():
            y_ref[...] = y

        @pl.when(first_ref[i] == 0)
        def _():
            y_ref[...] += y


def moe_experts(xs, items, w_gate, w_up, w_down, layer, *, tm=MOE_TM):
    P, D = xs.shape
    Fh = w_gate.shape[-1]
    n_items = items[0].shape[0]
    xmap = lambda i, ti, *_: (ti[i], 0)
    hbm = pl.BlockSpec(memory_space=pl.ANY)
    grid_spec = pltpu.PrefetchScalarGridSpec(
        num_scalar_prefetch=len(items),
        grid=(n_items,),
        in_specs=[pl.BlockSpec((tm, D), xmap), hbm, hbm, hbm],
        out_specs=pl.BlockSpec((tm, D), xmap),
        scratch_shapes=[pltpu.VMEM((2, D, Fh), F32), pltpu.VMEM((2, D, Fh), F32), pltpu.VMEM((2, Fh, D), F32),
                        pltpu.VMEM((D, Fh), BF16), pltpu.VMEM((D, Fh), BF16), pltpu.VMEM((Fh, D), BF16),
                        pltpu.SemaphoreType.DMA((2, 3))],
    )
    return pl.pallas_call(
        functools.partial(_moe_kernel, layer=layer),
        out_shape=jax.ShapeDtypeStruct((P, D), F32),
        grid_spec=grid_spec,
        compiler_params=_cparams(("arbitrary",)),
        name="moe_experts",
    )(*items, xs, w_gate, w_up, w_down)


def moe_items(tot, n_rows, tm=MOE_TM):
    E = N_EXPERTS
    n_max = n_rows // tm + E
    ar = jnp.arange(E, dtype=jnp.int32)
    ends = jnp.cumsum(tot)
    offs = ends - tot
    first_tile = offs // tm
    n_e = jnp.where(tot > 0, (ends - 1) // tm - first_tile + 1, 0)
    s_end = jnp.cumsum(n_e)
    s_beg = s_end - n_e
    n_items = s_end[-1]
    i = jnp.arange(n_max, dtype=jnp.int32)
    ic = jnp.minimum(i, n_items - 1)
    e_i = jnp.sum((ic[:, None] >= s_end[None, :]).astype(jnp.int32), axis=1)
    pick = (e_i[:, None] == ar[None, :]).astype(jnp.int32)
    at = lambda v: jnp.sum(pick * v[None, :], axis=1)
    tile = at(first_tile) + ic - at(s_beg)
    live = i < n_items
    lo = jnp.where(live, jnp.maximum(at(offs), tile * tm) - tile * tm, 0)
    hi = jnp.where(live, jnp.minimum(at(ends), tile * tm + tm) - tile * tm, 0)
    prev_tile = jnp.concatenate([jnp.full((1,), -1, jnp.int32), tile[:-1]])
    first = live & (tile != prev_tile)
    head = live & (ic == at(s_beg))
    used = (tot > 0).astype(jnp.int32)
    slot = at(jnp.cumsum(used) - used) & 1
    later = jnp.where((ar[None, :] > ar[:, None]) & (tot[None, :] > 0), ar[None, :], E)
    nxt_e = jnp.min(later, axis=1)
    nxt = at(jnp.where(nxt_e < E, nxt_e, -1))
    return tuple(v.astype(jnp.int32) for v in (tile, e_i, lo, hi, first, head, slot, nxt))


def _rope_tables(pos):
    inv = ROPE_THETA ** (-jnp.arange(0, ROPE_DIM, 2, dtype=F32) / ROPE_DIM)
    ang = pos.astype(F32)[:, None] * inv[None, :]
    cos, sin = jnp.cos(ang), jnp.sin(ang)
    n = pos.shape[0]
    rest = HEAD_DIM - ROPE_DIM
    cos_t = jnp.concatenate([cos, cos, jnp.ones((n, rest), F32)], axis=1)
    sin_lo = jnp.concatenate([-sin, jnp.zeros((n, HEAD_DIM - ROPE_HALF), F32)], axis=1)
    sin_hi = jnp.concatenate([jnp.zeros((n, ROPE_HALF), F32), sin, jnp.zeros((n, rest), F32)], axis=1)
    return cos_t, sin_lo, sin_hi


def _even_mixer(xb, B, T, w_in_all, j, pos, w1, w2, conv_w, conv_b, gn_g, gn_b):
    N, D = xb.shape
    G, HPG, dh = NSA_KV_HEADS, NSA_HPG, HEAD_DIM
    nsa_w = G * HPG * dh
    kv_cols = 3 * 2 * G * dh
    n_gate = 3 * G * HPG
    conv_c = (w_in_all.shape[2] - nsa_w - kv_cols - n_gate) // 2
    qkv = linear(xb, w_in_all, nsa_w + kv_cols, F32, layer=j, tm=512, tn=(nsa_w + kv_cols) // 2, name="even_qkv")
    w_gl = w_in_all[j, :, nsa_w + kv_cols:nsa_w + kv_cols + n_gate].reshape(D, G, 3 * HPG)
    w_gl = jnp.pad(w_gl, ((0, 0), (0, 0), (0, LANES - 3 * HPG))).reshape(D, G * LANES)
    gl = linear(xb, w_gl, G * LANES, F32, tn=G * LANES, name="even_gates")
    glu = linear(xb, w_in_all[j, :, nsa_w + kv_cols + n_gate:], 2 * conv_c, BF16, tn=1024, name="even_glu")

    qkv3 = qkv.reshape(B, T, nsa_w + kv_cols)
    n_cmp = (T - CMP_LEN) // CMP_STRIDE + 1
    nb = T // CMP_STRIDE
    cmp_end = jnp.arange(nb) * CMP_STRIDE + (CMP_LEN - 1)
    cmp_kv = compress(qkv3, pos, w1, w2, _rope_tables(cmp_end), B, T, G * HPG)
    o_nsa = nsa_attention(qkv3, cmp_kv, gl.reshape(B, T, G * LANES), _rope_tables(jnp.arange(T)), B, T)
    u = conv_module(glu.reshape(B, T, 2 * conv_c), conv_w, conv_b, gn_g, gn_b, B, T)
    return o_nsa.reshape(N, nsa_w), u.reshape(N, conv_c)


def _fox_mixer(xb, B, T, w_in_all, j, f_bias, q_gain, k_gain):
    N, D = xb.shape
    H = D // HEAD_DIM
    proj = linear(xb, w_in_all, 4 * D, BF16, layer=j, tm=512, tn=2048, name="fox_qkvg")
    w_f = jnp.pad(w_in_all[j, :, 4 * D:], ((0, 0), (0, LANES - H)))
    fl = linear(xb, w_f, LANES, F32, tn=LANES, name="fox_forget")
    c = fox_decay(fl, jnp.pad(f_bias, (0, LANES - H)), B, T)[:, :H].reshape(B, T, H)
    return fox_attention(proj.reshape(B, T, 4 * D), c, q_gain, k_gain, B, T, H).reshape(N, D)


def _moe(xf, router_w, router_bias, w_gate, w_up, w_down, layer):
    N, D = xf.shape
    idx, wts = router(xf, router_w, router_bias)
    dest8, cnt = moe_plan(idx)
    dest = dest8[:TOP_K].reshape(-1)
    xs = moe_scatter(xf, dest)
    y = moe_experts(xs, moe_items(cnt[:, 0], TOP_K * N), w_gate, w_up, w_down, layer)
    return y, dest, wts[:TOP_K].T


def kernel(x, even_w_in, even_w_out, nsa_cmp_pos, nsa_cmp_w1, nsa_cmp_w2, conv_w, conv_b, conv_gn_g, conv_gn_b, fox_w_in, fox_f_bias, fox_q_gain, fox_k_gain, fox_w_out, ln_mix_g, ln_mix_b, ln_ffn_g, ln_ffn_b, router_w, router_bias, exp_w_gate, exp_w_up, exp_w_down):
    B, T, D = x.shape
    depth = ln_mix_g.shape[0]
    alpha = (2.0 * depth) ** 0.25
    N = B * T
    xf = x.reshape(N, D)
    xb = xf
    for layer in range(depth):
        j = layer // 2
        if layer % 2 == 0:
            a1, a2 = _even_mixer(xb, B, T, even_w_in, j, nsa_cmp_pos[j], nsa_cmp_w1[j], nsa_cmp_w2[j],
                                 conv_w[j], conv_b[j], conv_gn_g[j], conv_gn_b[j])
            xf, xb = outproj_ln(a1, a2, 0, even_w_out, j, xf, ln_mix_g[layer], ln_mix_b[layer], alpha)
        else:
            a = _fox_mixer(xb, B, T, fox_w_in, j, fox_f_bias[j], fox_q_gain[j], fox_k_gain[j])
            xf, xb = outproj_ln(a, a, 1, fox_w_out, j, xf, ln_mix_g[layer], ln_mix_b[layer], alpha)
        y, dest, wts = _moe(xf, router_w, router_bias, exp_w_gate, exp_w_up, exp_w_down, layer)
        xf, xb = combine_ln(xf, y, dest, wts, ln_ffn_g[layer], ln_ffn_b[layer], alpha)
    return xf.reshape(B, T, D)
```

```python
import functools

import numpy as np
import jax
import jax.numpy as jnp
from jax import lax
from jax.experimental import pallas as pl
from jax.experimental.pallas import tpu as pltpu

F32 = jnp.float32
BF16 = jnp.bfloat16

HEAD_DIM = 128
ROPE_THETA = 500000.0
ROPE_DIM = HEAD_DIM // 4
ROPE_HALF = ROPE_DIM // 2
Q_BLOCK = 128

NSA_KV_HEADS = 2
NSA_HPG = 4
CMP_LEN = 32
CMP_STRIDE = 16
SLC_LEN = 64
N_SLC = 8
WIN = 512
CONV_KERNEL = 31
CONV_GROUPS = 8

N_EXPERTS = 16
N_EXPERT_GROUPS = 4
EXPERTS_PER_GROUP = 4
TOP_K = 2

LN_EPS = 1e-5
NEG = -1e30
BIG = 1e9
TINY = 1e-30

LANES = 128
V7X_VMEM_BYTES = 64 * 1024 * 1024
VMEM_LIMIT = V7X_VMEM_BYTES * 7 // 8

MOE_TM = 256


def _cparams(sem, vmem=VMEM_LIMIT):
    return pltpu.CompilerParams(dimension_semantics=sem, vmem_limit_bytes=vmem)


def _dot(a, b):
    return jnp.dot(a, b, preferred_element_type=F32)


def _dot_nt(a, b):
    return lax.dot_general(a, b, (((1,), (1,)), ((), ())), preferred_element_type=F32)


def _sigmoid(x):
    return 1.0 / (1.0 + jnp.exp(-x))


def _linear_kernel(x_ref, w_ref, o_ref, wb_ref):
    @pl.when(pl.program_id(1) == 0)
    def _():
        wb_ref[...] = w_ref[...].astype(BF16)

    o_ref[...] = _dot(x_ref[...].astype(BF16), wb_ref[...]).astype(o_ref.dtype)


def linear(x, w, n_cols, out_dtype, *, layer=None, tm=1024, tn=512, name="linear"):
    M, K = x.shape
    tm = min(tm, M)
    tn = min(tn, n_cols)
    assert M % tm == 0 and n_cols % tn == 0
    row_block = 0
    if layer is not None:
        w = w.reshape(-1, w.shape[-1])
        row_block = layer
    w_spec = pl.BlockSpec((K, tn), lambda j, i: (row_block, j))
    return pl.pallas_call(
        _linear_kernel,
        out_shape=jax.ShapeDtypeStruct((M, n_cols), out_dtype),
        grid=(n_cols // tn, M // tm),
        in_specs=[
            pl.BlockSpec((tm, K), lambda j, i: (i, 0)),
            w_spec,
        ],
        out_specs=pl.BlockSpec((tm, tn), lambda j, i: (i, j)),
        scratch_shapes=[pltpu.VMEM((K, tn), BF16)],
        compiler_params=_cparams(("arbitrary", "arbitrary")),
        name=name,
    )(x, w)


def _layer_norm_rows(y, g, b):
    mu = jnp.mean(y, axis=-1, keepdims=True)
    d = y - mu
    var = jnp.mean(d * d, axis=-1, keepdims=True)
    return d * lax.rsqrt(var + LN_EPS) * g + b


def _outproj_ln_kernel(a1_ref, a2_ref, w_ref, x_ref, g_ref, b_ref, xo_ref, xb_ref, wb_ref, *, alpha):
    @pl.when(pl.program_id(0) == 0)
    def _():
        wb_ref[...] = w_ref[...].astype(BF16)

    half = a1_ref.shape[1]
    h = _dot(a1_ref[...].astype(BF16), wb_ref[0:half, :])
    h = h + _dot(a2_ref[...].astype(BF16), wb_ref[half:2 * half, :])
    out = _layer_norm_rows(alpha * x_ref[...] + h, g_ref[...], b_ref[...])
    xo_ref[...] = out
    xb_ref[...] = out.astype(BF16)


def outproj_ln(a1, a2, a2_col_block, w, layer, x, g, b, alpha, *, tm=512):
    M, D = x.shape
    half = D // 2
    return pl.pallas_call(
        functools.partial(_outproj_ln_kernel, alpha=alpha),
        out_shape=(jax.ShapeDtypeStruct((M, D), F32), jax.ShapeDtypeStruct((M, D), BF16)),
        grid=(M // tm,),
        in_specs=[
            pl.BlockSpec((tm, half), lambda i: (i, 0)),
            pl.BlockSpec((tm, half), lambda i: (i, a2_col_block)),
            pl.BlockSpec((None, D, D), lambda i: (layer, 0, 0), pipeline_mode=pl.Buffered(1)),
            pl.BlockSpec((tm, D), lambda i: (i, 0)),
            pl.BlockSpec((1, D), lambda i: (0, 0)),
            pl.BlockSpec((1, D), lambda i: (0, 0)),
        ],
        out_specs=(pl.BlockSpec((tm, D), lambda i: (i, 0)), pl.BlockSpec((tm, D), lambda i: (i, 0))),
        scratch_shapes=[pltpu.VMEM((D, D), BF16)],
        compiler_params=_cparams(("arbitrary",)),
        name="outproj_ln",
    )(a1, a2, w, x, g.reshape(1, D), b.reshape(1, D))


ROW_DMA_UNROLL = 8


def _row_copies(dest_ref, n_tok, base, tm, make):
    def body(g, carry):
        r0 = pl.multiple_of(g * ROW_DMA_UNROLL, ROW_DMA_UNROLL)
        for u in range(ROW_DMA_UNROLL):
            for k in range(TOP_K):
                make(r0 + u, k, dest_ref[k * n_tok + base + r0 + u])
        return carry

    lax.fori_loop(0, tm // ROW_DMA_UNROLL, body, 0)


def _combine_ln_kernel(dest_ref, x_ref, w_ref, g_ref, b_ref, y_hbm, xo_ref, xb_ref, buf_ref, sem, *,
                       alpha, tm, n_tok):
    i = pl.program_id(0)
    n_steps = pl.num_programs(0)

    def copies(step, slot, start):
        def make(r, k, d):
            cp = pltpu.make_async_copy(y_hbm.at[pl.ds(d, 1)], buf_ref.at[slot, k, pl.ds(r, 1)], sem.at[slot])
            if start:
                cp.start(priority=k)
            else:
                cp.wait()
        _row_copies(dest_ref, n_tok, step * tm, tm, make)

    @pl.when(i == 0)
    def _():
        copies(0, 0, True)

    @pl.when(i + 1 < n_steps)
    def _():
        copies(i + 1, (i + 1) % 2, True)

    slot = i % 2
    copies(i, slot, False)
    f = w_ref[:, 0:1] * buf_ref[slot, 0] + w_ref[:, 1:2] * buf_ref[slot, 1]
    out = _layer_norm_rows(alpha * x_ref[...] + f, g_ref[...], b_ref[...])
    xo_ref[...] = out
    xb_ref[...] = out.astype(BF16)


def combine_ln(x, y, dest, wts, g, b, alpha, *, tm=256):
    N, D = x.shape
    row = lambda i, d: (i, 0)
    grid_spec = pltpu.PrefetchScalarGridSpec(
        num_scalar_prefetch=1,
        grid=(N // tm,),
        in_specs=[
            pl.BlockSpec((tm, D), row),
            pl.BlockSpec((tm, TOP_K), row),
            pl.BlockSpec((1, D), lambda i, d: (0, 0)),
            pl.BlockSpec((1, D), lambda i, d: (0, 0)),
            pl.BlockSpec(memory_space=pl.ANY),
        ],
        out_specs=(pl.BlockSpec((tm, D), row), pl.BlockSpec((tm, D), row)),
        scratch_shapes=[pltpu.VMEM((2, TOP_K, tm, D), F32), pltpu.SemaphoreType.DMA((2,))],
    )
    return pl.pallas_call(
        functools.partial(_combine_ln_kernel, alpha=alpha, tm=tm, n_tok=N),
        out_shape=(jax.ShapeDtypeStruct((N, D), F32), jax.ShapeDtypeStruct((N, D), BF16)),
        grid_spec=grid_spec,
        compiler_params=_cparams(("arbitrary",)),
        name="moe_combine_ln",
    )(dest, x, wts, g.reshape(1, D), b.reshape(1, D), y)


def _rms(x, gain):
    return x * lax.rsqrt(jnp.mean(x * x, axis=-1, keepdims=True) + LN_EPS) * gain


CUM_CHUNK = 256


def _decay_kernel(fl_ref, bias_ref, c_ref):
    T = fl_ref.shape[0]
    tri = (lax.broadcasted_iota(jnp.int32, (CUM_CHUNK, CUM_CHUNK), 0)
           >= lax.broadcasted_iota(jnp.int32, (CUM_CHUNK, CUM_CHUNK), 1)).astype(F32)
    carry = jnp.zeros((1, LANES), F32)
    for c in range(T // CUM_CHUNK):
        z = fl_ref[c * CUM_CHUNK:(c + 1) * CUM_CHUNK, :] + bias_ref[...]
        log_f = jnp.minimum(z, 0.0) - jnp.log1p(jnp.exp(-jnp.abs(z)))
        cs = jnp.dot(tri, log_f, preferred_element_type=F32, precision=lax.Precision.HIGHEST) + carry
        c_ref[c * CUM_CHUNK:(c + 1) * CUM_CHUNK, :] = cs
        carry = cs[CUM_CHUNK - 1:CUM_CHUNK, :]


def fox_decay(fl, bias, B, T):
    assert T % CUM_CHUNK == 0
    return pl.pallas_call(
        _decay_kernel,
        out_shape=jax.ShapeDtypeStruct(fl.shape, F32),
        grid=(B,),
        in_specs=[pl.BlockSpec((T, LANES), lambda b: (b, 0)), pl.BlockSpec((1, LANES), lambda b: (0, 0))],
        out_specs=pl.BlockSpec((T, LANES), lambda b: (b, 0)),
        compiler_params=_cparams(("arbitrary",)),
        name="fox_decay",
    )(fl, bias.reshape(1, LANES))


LOG2E = 1.4426950408889634
FAST_SOFTMAX_BOUND = 38.0
BIAS_PIECES = 3


def _split3(a):
    hi = a.astype(BF16).astype(F32)
    r = a - hi
    mid = r.astype(BF16).astype(F32)
    return hi, mid, (r - mid).astype(BF16).astype(F32)


def _aux_lanes(pieces, ones_first, n):
    lane = lax.broadcasted_iota(jnp.int32, (n, LANES), 1)
    p0 = BIAS_PIECES if ones_first else 0
    o0 = 0 if ones_first else BIAS_PIECES
    out = jnp.where((lane >= o0) & (lane < o0 + BIAS_PIECES), 1.0, 0.0)
    for i, piece in enumerate(pieces):
        out = out + jnp.where(lane == p0 + i, piece, 0.0)
    return out


def _fox_kernel(fast_ref, q_ref, k_ref, v_ref, og_ref, ckrow_ref, ckcol_ref, cq_ref, bound_ref, qg_ref, kg_ref,
                o_ref, kx_ref, vx_ref, *, tq, tk, hb):
    qi = pl.program_id(2)
    dh = HEAD_DIM
    T = k_ref.shape[1]

    def piece_by_lane(a, lane):
        hi, mid, lo = _split3(a)
        in_group = lambda g: ((lane >= g * hb) & (lane < (g + 1) * hb)) | (
            (lane >= (BIAS_PIECES + g) * hb) & (lane < (BIAS_PIECES + g + 1) * hb))
        return jnp.where(in_group(0), hi, jnp.where(in_group(1), mid, lo))

    @pl.when(qi == 0)
    def _():
        lane_k = lax.broadcasted_iota(jnp.int32, (T, LANES), 1)
        k_aux = jnp.where(lane_k < BIAS_PIECES * hb, piece_by_lane(-LOG2E * ckcol_ref[0, 0], lane_k),
                          jnp.where(lane_k < 2 * BIAS_PIECES * hb, 1.0, 0.0)).astype(BF16)
        for h in range(hb):
            kx_ref[h, :, 0:dh] = _rms(k_ref[0, :, h * dh:(h + 1) * dh].astype(F32), kg_ref[...]).astype(BF16)
            kx_ref[h, :, dh:2 * dh] = k_aux
            vx_ref[h, :, 0:dh] = v_ref[0, :, h * dh:(h + 1) * dh].astype(BF16)
            vx_ref[h, :, dh:2 * dh] = jnp.ones((T, dh), BF16)

    row = qi * tq + lax.broadcasted_iota(jnp.int32, (tq, tk), 0)
    lane = lax.broadcasted_iota(jnp.int32, (tq, tk), 1)
    n_full = (qi * tq) // tk

    def finish(accs):
        for h in range(hb):
            o = accs[h][:, 0:dh] / jnp.maximum(accs[h][:, dh:2 * dh], TINY)
            gate = _sigmoid(og_ref[0, :, h * dh:(h + 1) * dh].astype(F32))
            o_ref[0, :, h * dh:(h + 1) * dh] = (o * gate).astype(o_ref.dtype)

    def q_normed(h):
        return _rms(q_ref[0, :, h * dh:(h + 1) * dh].astype(F32), qg_ref[...]) * (dh ** -0.5)

    @pl.when(fast_ref[0] == 1)
    def _():
        lane_q = lax.broadcasted_iota(jnp.int32, (tq, LANES), 1)
        r = piece_by_lane(LOG2E * (cq_ref[0, 0] - bound_ref[0:1, 0:1]), lane_q)
        qx = []
        for h in range(hb):
            mine = (lane_q & (hb - 1)) == h
            q_aux = jnp.where(mine & (lane_q < BIAS_PIECES * hb), 1.0,
                              jnp.where(mine & (lane_q < 2 * BIAS_PIECES * hb), r, 0.0))
            qx.append(jnp.concatenate([(q_normed(h) * LOG2E).astype(BF16), q_aux.astype(BF16)], axis=1))

        def step(j, accs, masked):
            start = pl.multiple_of(j * tk, tk)
            out = []
            for h in range(hb):
                s = _dot_nt(qx[h], kx_ref[h, pl.ds(start, tk), :])
                if masked:
                    s = jnp.where(j * tk + lane <= row, s, NEG)
                out.append(accs[h] + _dot(jnp.exp2(s).astype(BF16), vx_ref[h, pl.ds(start, tk), :]))
            return tuple(out)

        def diagonal(accs):
            hq = tq // 2
            start = pl.multiple_of(n_full * tk, tk)
            tri = (lax.broadcasted_iota(jnp.int32, (hq, hq), 1) <= lax.broadcasted_iota(jnp.int32, (hq, hq), 0))
            out = []
            for h in range(hb):
                k_lo, k_hi = kx_ref[h, pl.ds(start, hq), :], kx_ref[h, pl.ds(start + hq, hq), :]
                v_lo, v_all = vx_ref[h, pl.ds(start, hq), :], vx_ref[h, pl.ds(start, tk), :]
                p_top = jnp.exp2(jnp.where(tri, _dot_nt(qx[h][0:hq], k_lo), NEG)).astype(BF16)
                p_bot = jnp.concatenate(
                    [jnp.exp2(_dot_nt(qx[h][hq:tq], k_lo)).astype(BF16),
                     jnp.exp2(jnp.where(tri, _dot_nt(qx[h][hq:tq], k_hi), NEG)).astype(BF16)], axis=1)
                out.append(accs[h] + jnp.concatenate([_dot(p_top, v_lo), _dot(p_bot, v_all)], axis=0))
            return out

        accs = tuple(jnp.zeros((tq, 2 * dh), F32) for _ in range(hb))
        accs = lax.fori_loop(0, n_full, lambda j, a: step(j, a, False), accs)
        finish(diagonal(accs) if tq == tk else step(n_full, accs, True))

    @pl.when(fast_ref[0] == 0)
    def _():
        qs = [q_normed(h).astype(BF16) for h in range(hb)]

        def step(j, carry, masked):
            start = pl.multiple_of(j * tk, tk)
            out = []
            for h in range(hb):
                m, acc = carry[h]
                s = _dot_nt(qs[h], kx_ref[h, pl.ds(start, tk), 0:dh]) - ckrow_ref[0, h, pl.ds(j, 1), :]
                if masked:
                    s = jnp.where(j * tk + lane <= row, s, NEG)
                m_new = jnp.maximum(m, jnp.max(s, axis=-1, keepdims=True))
                p = jnp.exp(s - m_new).astype(BF16)
                acc = jnp.exp(m - m_new) * acc + _dot(p, vx_ref[h, pl.ds(start, tk), :])
                out.append((m_new, acc))
            return tuple(out)

        init = tuple((jnp.full((tq, 1), NEG, F32), jnp.zeros((tq, 2 * dh), F32)) for _ in range(hb))
        carry = lax.fori_loop(0, n_full, lambda j, c: step(j, c, False), init)
        finish([c[1] for c in step(n_full, carry, True)])


def fox_attention(proj, c, q_gain, k_gain, B, T, H, *, tq=512, tk=512, hb=4):
    nq = T // tq
    nk = T // tk
    dh = HEAD_DIM
    hg = H // hb
    c_row = c.transpose(0, 2, 1).reshape(B, H, nk, tk)
    assert hb & (hb - 1) == 0 and 2 * BIAS_PIECES * hb <= LANES
    c_col = c.reshape(B, T, hg, hb).transpose(0, 2, 1, 3)
    c_col = jnp.concatenate([jnp.tile(c_col, (1, 1, 1, 2 * BIAS_PIECES)),
                             jnp.zeros((B, hg, T, LANES - 2 * BIAS_PIECES * hb), F32)], axis=-1)
    bound = jnp.max(jnp.abs(q_gain)) * jnp.max(jnp.abs(k_gain)) * (dh ** 0.5) * 1.01
    fast = (bound <= FAST_SOFTMAX_BOUND).astype(jnp.int32).reshape(1)
    full = lambda off: pl.BlockSpec((1, T, hb * dh), lambda b, h, i, f: (b, 0, off + h))
    tile = lambda off: pl.BlockSpec((1, tq, hb * dh), lambda b, h, i, f: (b, i, off + h))
    vec = pl.BlockSpec((1, dh), lambda b, h, i, f: (0, 0))
    grid_spec = pltpu.PrefetchScalarGridSpec(
        num_scalar_prefetch=1,
        grid=(B, hg, nq),
        in_specs=[
            tile(0), full(hg), full(2 * hg), tile(3 * hg),
            pl.BlockSpec((1, hb, nk, tk), lambda b, h, i, f: (b, h, 0, 0)),
            pl.BlockSpec((1, 1, T, LANES), lambda b, h, i, f: (b, h, 0, 0)),
            pl.BlockSpec((1, 1, tq, LANES), lambda b, h, i, f: (b, h, i, 0)),
            vec, vec, vec,
        ],
        out_specs=pl.BlockSpec((1, tq, hb * dh), lambda b, h, i, f: (b, i, h)),
        scratch_shapes=[pltpu.VMEM((hb, T, 2 * dh), BF16), pltpu.VMEM((hb, T, 2 * dh), BF16)],
    )
    return pl.pallas_call(
        functools.partial(_fox_kernel, tq=tq, tk=tk, hb=hb),
        out_shape=jax.ShapeDtypeStruct((B, T, H * dh), BF16),
        grid_spec=grid_spec,
        compiler_params=_cparams(("arbitrary", "arbitrary", "arbitrary")),
        name="fox_attention",
    )(fast, proj, proj, proj, proj, c_row, c_col, c_col, jnp.full((1, dh), bound, F32),
      q_gain.reshape(1, dh), k_gain.reshape(1, dh))


def _rope(x, cos, sin_lo, sin_hi):
    return (x * cos + pltpu.roll(x, LANES - ROPE_HALF, 1) * sin_lo
            + pltpu.roll(x, ROPE_HALF, 1) * sin_hi)


def _gelu_tanh(x):
    return 0.5 * x * (1.0 + jnp.tanh(0.7978845608028654 * (x + 0.044715 * (x * x * x))))


def _compress_kernel(raw_ref, pos_ref, w1_ref, w2_ref, cos_ref, slo_ref, shi_ref, o_ref, *, n_cmp):
    nb = o_ref.shape[-2]
    half = CMP_LEN // 2
    acc_a = jnp.zeros((nb, HEAD_DIM), F32)
    acc_b = jnp.zeros((nb, HEAD_DIM), F32)
    for l in range(half):
        rl = raw_ref[0, pl.ds(l, nb, stride=CMP_STRIDE), :]
        wa = w1_ref[0, l * HEAD_DIM:(l + 1) * HEAD_DIM, :].astype(BF16)
        wb = w1_ref[0, (half + l) * HEAD_DIM:(half + l + 1) * HEAD_DIM, :].astype(BF16)
        acc_a = acc_a + _dot((rl + pos_ref[0, l:l + 1, :]).astype(BF16), wa)
        acc_b = acc_b + _dot((rl + pos_ref[0, half + l:half + l + 1, :]).astype(BF16), wb)
    pre = acc_a + pltpu.roll(acc_b, nb - 1, 0)
    out = _dot(_gelu_tanh(pre).astype(BF16), w2_ref[0].astype(BF16))
    roped = _rope(out, cos_ref[...], slo_ref[...], shi_ref[...])
    out = jnp.where(pl.program_id(1) == 0, roped, out)
    rows = lax.broadcasted_iota(jnp.int32, out.shape, 0)
    o_ref[0, 0, 0] = jnp.where(rows < n_cmp, out, 0.0)


def compress(qkv, pos, w1, w2, tabs, B, T, kv_block0):
    G = NSA_KV_HEADS
    nb = T // CMP_STRIDE
    n_cmp = (T - CMP_LEN) // CMP_STRIDE + 1
    cos, slo, shi = tabs
    tab = pl.BlockSpec((nb, HEAD_DIM), lambda b, kv, g: (0, 0))
    return pl.pallas_call(
        functools.partial(_compress_kernel, n_cmp=n_cmp),
        out_shape=jax.ShapeDtypeStruct((B, 2, G, nb, HEAD_DIM), F32),
        grid=(B, 2, G),
        in_specs=[
            pl.BlockSpec((1, T, HEAD_DIM), lambda b, kv, g: (b, 0, kv_block0 + kv * G + g)),
            pl.BlockSpec((1, CMP_LEN, HEAD_DIM), lambda b, kv, g: (kv, 0, 0)),
            pl.BlockSpec((1, CMP_LEN * HEAD_DIM, HEAD_DIM), lambda b, kv, g: (kv, 0, 0)),
            pl.BlockSpec((1, HEAD_DIM, HEAD_DIM), lambda b, kv, g: (kv, 0, 0)),
            tab, tab, tab,
        ],
        out_specs=pl.BlockSpec((1, 1, 1, nb, HEAD_DIM), lambda b, kv, g: (b, kv, g, 0, 0)),
        compiler_params=_cparams(("arbitrary", "arbitrary", "arbitrary")),
        name="nsa_compress",
    )(qkv, pos, w1, w2, cos, slo, shi)


SLC_CHUNK = 512
WIN_SPAN = WIN + Q_BLOCK
NSA_QB = 4


def _select_blocks(imp, qi, n_slc):
    QB = Q_BLOCK
    n_rows = -(-n_slc // 8) * 8
    imp_t = jnp.transpose(imp)[0:n_rows]
    blk = lax.broadcasted_iota(jnp.int32, (n_rows, QB), 0)
    t = qi * QB + lax.broadcasted_iota(jnp.int32, (n_rows, QB), 1)
    cur = jnp.right_shift(t, 6)
    forced = (blk == 0) | (blk == cur) | (blk == cur - 1)
    score = jnp.where(forced, BIG, jnp.where(blk * SLC_LEN <= t, imp_t, -BIG))
    score = jnp.where(blk < n_slc, score, -2.0 * BIG)
    cnt = jnp.zeros((n_rows, QB), F32)
    for jp in range(n_slc):
        r = score[jp:jp + 1, :]
        cnt = cnt + ((r > score) | ((r == score) & (jp < blk))).astype(F32)
    sel_t = ((cnt < float(min(N_SLC, n_slc))) & (blk < n_slc)).astype(F32)
    if n_rows < LANES:
        sel_t = jnp.concatenate([sel_t, jnp.zeros((LANES - n_rows, QB), F32)], axis=0)
    return jnp.transpose(sel_t).astype(BF16)


def _nsa_kernel(q_ref, ks_ref, vs_ref, kw_ref, vw_ref, kc_ref, vc_ref, gl_ref,
                cq_ref, sloq_ref, shiq_ref, ck_ref, slok_ref, shik_ref, c2s_ref, ex_ref,
                o_ref, ksx_ref, vsx_ref, kwb_ref, vwx_ref, bias_ref, kmax_ref, acc_ref, *, n_slc, n_cmp):
    step = pl.program_id(2)
    QB, HPG, dh = Q_BLOCK, NSA_HPG, HEAD_DIM
    T = ks_ref.shape[1]

    @pl.when(step == 0)
    def _():
        ks = _rope(ks_ref[0], ck_ref[...], slok_ref[...], shik_ref[...])
        ksx_ref[:, 0:dh] = ks.astype(BF16)
        ksx_ref[:, dh:2 * dh] = _aux_lanes((), False, T).astype(BF16)
        kmax_ref[...] = jnp.full(kmax_ref.shape, jnp.max(jnp.sum(ks * ks, axis=-1, keepdims=True)), F32)
        kwb_ref[...] = _rope(kw_ref[0], ck_ref[...], slok_ref[...], shik_ref[...]).astype(BF16)
        ones = jnp.ones((T, dh), BF16)
        vsx_ref[:, 0:dh] = vs_ref[0].astype(BF16)
        vsx_ref[:, dh:2 * dh] = ones
        vwx_ref[:, 0:dh] = vw_ref[0].astype(BF16)
        vwx_ref[:, dh:2 * dh] = ones

    qis = [step * NSA_QB + b for b in range(NSA_QB)]
    rows = lambda b, h: slice((b * HPG + h) * QB, (b * HPG + h + 1) * QB)
    qrows = lambda b: slice(b * QB, (b + 1) * QB)
    pieces, norms2 = [], []
    for b in range(NSA_QB):
        for h in range(HPG):
            x = q_ref[0, qrows(b), h * dh:(h + 1) * dh]
            x = _rope(x, cq_ref[qrows(b), :], sloq_ref[qrows(b), :], shiq_ref[qrows(b), :]) * (dh ** -0.5)
            pieces.append(x)
            norms2.append(_dot((x * x).astype(BF16), jnp.ones((dh, LANES), BF16)))
    q = jnp.concatenate(pieces, axis=0).astype(BF16)

    lane = lax.broadcasted_iota(jnp.int32, (QB, LANES), 1)
    sub = lax.broadcasted_iota(jnp.int32, (QB, LANES), 0)

    kc = kc_ref[0, 0, 0].astype(BF16)
    vc = vc_ref[0, 0, 0].astype(BF16)
    sc = _dot_nt(q, kc)
    o_cmp, sels = {}, []
    for b in range(NSA_QB):
        t = qis[b] * QB + sub
        mask_c = (lane * CMP_STRIDE + (CMP_LEN - 1) <= t) & (lane < n_cmp)
        mask_cf = mask_c.astype(F32)
        imp_c = jnp.zeros((QB, LANES), F32)
        for h in range(HPG):
            s = jnp.where(mask_c, sc[rows(b, h)], NEG)
            e = jnp.exp(s - jnp.max(s, axis=-1, keepdims=True)) * mask_cf
            p = e / jnp.maximum(jnp.sum(e, axis=-1, keepdims=True), TINY)
            imp_c = imp_c + p
            o_cmp[b, h] = _dot(p.astype(BF16), vc)
        imp = jnp.dot(imp_c, c2s_ref[...], preferred_element_type=F32, precision=lax.Precision.HIGHEST)
        sels.append(_select_blocks(imp, qis[b], n_slc))

    sel_all = jnp.concatenate(sels, axis=0)
    for c in range(T // SLC_CHUNK):
        selx = _dot(sel_all, ex_ref[:, c * SLC_CHUNK:(c + 1) * SLC_CHUNK])
        kpos = c * SLC_CHUNK + lax.broadcasted_iota(jnp.int32, (QB, SLC_CHUNK), 1)
        for b in range(NSA_QB):
            tq_ = qis[b] * QB + lax.broadcasted_iota(jnp.int32, (QB, SLC_CHUNK), 0)
            bias_ref[b, c] = jnp.where((selx[qrows(b)] > 0.5) & (kpos <= tq_), 0.0, NEG)

    o_win = {}
    for b in range(NSA_QB):
        wstart = pl.multiple_of(jnp.clip(qis[b] * QB - WIN, 0, T - WIN_SPAN), Q_BLOCK)
        qb = q[b * HPG * QB:(b + 1) * HPG * QB]
        sw = _dot_nt(qb, kwb_ref[pl.ds(wstart, WIN_SPAN), :])
        kpos = wstart + lax.broadcasted_iota(jnp.int32, (QB, WIN_SPAN), 1)
        tw = qis[b] * QB + lax.broadcasted_iota(jnp.int32, (QB, WIN_SPAN), 0)
        bias_w = jnp.where((kpos <= tw) & (kpos > tw - WIN), 0.0, NEG)
        pw = []
        for h in range(HPG):
            sh = sw[h * QB:(h + 1) * QB] + bias_w
            pw.append(jnp.exp(sh - jnp.max(sh, axis=-1, keepdims=True)).astype(BF16))
        ow = _dot(jnp.concatenate(pw, axis=0), vwx_ref[pl.ds(wstart, WIN_SPAN), :])
        for h in range(HPG):
            acc_w = ow[h * QB:(h + 1) * QB]
            o_win[b, h] = acc_w[:, 0:dh] / jnp.maximum(acc_w[:, dh:2 * dh], TINY)

    chains = [(b, h) for b in range(NSA_QB) for h in range(HPG)]

    n_chunks = (qis[-1] * QB + QB + SLC_CHUNK - 1) // SLC_CHUNK
    kmax2 = kmax_ref[0:1, 0:1]
    bounds = [jnp.sqrt(n2 * kmax2) * 1.02 for n2 in norms2]
    worst = functools.reduce(jnp.maximum, [jnp.max(bd) for bd in bounds])

    @pl.when(worst <= FAST_SOFTMAX_BOUND)
    def _():
        qx = jnp.concatenate(
            [jnp.concatenate([(x * LOG2E).astype(BF16),
                              _aux_lanes(_split3(-LOG2E * bd), True, QB).astype(BF16)], axis=1)
             for x, bd in zip(pieces, bounds)], axis=0)

        def body(c, accs):
            start = pl.multiple_of(c * SLC_CHUNK, SLC_CHUNK)
            s = _dot_nt(qx, ksx_ref[pl.ds(start, SLC_CHUNK), :])
            ps = [jnp.exp2(s[rows(b, h)] + bias_ref[b, c]).astype(BF16) for (b, h) in chains]
            pv = _dot(jnp.concatenate(ps, axis=0), vsx_ref[pl.ds(start, SLC_CHUNK), :])
            return tuple(accs[n] + pv[rows(b, h)] for n, (b, h) in enumerate(chains))

        accs = lax.fori_loop(0, n_chunks, body, tuple(jnp.zeros((QB, 2 * dh), F32) for _ in chains))
        for n, (b, h) in enumerate(chains):
            acc_ref[rows(b, h), :] = accs[n]

    @pl.when(worst > FAST_SOFTMAX_BOUND)
    def _():
        def body(c, carry):
            start = pl.multiple_of(c * SLC_CHUNK, SLC_CHUNK)
            s = _dot_nt(q, ksx_ref[pl.ds(start, SLC_CHUNK), 0:dh])
            ms, alphas, ps = [], [], []
            for n, (b, h) in enumerate(chains):
                sh = s[rows(b, h)] + bias_ref[b, c]
                m_new = jnp.maximum(carry[n][0], jnp.max(sh, axis=-1, keepdims=True))
                alphas.append(jnp.exp(carry[n][0] - m_new))
                ps.append(jnp.exp(sh - m_new).astype(BF16))
                ms.append(m_new)
            pv = _dot(jnp.concatenate(ps, axis=0), vsx_ref[pl.ds(start, SLC_CHUNK), :])
            return tuple((ms[n], alphas[n] * carry[n][1] + pv[rows(b, h)]) for n, (b, h) in enumerate(chains))

        init = tuple((jnp.full((QB, 1), NEG, F32), jnp.zeros((QB, 2 * dh), F32)) for _ in chains)
        slc = lax.fori_loop(0, n_chunks, body, init)
        for n, (b, h) in enumerate(chains):
            acc_ref[rows(b, h), :] = slc[n][1]

    for b in range(NSA_QB):
        gates = _sigmoid(gl_ref[0, qrows(b), :])
        for h in range(HPG):
            acc_s = acc_ref[rows(b, h), :]
            o_s = acc_s[:, 0:dh] / jnp.maximum(acc_s[:, dh:2 * dh], TINY)
            out = (gates[:, 3 * h:3 * h + 1] * o_cmp[b, h] + gates[:, 3 * h + 1:3 * h + 2] * o_s
                   + gates[:, 3 * h + 2:3 * h + 3] * o_win[b, h])
            o_ref[0, qrows(b), h * dh:(h + 1) * dh] = out.astype(o_ref.dtype)


def nsa_attention(qkv, cmp_kv, gl, tabs_q, B, T):
    G, HPG, dh = NSA_KV_HEADS, NSA_HPG, HEAD_DIM
    tq = NSA_QB * Q_BLOCK
    nb = T // CMP_STRIDE
    n_cmp = (T - CMP_LEN) // CMP_STRIDE + 1
    n_slc = T // SLC_LEN
    assert nb == LANES and n_slc <= LANES and T % SLC_CHUNK == 0 and T >= WIN_SPAN and T % tq == 0
    kvb = (HPG * G)

    c_start = np.arange(nb) * CMP_STRIDE
    s_start = np.arange(LANES) * SLC_LEN
    c2s = ((c_start[:, None] < s_start[None, :] + SLC_LEN) & (c_start[:, None] + CMP_LEN > s_start[None, :])
           & (np.arange(nb)[:, None] < n_cmp) & (np.arange(LANES)[None, :] < n_slc)).astype(np.float32)
    expand = (np.arange(T)[None, :] // SLC_LEN == np.arange(LANES)[:, None]).astype(np.float32)

    cos, slo, shi = tabs_q
    kv_full = lambda blk: pl.BlockSpec((1, T, dh), lambda b, g, i: (b, 0, kvb + blk + g))
    qtab = pl.BlockSpec((tq, dh), lambda b, g, i: (i, 0))
    ktab = pl.BlockSpec((T, dh), lambda b, g, i: (0, 0))
    return pl.pallas_call(
        functools.partial(_nsa_kernel, n_slc=n_slc, n_cmp=n_cmp),
        out_shape=jax.ShapeDtypeStruct((B, T, G * HPG * dh), BF16),
        grid=(B, G, T // tq),
        in_specs=[
            pl.BlockSpec((1, tq, HPG * dh), lambda b, g, i: (b, i, g)),
            kv_full(2 * G), kv_full(3 * G), kv_full(4 * G), kv_full(5 * G),
            pl.BlockSpec((1, 1, 1, nb, dh), lambda b, g, i: (b, 0, g, 0, 0)),
            pl.BlockSpec((1, 1, 1, nb, dh), lambda b, g, i: (b, 1, g, 0, 0)),
            pl.BlockSpec((1, tq, LANES), lambda b, g, i: (b, i, g)),
            qtab, qtab, qtab, ktab, ktab, ktab,
            pl.BlockSpec((nb, LANES), lambda b, g, i: (0, 0)),
            pl.BlockSpec((LANES, T), lambda b, g, i: (0, 0)),
        ],
        out_specs=pl.BlockSpec((1, tq, HPG * dh), lambda b, g, i: (b, i, g)),
        scratch_shapes=[pltpu.VMEM((T, 2 * dh), BF16), pltpu.VMEM((T, 2 * dh), BF16),
                        pltpu.VMEM((T, dh), BF16), pltpu.VMEM((T, 2 * dh), BF16),
                        pltpu.VMEM((NSA_QB, T // SLC_CHUNK, Q_BLOCK, SLC_CHUNK), F32),
                        pltpu.VMEM((8, LANES), F32),
                        pltpu.VMEM((NSA_QB * HPG * Q_BLOCK, 2 * dh), F32)],
        compiler_params=_cparams(("arbitrary", "arbitrary", "arbitrary")),
        name="nsa_attention",
    )(qkv, qkv, qkv, qkv, qkv, cmp_kv, cmp_kv, gl, cos, slo, shi, cos, slo, shi,
      jnp.asarray(c2s), jnp.asarray(expand, dtype=BF16))


CONV_PAD = 32
CONV_CHUNK = 256


def _conv_kernel(a_ref, b_ref, w_ref, cb_ref, g_ref, gb_ref, o_ref, u_ref):
    T = a_ref.shape[1]
    u_ref[0:CONV_PAD, :] = jnp.zeros((CONV_PAD, LANES), F32)
    u_ref[CONV_PAD:CONV_PAD + T, :] = a_ref[0].astype(F32) * _sigmoid(b_ref[0].astype(F32))
    base = CONV_PAD - (CONV_KERNEL - 1)
    for c in range(T // CONV_CHUNK):
        t0 = c * CONV_CHUNK
        acc = jnp.zeros((CONV_CHUNK, LANES), F32)
        for k in range(CONV_KERNEL):
            acc = acc + u_ref[t0 + base + k:t0 + base + k + CONV_CHUNK, :] * w_ref[k:k + 1, :]
        acc = acc + cb_ref[...]
        mu = jnp.mean(acc, axis=-1, keepdims=True)
        d = acc - mu
        var = jnp.mean(d * d, axis=-1, keepdims=True)
        y = d * lax.rsqrt(var + LN_EPS) * g_ref[...] + gb_ref[...]
        o_ref[0, t0:t0 + CONV_CHUNK, :] = (y * _sigmoid(y)).astype(o_ref.dtype)


def conv_module(glu, conv_w, conv_b, gn_g, gn_b, B, T):
    C = glu.shape[-1] // 2
    ng = C // LANES
    assert C // CONV_GROUPS == LANES
    vec = pl.BlockSpec((1, LANES), lambda b, g: (0, g))
    return pl.pallas_call(
        _conv_kernel,
        out_shape=jax.ShapeDtypeStruct((B, T, C), BF16),
        grid=(B, ng),
        in_specs=[
            pl.BlockSpec((1, T, LANES), lambda b, g: (b, 0, g)),
            pl.BlockSpec((1, T, LANES), lambda b, g: (b, 0, ng + g)),
            pl.BlockSpec((CONV_KERNEL, LANES), lambda b, g: (0, g)),
            vec, vec, vec,
        ],
        out_specs=pl.BlockSpec((1, T, LANES), lambda b, g: (b, 0, g)),
        scratch_shapes=[pltpu.VMEM((CONV_PAD + T, LANES), F32)],
        compiler_params=_cparams(("arbitrary", "arbitrary")),
        name="conformer_conv",
    )(glu, glu, conv_w, conv_b.reshape(1, C), gn_g.reshape(1, C), gn_b.reshape(1, C))


def _split2(a):
    hi = a.astype(BF16)
    return hi, (a - hi.astype(F32)).astype(BF16)


def _router_kernel(x_ref, w_ref, bias_ref, idx_ref, wts_ref):
    xh, xl = _split2(x_ref[...])
    wh, wl = _split2(w_ref[...])
    logits = _dot(xh, wh) + (_dot(xh, wl) + _dot(xl, wh))
    aff = _sigmoid(jnp.concatenate(
        [jnp.transpose(logits[c * LANES:(c + 1) * LANES])[0:N_EXPERTS] for c in range(logits.shape[0] // LANES)],
        axis=1))
    sel = aff + bias_ref[...]
    a = [aff[e:e + 1, :] for e in range(N_EXPERTS)]
    s = [sel[e:e + 1, :] for e in range(N_EXPERTS)]
    P = EXPERTS_PER_GROUP
    grp = []
    for g in range(N_EXPERT_GROUPS):
        v = s[g * P:(g + 1) * P]
        best = None
        for i in range(P):
            for j in range(i + 1, P):
                pair = v[i] + v[j]
                best = pair if best is None else jnp.maximum(best, pair)
        grp.append(best)
    gbest = jnp.zeros_like(grp[0], dtype=jnp.int32)
    gval = grp[0]
    for g in range(1, N_EXPERT_GROUPS):
        better = grp[g] > gval
        gbest = jnp.where(better, g, gbest)
        gval = jnp.where(better, grp[g], gval)
    cs, ca = [], []
    for p in range(P):
        sv, av = s[p], a[p]
        for g in range(1, N_EXPERT_GROUPS):
            sv = jnp.where(gbest == g, s[g * P + p], sv)
            av = jnp.where(gbest == g, a[g * P + p], av)
        cs.append(sv)
        ca.append(av)
    i1 = jnp.zeros_like(gbest)
    v1, a1 = cs[0], ca[0]
    for p in range(1, P):
        better = cs[p] > v1
        i1 = jnp.where(better, p, i1)
        v1 = jnp.where(better, cs[p], v1)
        a1 = jnp.where(better, ca[p], a1)
    i2 = jnp.full_like(gbest, -1)
    v2 = jnp.full_like(v1, -jnp.inf)
    a2 = jnp.zeros_like(a1)
    for p in range(P):
        better = (i1 != p) & (cs[p] > v2)
        i2 = jnp.where(better, p, i2)
        v2 = jnp.where(better, cs[p], v2)
        a2 = jnp.where(better, ca[p], a2)
    den = a1 + a2
    idx_ref[...] = jnp.zeros(idx_ref.shape, jnp.int32)
    wts_ref[...] = jnp.zeros(wts_ref.shape, F32)
    idx_ref[0:1, :] = gbest * P + i1
    idx_ref[1:2, :] = gbest * P + i2
    wts_ref[0:1, :] = a1 / den
    wts_ref[1:2, :] = a2 / den


def router(x, router_w, router_bias, *, tm=1024):
    N, D = x.shape
    E = N_EXPERTS
    return pl.pallas_call(
        _router_kernel,
        out_shape=(jax.ShapeDtypeStruct((8, N), jnp.int32), jax.ShapeDtypeStruct((8, N), F32)),
        grid=(N // tm,),
        in_specs=[
            pl.BlockSpec((tm, D), lambda i: (i, 0)),
            pl.BlockSpec((D, LANES), lambda i: (0, 0)),
            pl.BlockSpec((E, 1), lambda i: (0, 0)),
        ],
        out_specs=(pl.BlockSpec((8, tm), lambda i: (0, i)), pl.BlockSpec((8, tm), lambda i: (0, i))),
        compiler_params=_cparams(("arbitrary",)),
        name="moe_router",
    )(x, jnp.pad(router_w, ((0, 0), (0, LANES - E))), router_bias.reshape(E, 1))


PLAN_CHUNK = 512


def _plan_kernel(idx_ref, dest_ref, cnt_ref, *, n_tok):
    E = N_EXPERTS
    sub = lax.broadcasted_iota(jnp.int32, (TOP_K * E, n_tok), 0)
    tgt = jnp.where(sub < E, idx_ref[0:1, :], idx_ref[1:2, :])
    onehot = ((sub & (E - 1)) == tgt).astype(F32)
    onehot_b = onehot.astype(BF16)
    tri = (lax.broadcasted_iota(jnp.int32, (PLAN_CHUNK, PLAN_CHUNK), 0)
           <= lax.broadcasted_iota(jnp.int32, (PLAN_CHUNK, PLAN_CHUNK), 1)).astype(F32).astype(BF16)
    carry = jnp.zeros((TOP_K * E, 1), F32)
    parts = []
    for c in range(n_tok // PLAN_CHUNK):
        pre = _dot(onehot_b[:, c * PLAN_CHUNK:(c + 1) * PLAN_CHUNK], tri) + carry
        parts.append(pre)
        carry = pre[:, PLAN_CHUNK - 1:PLAN_CHUNK]
    excl = jnp.concatenate(parts, axis=1) - onehot
    cnt0 = carry[0:E]
    tot = cnt0 + carry[E:2 * E]
    lower = (lax.broadcasted_iota(jnp.int32, (E, E), 1)
             < lax.broadcasted_iota(jnp.int32, (E, E), 0)).astype(F32)
    offs = jnp.dot(lower, jnp.broadcast_to(tot, (E, LANES)), preferred_element_type=F32,
                   precision=lax.Precision.HIGHEST)[:, 0:1]
    base = jnp.concatenate([offs, offs + cnt0], axis=0)
    val = onehot * (base + excl)
    dest_ref[...] = jnp.zeros(dest_ref.shape, jnp.int32)
    dest_ref[0:1, :] = jnp.sum(val[0:E], axis=0, keepdims=True).astype(jnp.int32)
    dest_ref[1:2, :] = jnp.sum(val[E:2 * E], axis=0, keepdims=True).astype(jnp.int32)
    cnt_ref[...] = jnp.broadcast_to(tot, (E, LANES)).astype(jnp.int32)


def moe_plan(idx):
    n_tok = idx.shape[1]
    assert n_tok % PLAN_CHUNK == 0 and TOP_K == 2
    return pl.pallas_call(
        functools.partial(_plan_kernel, n_tok=n_tok),
        out_shape=(jax.ShapeDtypeStruct((8, n_tok), jnp.int32),
                   jax.ShapeDtypeStruct((N_EXPERTS, LANES), jnp.int32)),
        compiler_params=_cparams(None),
        name="moe_plan",
    )(idx)


def _scatter_kernel(dest_ref, x_hbm, xs_hbm, sem, *, tm, n_tok):
    i = pl.program_id(0)

    def copies(step, start):
        base = step * tm

        def make(r, k, d):
            cp = pltpu.make_async_copy(x_hbm.at[pl.ds(base + r, 1)], xs_hbm.at[pl.ds(d, 1)], sem.at[step % 2])
            if start:
                cp.start(priority=k)
            else:
                cp.wait()
        _row_copies(dest_ref, n_tok, base, tm, make)

    copies(i, True)

    @pl.when(i > 0)
    def _():
        copies(i - 1, False)

    @pl.when(i == pl.num_programs(0) - 1)
    def _():
        copies(i, False)


def moe_scatter(x, dest, *, tm=1024):
    N, D = x.shape
    grid_spec = pltpu.PrefetchScalarGridSpec(
        num_scalar_prefetch=1,
        grid=(N // tm,),
        in_specs=[pl.BlockSpec(memory_space=pl.ANY)],
        out_specs=pl.BlockSpec(memory_space=pl.ANY),
        scratch_shapes=[pltpu.SemaphoreType.DMA((2,))],
    )
    return pl.pallas_call(
        functools.partial(_scatter_kernel, tm=tm, n_tok=N),
        out_shape=jax.ShapeDtypeStruct((TOP_K * N, D), x.dtype),
        grid_spec=grid_spec,
        compiler_params=_cparams(("arbitrary",)),
        name="moe_scatter",
    )(dest, x)


def _moe_kernel(ti_ref, te_ref, lo_ref, hi_ref, first_ref, head_ref, slot_ref, nxt_ref,
                xs_ref, wg_hbm, wu_hbm, wd_hbm, y_ref,
                wg32_ref, wu32_ref, wd32_ref, wgb_ref, wub_ref, wdb_ref, sem, *, layer):
    i = pl.program_id(0)

    def weight_copies(e, slot):
        return [pltpu.make_async_copy(src.at[layer, e], dst.at[slot], sem.at[slot, n])
                for n, (src, dst) in enumerate(((wg_hbm, wg32_ref), (wu_hbm, wu32_ref), (wd_hbm, wd32_ref)))]

    @pl.when(i == 0)
    def _():
        for cp in weight_copies(te_ref[0], 0):
            cp.start()

    @pl.when(head_ref[i] == 1)
    def _():
        slot = slot_ref[i]
        for cp in weight_copies(te_ref[i], slot):
            cp.wait()
        wgb_ref[...] = wg32_ref[slot].astype(BF16)
        wub_ref[...] = wu32_ref[slot].astype(BF16)
        wdb_ref[...] = wd32_ref[slot].astype(BF16)

        @pl.when(nxt_ref[i] >= 0)
        def _():
            for cp in weight_copies(nxt_ref[i], 1 - slot):
                cp.start()

    lo, hi = lo_ref[i], hi_ref[i]

    @pl.when(hi > lo)
    def _():
        x = xs_ref[...].astype(BF16)
        g = _dot(x, wgb_ref[...])
        u = _dot(x, wub_ref[...])
        rows = lax.broadcasted_iota(jnp.int32, (x.shape[0], 1), 0)
        h = jnp.where((rows >= lo) & (rows < hi), (g * _sigmoid(g)) * u, 0.0)
        y = _dot(h.astype(BF16), wdb_ref[...])

        @pl.when(first_ref[i] == 1)
        def _():
            y_ref[...] = y

        @pl.when(first_ref[i] == 0)
        def _():
            y_ref[...] += y


def moe_experts(xs, items, w_gate, w_up, w_down, layer, *, tm=MOE_TM):
    P, D = xs.shape
    Fh = w_gate.shape[-1]
    n_items = items[0].shape[0]
    xmap = lambda i, ti, *_: (ti[i], 0)
    hbm = pl.BlockSpec(memory_space=pl.ANY)
    grid_spec = pltpu.PrefetchScalarGridSpec(
        num_scalar_prefetch=len(items),
        grid=(n_items,),
        in_specs=[pl.BlockSpec((tm, D), xmap), hbm, hbm, hbm],
        out_specs=pl.BlockSpec((tm, D), xmap),
        scratch_shapes=[pltpu.VMEM((2, D, Fh), F32), pltpu.VMEM((2, D, Fh), F32), pltpu.VMEM((2, Fh, D), F32),
                        pltpu.VMEM((D, Fh), BF16), pltpu.VMEM((D, Fh), BF16), pltpu.VMEM((Fh, D), BF16),
                        pltpu.SemaphoreType.DMA((2, 3))],
    )
    return pl.pallas_call(
        functools.partial(_moe_kernel, layer=layer),
        out_shape=jax.ShapeDtypeStruct((P, D), F32),
        grid_spec=grid_spec,
        compiler_params=_cparams(("arbitrary",)),
        name="moe_experts",
    )(*items, xs, w_gate, w_up, w_down)


def moe_items(tot, n_rows, tm=MOE_TM):
    E = N_EXPERTS
    n_max = n_rows // tm + E
    ar = jnp.arange(E, dtype=jnp.int32)
    ends = jnp.cumsum(tot)
    offs = ends - tot
    first_tile = offs // tm
    n_e = jnp.where(tot > 0, (ends - 1) // tm - first_tile + 1, 0)
    s_end = jnp.cumsum(n_e)
    s_beg = s_end - n_e
    n_items = s_end[-1]
    i = jnp.arange(n_max, dtype=jnp.int32)
    ic = jnp.minimum(i, n_items - 1)
    e_i = jnp.sum((ic[:, None] >= s_end[None, :]).astype(jnp.int32), axis=1)
    pick = (e_i[:, None] == ar[None, :]).astype(jnp.int32)
    at = lambda v: jnp.sum(pick * v[None, :], axis=1)
    tile = at(first_tile) + ic - at(s_beg)
    live = i < n_items
    lo = jnp.where(live, jnp.maximum(at(offs), tile * tm) - tile * tm, 0)
    hi = jnp.where(live, jnp.minimum(at(ends), tile * tm + tm) - tile * tm, 0)
    prev_tile = jnp.concatenate([jnp.full((1,), -1, jnp.int32), tile[:-1]])
    first = live & (tile != prev_tile)
    head = live & (ic == at(s_beg))
    used = (tot > 0).astype(jnp.int32)
    slot = at(jnp.cumsum(used) - used) & 1
    later = jnp.where((ar[None, :] > ar[:, None]) & (tot[None, :] > 0), ar[None, :], E)
    nxt_e = jnp.min(later, axis=1)
    nxt = at(jnp.where(nxt_e < E, nxt_e, -1))
    return tuple(v.astype(jnp.int32) for v in (tile, e_i, lo, hi, first, head, slot, nxt))


def _rope_tables(pos):
    inv = ROPE_THETA ** (-jnp.arange(0, ROPE_DIM, 2, dtype=F32) / ROPE_DIM)
    ang = pos.astype(F32)[:, None] * inv[None, :]
    cos, sin = jnp.cos(ang), jnp.sin(ang)
    n = pos.shape[0]
    rest = HEAD_DIM - ROPE_DIM
    cos_t = jnp.concatenate([cos, cos, jnp.ones((n, rest), F32)], axis=1)
    sin_lo = jnp.concatenate([-sin, jnp.zeros((n, HEAD_DIM - ROPE_HALF), F32)], axis=1)
    sin_hi = jnp.concatenate([jnp.zeros((n, ROPE_HALF), F32), sin, jnp.zeros((n, rest), F32)], axis=1)
    return cos_t, sin_lo, sin_hi


def _even_mixer(xb, B, T, w_in_all, j, pos, w1, w2, conv_w, conv_b, gn_g, gn_b):
    N, D = xb.shape
    G, HPG, dh = NSA_KV_HEADS, NSA_HPG, HEAD_DIM
    nsa_w = G * HPG * dh
    kv_cols = 3 * 2 * G * dh
    n_gate = 3 * G * HPG
    conv_c = (w_in_all.shape[2] - nsa_w - kv_cols - n_gate) // 2
    qkv = linear(xb, w_in_all, nsa_w + kv_cols, F32, layer=j, tm=512, tn=(nsa_w + kv_cols) // 2, name="even_qkv")
    w_gl = w_in_all[j, :, nsa_w + kv_cols:nsa_w + kv_cols + n_gate].reshape(D, G, 3 * HPG)
    w_gl = jnp.pad(w_gl, ((0, 0), (0, 0), (0, LANES - 3 * HPG))).reshape(D, G * LANES)
    gl = linear(xb, w_gl, G * LANES, F32, tn=G * LANES, name="even_gates")
    glu = linear(xb, w_in_all[j, :, nsa_w + kv_cols + n_gate:], 2 * conv_c, BF16, tn=1024, name="even_glu")

    qkv3 = qkv.reshape(B, T, nsa_w + kv_cols)
    n_cmp = (T - CMP_LEN) // CMP_STRIDE + 1
    nb = T // CMP_STRIDE
    cmp_end = jnp.arange(nb) * CMP_STRIDE + (CMP_LEN - 1)
    cmp_kv = compress(qkv3, pos, w1, w2, _rope_tables(cmp_end), B, T, G * HPG)
    o_nsa = nsa_attention(qkv3, cmp_kv, gl.reshape(B, T, G * LANES), _rope_tables(jnp.arange(T)), B, T)
    u = conv_module(glu.reshape(B, T, 2 * conv_c), conv_w, conv_b, gn_g, gn_b, B, T)
    return o_nsa.reshape(N, nsa_w), u.reshape(N, conv_c)


def _fox_mixer(xb, B, T, w_in_all, j, f_bias, q_gain, k_gain):
    N, D = xb.shape
    H = D // HEAD_DIM
    proj = linear(xb, w_in_all, 4 * D, BF16, layer=j, tm=512, tn=2048, name="fox_qkvg")
    w_f = jnp.pad(w_in_all[j, :, 4 * D:], ((0, 0), (0, LANES - H)))
    fl = linear(xb, w_f, LANES, F32, tn=LANES, name="fox_forget")
    c = fox_decay(fl, jnp.pad(f_bias, (0, LANES - H)), B, T)[:, :H].reshape(B, T, H)
    return fox_attention(proj.reshape(B, T, 4 * D), c, q_gain, k_gain, B, T, H).reshape(N, D)


def _moe(xf, router_w, router_bias, w_gate, w_up, w_down, layer):
    N, D = xf.shape
    idx, wts = router(xf, router_w, router_bias)
    dest8, cnt = moe_plan(idx)
    dest = dest8[:TOP_K].reshape(-1)
    xs = moe_scatter(xf, dest)
    y = moe_experts(xs, moe_items(cnt[:, 0], TOP_K * N), w_gate, w_up, w_down, layer)
    return y, dest, wts[:TOP_K].T


def kernel(x, even_w_in, even_w_out, nsa_cmp_pos, nsa_cmp_w1, nsa_cmp_w2, conv_w, conv_b, conv_gn_g, conv_gn_b, fox_w_in, fox_f_bias, fox_q_gain, fox_k_gain, fox_w_out, ln_mix_g, ln_mix_b, ln_ffn_g, ln_ffn_b, router_w, router_bias, exp_w_gate, exp_w_up, exp_w_down):
    B, T, D = x.shape
    depth = ln_mix_g.shape[0]
    alpha = (2.0 * depth) ** 0.25
    N = B * T
    xf = x.reshape(N, D)
    xb = xf
    for layer in range(depth):
        j = layer // 2
        if layer % 2 == 0:
            a1, a2 = _even_mixer(xb, B, T, even_w_in, j, nsa_cmp_pos[j], nsa_cmp_w1[j], nsa_cmp_w2[j],
                                 conv_w[j], conv_b[j], conv_gn_g[j], conv_gn_b[j])
            xf, xb = outproj_ln(a1, a2, 0, even_w_out, j, xf, ln_mix_g[layer], ln_mix_b[layer], alpha)
        else:
            a = _fox_mixer(xb, B, T, fox_w_in, j, fox_f_bias[j], fox_q_gain[j], fox_k_gain[j])
            xf, xb = outproj_ln(a, a, 1, fox_w_out, j, xf, ln_mix_g[layer], ln_mix_b[layer], alpha)
        y, dest, wts = _moe(xf, router_w, router_bias, exp_w_gate, exp_w_up, exp_w_down, layer)
        xf, xb = combine_ln(xf, y, dest, wts, ln_ffn_g[layer], ln_ffn_b[layer], alpha)
    return xf.reshape(B, T, D)
```
